```python
import numpy as np
import jax, jax.numpy as jnp
from jax import lax

D_MODEL = 1024
BATCH = 4
SEQ = 8192
DEPTH = 1
DEC_BATCH = 32
DEC_SEQ = 4
PAST_LEN = 16384
PAGE_SIZE = 128

HEAD_DIM = 64
N_HEADS_A = 8
N_KV_A = 2
G_A = N_HEADS_A // N_KV_A
N_IDX_HEADS = 8
IDX_DIM = 64
DSA_TOPK = 256
N_HEADS_B = 8
N_KV_B = 2
G_B = N_HEADS_B // N_KV_B
CMP_LEN = 32
CMP_STRIDE = 16
CMP_HID = 2 * HEAD_DIM
SLC_BLOCK = 64
SLC_TOPN = 16
WINDOW = 512
D_FF = 2816
CONV_W = 3
ROPE_THETA = 10000.0
Q_BLOCK = 128
LN_EPS = 1e-5
DN_ALPHA = (2 * DEPTH) ** 0.25
DN_BETA = (8 * DEPTH) ** -0.25
WIDTH_A = N_HEADS_A * HEAD_DIM
WIDTH_B = N_HEADS_B * HEAD_DIM
NEG = -1e30
BIG = 1e30
SPLITS = (WIDTH_A, 2 * N_KV_A * HEAD_DIM, N_IDX_HEADS * IDX_DIM, IDX_DIM, N_IDX_HEADS,
          WIDTH_B, 6 * N_KV_B * HEAD_DIM, 3 * N_HEADS_B, D_MODEL, D_MODEL)
N_IN = sum(SPLITS)

kernel_name = 'hybrid_dsa_nsa_convffn_step'


def _rope(x, pos):
    half = x.shape[-1] // 2
    inv = ROPE_THETA ** (-jnp.arange(half, dtype=jnp.float32) / half)
    ang = pos.astype(jnp.float32)[:, None] * inv[None, :]
    cos = jnp.cos(ang)[None, :, None, :].astype(x.dtype)
    sin = jnp.sin(ang)[None, :, None, :].astype(x.dtype)
    x1, x2 = x[..., :half], x[..., half:]
    return jnp.concatenate([x1 * cos - x2 * sin, x1 * sin + x2 * cos], axis=-1)


def _layernorm(x, g, b):
    xf = x.astype(jnp.float32)
    mu = xf.mean(-1, keepdims=True)
    var = jnp.square(xf - mu).mean(-1, keepdims=True)
    y = (xf - mu) * lax.rsqrt(var + LN_EPS) * g.astype(jnp.float32) + b.astype(jnp.float32)
    return y.astype(x.dtype)


def _masked_softmax(s, mask):
    p = jax.nn.softmax(jnp.where(mask, s.astype(jnp.float32), NEG), axis=-1)
    return jnp.where(mask, p, 0.0)


def _project(x, c, pos, w_ada, b_ada, w_in):
    B, T, _ = x.shape
    ada = jax.nn.silu(c) @ w_ada + b_ada
    shift1, scale1 = ada[:, :D_MODEL], ada[:, D_MODEL:2 * D_MODEL]
    h = x * (1 + scale1[:, None]) + shift1[:, None]
    cuts = [int(v) for v in np.cumsum(SPLITS)[:-1]]
    qa, kva, qi, ki, wi, qb, kvb, gb, mga, mgb = jnp.split(h @ w_in, cuts, axis=-1)
    qa = _rope(qa.reshape(B, T, N_HEADS_A, HEAD_DIM), pos).reshape(B, T, N_KV_A, G_A, HEAD_DIM)
    kva = kva.reshape(B, T, 2, N_KV_A, HEAD_DIM)
    kva = jnp.stack([_rope(kva[:, :, 0], pos), kva[:, :, 1]], axis=2)
    qi = _rope(qi.reshape(B, T, N_IDX_HEADS, IDX_DIM), pos)
    ki = _rope(ki.reshape(B, T, 1, IDX_DIM), pos)[:, :, 0]
    wi = wi * (N_IDX_HEADS * IDX_DIM) ** -0.5
    qb = _rope(qb.reshape(B, T, N_HEADS_B, HEAD_DIM), pos).reshape(B, T, N_KV_B, G_B, HEAD_DIM)
    kvb = kvb.reshape(B, T, 3, 2, N_KV_B, HEAD_DIM)
    kb = _rope(kvb[:, :, :, 0].reshape(B, T, 3 * N_KV_B, HEAD_DIM), pos).reshape(B, T, 3, N_KV_B, HEAD_DIM)
    kvb = jnp.stack([kb, kvb[:, :, :, 1]], axis=3)
    gb = jax.nn.sigmoid(gb).reshape(B, T, 3, N_KV_B, G_B)
    return (ada, qa, kva, qi, ki, wi, qb, kvb[:, :, 0], kvb[:, :, 1], kvb[:, :, 2], gb, mga, mgb)


def _dsa_core(q, qi, wi, ki_all, q_pos, fetch_kv):
    L = ki_all.shape[1]
    rel = jax.nn.relu(jnp.einsum('bqhd,bld->bqhl', qi, ki_all).astype(jnp.float32))
    score = jnp.einsum('bqhl,bqh->bql', rel, wi.astype(jnp.float32))
    admissible = jnp.arange(L)[None, :] <= q_pos[:, None]
    score = jnp.where(admissible[None], score, NEG)
    _, idx = lax.top_k(score, min(DSA_TOPK, L // 4))
    valid = idx <= q_pos[None, :, None]
    kv = fetch_kv(idx)
    s = jnp.einsum('bqhgd,bqkhd->bqhgk', q, kv[:, :, :, 0]) * HEAD_DIM ** -0.5
    p = _masked_softmax(s, valid[:, :, None, None, :])
    o = jnp.einsum('bqhgk,bqkhd->bqhgd', p.astype(q.dtype), kv[:, :, :, 1])
    return o.reshape(q.shape[0], q.shape[1], WIDTH_A)


def _compress(kv, pe, w1, b1, w2, b2):
    B, L = kv.shape[:2]
    n_ch = -(-L // CMP_STRIDE)
    kv = jnp.pad(kv, ((0, 0), (0, n_ch * CMP_STRIDE - L), (0, 0), (0, 0), (0, 0)))
    ch = kv.reshape(B, n_ch, CMP_STRIDE, 2, N_KV_B, HEAD_DIM)
    ha = jnp.einsum('bncjhd,jcdf->bnjhf', ch, w1[:, :CMP_STRIDE])
    hb = jnp.einsum('bncjhd,jcdf->bnjhf', ch, w1[:, CMP_STRIDE:])
    hpe = jnp.einsum('jcd,jcdf->jf', pe, w1) + b1
    hid = jax.nn.gelu(ha[:, :-1] + hb[:, 1:] + hpe[:, None, :])
    out = jnp.einsum('bnjhf,jfd->bnjhd', hid, w2) + b2[:, None, :]
    ends = jnp.arange(n_ch - 1) * CMP_STRIDE + CMP_LEN - 1
    return out, ends


def _slc_map(n_cmp, n_slc):
    st = np.arange(n_cmp) * CMP_STRIDE
    bs = np.arange(n_slc) * SLC_BLOCK
    m = (st[:, None] < bs[None, :] + SLC_BLOCK) & (st[:, None] + CMP_LEN > bs[None, :])
    return jnp.asarray(m.astype(np.float32))


def _nsa_core(q, q_pos, gates, kc, vc, ends, m_cs, fetch_slc, win_kv, win_pos):
    B, Q = q.shape[:2]
    scale = HEAD_DIM ** -0.5
    s = jnp.einsum('bqhgd,bnhd->bqhgn', q, kc) * scale
    vis = ends[None, :] <= q_pos[:, None]
    p = _masked_softmax(s, vis[None, :, None, None, :])
    o_cmp = jnp.einsum('bqhgn,bnhd->bqhgd', p.astype(q.dtype), vc)
    n_slc = m_cs.shape[1]
    imp = jnp.einsum('bqhgn,nj->bqhj', p, m_cs)
    blk = jnp.arange(n_slc)[None, :]
    cur = (q_pos // SLC_BLOCK)[:, None]
    forced = (blk == 0) | (blk == cur) | (blk == cur - 1)
    future = blk * SLC_BLOCK > q_pos[:, None]
    imp = jnp.where(forced[None, :, None], BIG, imp)
    imp = jnp.where(future[None, :, None], NEG, imp)
    _, sel = lax.top_k(imp, min(SLC_TOPN, n_slc))
    pos = sel[..., None] * SLC_BLOCK + jnp.arange(SLC_BLOCK)
    rows = fetch_slc(pos)
    n_k = pos.shape[3] * SLC_BLOCK
    ks = rows[..., 0, :].reshape(B, Q, N_KV_B, n_k, HEAD_DIM)
    vs = rows[..., 1, :].reshape(B, Q, N_KV_B, n_k, HEAD_DIM)
    s = jnp.einsum('bqhgd,bqhkd->bqhgk', q, ks) * scale
    ok = (pos.reshape(B, Q, N_KV_B, n_k) <= q_pos[None, :, None, None])[:, :, :, None, :]
    p = _masked_softmax(s, ok)
    o_slc = jnp.einsum('bqhgk,bqhkd->bqhgd', p.astype(q.dtype), vs)
    s = jnp.einsum('bqhgd,bkhd->bqhgk', q, win_kv[:, :, 0]) * scale
    dist = q_pos[:, None] - win_pos[None, :]
    inwin = (dist >= 0) & (dist <= WINDOW) & (win_pos[None, :] >= 0)
    p = _masked_softmax(s, inwin[None, :, None, None, :])
    o_win = jnp.einsum('bqhgk,bkhd->bqhgd', p.astype(q.dtype), win_kv[:, :, 1])
    o = (gates[:, :, 0, :, :, None] * o_cmp + gates[:, :, 1, :, :, None] * o_slc
         + gates[:, :, 2, :, :, None] * o_win)
    return o.reshape(B, Q, WIDTH_B)


def _finish(x, ada, oa, ob, mga, mgb, g_hist, w_pa, w_pb, w_o, ln1_g, ln1_b,
            w_ffn_gate, w_ffn_up, conv_w, conv_b, w_down, ln2_g, ln2_b):
    D = D_MODEL
    gate1 = ada[:, 2 * D:3 * D][:, None]
    shift2 = ada[:, 3 * D:4 * D][:, None]
    scale2 = ada[:, 4 * D:5 * D][:, None]
    gate2 = ada[:, 5 * D:][:, None]
    m = jax.nn.sigmoid(mga) * (oa @ w_pa) + jax.nn.sigmoid(mgb) * (ob @ w_pb)
    x1 = _layernorm(DN_ALPHA * x + (1 + gate1) * (m @ w_o), ln1_g, ln1_b)
    h2 = x1 * (1 + scale2) + shift2
    u_g = h2 @ w_ffn_gate
    u = h2 @ w_ffn_up
    T = x.shape[1]
    gp = jnp.concatenate([g_hist.astype(u_g.dtype), u_g], axis=1)
    conv = conv_b + conv_w[CONV_W - 1] * gp[:, CONV_W - 1:CONV_W - 1 + T]
    for j in range(CONV_W - 1):
        conv = conv + conv_w[j] * gp[:, j:j + T]
    f = (jax.nn.gelu(conv) * u) @ w_down
    y = _layernorm(DN_ALPHA * x1 + (1 + gate2) * f, ln2_g, ln2_b)
    return y, gp[:, T:]


def _gather_rows(pool, page_table, new_rows, pos):
    past = page_table.shape[1] * PAGE_SIZE
    n_new = new_rows.shape[1]

    def one(pt, new, p):
        pc = jnp.clip(p, 0, past - 1)
        old = pool[pt[pc // PAGE_SIZE], pc % PAGE_SIZE]
        nw = new[jnp.clip(p - past, 0, n_new - 1)]
        is_new = (p >= past).reshape(p.shape + (1,) * (old.ndim - p.ndim))
        return jnp.where(is_new, nw, old)
    return jax.vmap(one)(page_table, new_rows, pos)


def _gather_rows_head(pool, page_table, new_rows, pos):
    past = page_table.shape[1] * PAGE_SIZE
    n_new = new_rows.shape[1]

    def one(pt, new, p):
        hd = jnp.arange(pool.shape[3])[:, None, None]
        pc = jnp.clip(p, 0, past - 1)
        old = pool[pt[pc // PAGE_SIZE], pc % PAGE_SIZE, :, hd]
        nw = new[jnp.clip(p - past, 0, n_new - 1), :, hd]
        return jnp.where((p >= past)[..., None, None], nw, old)
    return jax.vmap(one)(page_table, new_rows, pos)


def setup_inputs(seed: int = 0) -> dict:
    key = jax.random.key(seed)
    ks = jax.random.split(key, 32)
    n_pages = PAST_LEN // PAGE_SIZE
    n_phys = (5 * DEC_BATCH * n_pages) // 4
    win_buf = min(WINDOW, PAST_LEN)
    f32 = jnp.float32

    def nrm(k, shape, s):
        return jax.random.normal(k, shape, f32) * s
    D = D_MODEL
    perm = jax.random.permutation(ks[8], n_phys)[:DEC_BATCH * n_pages]
    return {
        'x_prompt': nrm(ks[0], (BATCH, SEQ, D), 1.0),
        'x_sample': nrm(ks[1], (DEC_BATCH, DEC_SEQ, D), 1.0),
        'cache_a_kv': nrm(ks[2], (n_phys, PAGE_SIZE, 2, N_KV_A, HEAD_DIM), 1.0),
        'cache_a_idx': nrm(ks[3], (n_phys, PAGE_SIZE, IDX_DIM), 1.0),
        'cache_b_cmp_kv': nrm(ks[4], (n_phys, PAGE_SIZE, 2, N_KV_B, HEAD_DIM), 1.0),
        'cache_b_slc_kv': nrm(ks[5], (n_phys, PAGE_SIZE, 2, N_KV_B, HEAD_DIM), 1.0),
        'state_b_win_kv': nrm(ks[6], (DEC_BATCH, win_buf, 2, N_KV_B, HEAD_DIM), 1.0),
        'state_ffn_conv': nrm(ks[7], (DEC_BATCH, CONV_W - 1, D_FF), 1.0),
        'page_table': perm.reshape(DEC_BATCH, n_pages).astype(jnp.int32),
        'c_prompt': nrm(ks[9], (BATCH, D), 1.0),
        'c_sample': nrm(ks[10], (DEC_BATCH, D), 1.0),
        'w_ada': nrm(ks[11], (D, 6 * D), 0.3 * D ** -0.5),
        'b_ada': nrm(ks[12], (6 * D,), 0.01),
        'w_in': nrm(ks[13], (D, N_IN), D ** -0.5),
        'cmp_pe': nrm(ks[14], (2, CMP_LEN, HEAD_DIM), 0.02),
        'cmp_w1': nrm(ks[15], (2, CMP_LEN, HEAD_DIM, CMP_HID), (CMP_LEN * HEAD_DIM) ** -0.5),
        'cmp_b1': nrm(ks[16], (2, CMP_HID), 0.01),
        'cmp_w2': nrm(ks[17], (2, CMP_HID, HEAD_DIM), CMP_HID ** -0.5),
        'cmp_b2': nrm(ks[18], (2, HEAD_DIM), 0.01),
        'w_pa': nrm(ks[19], (WIDTH_A, D), WIDTH_A ** -0.5),
        'w_pb': nrm(ks[20], (WIDTH_B, D), WIDTH_B ** -0.5),
        'w_o': nrm(ks[21], (D, D), DN_BETA * D ** -0.5),
        'ln1_g': 1.0 + nrm(ks[22], (D,), 0.02),
        'ln1_b': nrm(ks[23], (D,), 0.02),
        'w_ffn_gate': nrm(ks[24], (D, D_FF), D ** -0.5),
        'w_ffn_up': nrm(ks[25], (D, D_FF), D ** -0.5),
        'conv_w': nrm(ks[26], (CONV_W, D_FF), CONV_W ** -0.5),
        'conv_b': nrm(ks[27], (D_FF,), 0.01),
        'w_down': nrm(ks[28], (D_FF, D), DN_BETA * D_FF ** -0.5),
        'ln2_g': 1.0 + nrm(ks[29], (D,), 0.02),
        'ln2_b': nrm(ks[30], (D,), 0.02),
    }


def reference(x_prompt, x_sample, cache_a_kv, cache_a_idx, cache_b_cmp_kv, cache_b_slc_kv,
              state_b_win_kv, state_ffn_conv, page_table, c_prompt, c_sample,
              w_ada, b_ada, w_in, cmp_pe, cmp_w1, cmp_b1, cmp_w2, cmp_b2,
              w_pa, w_pb, w_o, ln1_g, ln1_b, w_ffn_gate, w_ffn_up, conv_w, conv_b,
              w_down, ln2_g, ln2_b):
    fin = (w_pa, w_pb, w_o, ln1_g, ln1_b, w_ffn_gate, w_ffn_up, conv_w, conv_b, w_down, ln2_g, ln2_b)
    B, S, _ = x_prompt.shape
    pos_p = jnp.arange(S)
    (ada_p, qa, kva, qi, ki, wi, qb, kv_cmp, kv_slc, kv_win, gb, mga, mgb) = _project(
        x_prompt, c_prompt, pos_p, w_ada, b_ada, w_in)
    cmp_p, ends_p = _compress(kv_cmp, cmp_pe, cmp_w1, cmp_b1, cmp_w2, cmp_b2)
    m_cs_p = _slc_map(cmp_p.shape[1], -(-S // SLC_BLOCK))
    win_pad = jnp.pad(kv_win, ((0, 0), (WINDOW, 0), (0, 0), (0, 0), (0, 0)))
    hd_idx = jnp.arange(N_KV_B)[:, None, None]

    def fetch_a_p(idx):
        return jax.vmap(lambda a, i: a[i])(kva, idx)

    def fetch_b_p(pos):
        return jax.vmap(lambda a, r: a[r, :, hd_idx])(kv_slc, jnp.clip(pos, 0, S - 1))

    def block(bi):
        t0 = bi * Q_BLOCK
        qp = t0 + jnp.arange(Q_BLOCK)
        sl = lambda a: lax.dynamic_slice_in_dim(a, t0, Q_BLOCK, axis=1)
        oa = _dsa_core(sl(qa), sl(qi), sl(wi), ki, qp, fetch_a_p)
        wkv = lax.dynamic_slice_in_dim(win_pad, t0, WINDOW + Q_BLOCK, axis=1)
        wpos = t0 - WINDOW + jnp.arange(WINDOW + Q_BLOCK)
        ob = _nsa_core(sl(qb), qp, sl(gb), cmp_p[:, :, 0], cmp_p[:, :, 1], ends_p, m_cs_p,
                       fetch_b_p, wkv, wpos)
        return oa, ob
    oa_blk, ob_blk = lax.map(block, jnp.arange(S // Q_BLOCK))
    oa_p = jnp.moveaxis(oa_blk, 0, 1).reshape(B, S, WIDTH_A)
    ob_p = jnp.moveaxis(ob_blk, 0, 1).reshape(B, S, WIDTH_B)
    y_prompt, conv_p = _finish(x_prompt, ada_p, oa_p, ob_p, mga, mgb,
                               jnp.zeros((B, CONV_W - 1, D_FF), x_prompt.dtype), *fin)
    win_p = kv_win[:, -min(WINDOW, S):]

    Bd, Tn, _ = x_sample.shape
    past = page_table.shape[1] * PAGE_SIZE
    L = past + Tn
    pos_s = past + jnp.arange(Tn)
    (ada_s, qa_s, kva_s, qi_s, ki_s, wi_s, qb_s, kv_cmp_s, kv_slc_s, kv_win_s, gb_s, mga_s, mgb_s) = _project(
        x_sample, c_sample, pos_s, w_ada, b_ada, w_in)
    ki_all = jnp.concatenate([cache_a_idx[page_table].reshape(Bd, past, IDX_DIM), ki_s], axis=1)
    oa_s = _dsa_core(qa_s, qi_s, wi_s, ki_all, pos_s,
                     lambda idx: _gather_rows(cache_a_kv, page_table, kva_s, idx))
    cmp_rows = cache_b_cmp_kv[page_table].reshape((Bd, past) + cache_b_cmp_kv.shape[2:])
    cmp_s, ends_s = _compress(jnp.concatenate([cmp_rows, kv_cmp_s], axis=1),
                              cmp_pe, cmp_w1, cmp_b1, cmp_w2, cmp_b2)
    m_cs_s = _slc_map(cmp_s.shape[1], -(-L // SLC_BLOCK))
    wb = state_b_win_kv.shape[1]
    win_all = jnp.concatenate([state_b_win_kv, kv_win_s], axis=1)
    wpos_s = past - wb + jnp.arange(wb + Tn)
    ob_s = _nsa_core(qb_s, pos_s, gb_s, cmp_s[:, :, 0], cmp_s[:, :, 1], ends_s, m_cs_s,
                     lambda pos: _gather_rows_head(cache_b_slc_kv, page_table, kv_slc_s, pos),
                     win_all, wpos_s)
    y_sample, conv_s = _finish(x_sample, ada_s, oa_s, ob_s, mga_s, mgb_s, state_ffn_conv, *fin)
    win_s = win_all[:, -min(WINDOW, wb + Tn):]

    return (y_prompt, y_sample, kva, kva_s, ki, ki_s, kv_cmp, kv_cmp_s, kv_slc, kv_slc_s,
            win_p, win_s, conv_p, conv_s)
```

```python
import functools
import math

import numpy as np
import jax
import jax.numpy as jnp
from jax import lax
from jax.experimental import pallas as pl
from jax.experimental.pallas import tpu as pltpu

D_MODEL = 1024
PAGE_SIZE = 128
HEAD_DIM = 64
N_HEADS_A = 8
N_KV_A = 2
N_IDX_HEADS = 8
IDX_DIM = 64
DSA_TOPK = 256
N_HEADS_B = 8
N_KV_B = 2
G_B = N_HEADS_B // N_KV_B
CMP_LEN = 32
CMP_STRIDE = 16
CMP_HID = 2 * HEAD_DIM
SLC_BLOCK = 64
SLC_TOPN = 16
WINDOW = 512
D_FF = 2816
CONV_W = 3
ROPE_THETA = 10000.0
LN_EPS = 1e-5
DEPTH = 1
DN_ALPHA = (2 * DEPTH) ** 0.25
NEG = -1e30
BIG = 1e30
MASK_VALUE = -2e30

LANES = 128
VMEM_LIMIT_BYTES = 56 * 1024 * 1024
Q_TILE = 128
KEY_CHUNK = 512
PAGES_PER_STEP = 16
PROJ_ROWS = 512
FINISH_ROWS = 256

F32 = jnp.float32
BF16 = jnp.bfloat16
I32 = jnp.int32
INT_MIN = -2 ** 31

_O_QA = 0
_O_KVA = _O_QA + N_HEADS_A * HEAD_DIM
_O_QI = _O_KVA + 2 * N_KV_A * HEAD_DIM
_O_KI = _O_QI + N_IDX_HEADS * IDX_DIM
_O_WI = _O_KI + IDX_DIM
_O_QB = _O_WI + N_IDX_HEADS
_O_KVB = _O_QB + N_HEADS_B * HEAD_DIM
_O_GB = _O_KVB + 6 * N_KV_B * HEAD_DIM
_O_MGA = _O_GB + 3 * N_HEADS_B
_O_MGB = _O_MGA + D_MODEL
_N_IN = _O_MGB + D_MODEL
_GATE_LANE0 = N_IDX_HEADS


def _cparams(n_axes):
    return pltpu.CompilerParams(dimension_semantics=("arbitrary",) * n_axes,
                                vmem_limit_bytes=VMEM_LIMIT_BYTES)


def _const_spec(shape):
    nd = len(shape)
    return pl.BlockSpec(shape, lambda *a: (0,) * nd, pipeline_mode=pl.Buffered(1))


def _dot(a, b):
    return jnp.dot(a, b, preferred_element_type=F32)


def _dot_nt(a, b):
    return lax.dot_general(a, b, (((1,), (1,)), ((), ())), preferred_element_type=F32)


def _sigmoid(x):
    return 1.0 / (1.0 + jnp.exp(-x))


def _gelu_tanh(x):
    return 0.5 * x * (1.0 + jnp.tanh(math.sqrt(2.0 / math.pi) * (x + 0.044715 * (x * x * x))))


def _layernorm(x, g, b):
    mu = jnp.mean(x, axis=-1, keepdims=True)
    xc = x - mu
    var = jnp.mean(xc * xc, axis=-1, keepdims=True)
    return xc * lax.rsqrt(var + LN_EPS) * g + b


def _ada_kernel(c_ref, w_ref, b_ref, o_ref):
    c = c_ref[...]
    s = (c * _sigmoid(c)).astype(BF16)
    o_ref[...] = _dot(s, w_ref[...]) + b_ref[...]


def _ada(c_all, w_ada_bf, b_ada):
    rows = c_all.shape[0]
    n_out = w_ada_bf.shape[1]
    tn = D_MODEL
    return pl.pallas_call(
        _ada_kernel,
        grid=(n_out // tn,),
        in_specs=[pl.BlockSpec((rows, D_MODEL), lambda j: (0, 0)),
                  pl.BlockSpec((D_MODEL, tn), lambda j: (0, j)),
                  pl.BlockSpec((1, tn), lambda j: (0, j))],
        out_specs=pl.BlockSpec((rows, tn), lambda j: (0, j)),
        out_shape=jax.ShapeDtypeStruct((rows, n_out), F32),
        compiler_params=_cparams(1),
        name="ada",
    )(c_all, w_ada_bf, b_ada.reshape(1, n_out))


_P_QA, _P_QI, _P_QB = 0, 512, 1024
_P_K = 1536
_P_V = 2176
_P_MISC = 2688
_P_MG = 2816
_P_END = _P_MG + 2 * D_MODEL


def _proj_perm():
    def paired_heads(base):
        cols = []
        for g in range(4):
            cols += list(range(base + g * HEAD_DIM, base + (g + 1) * HEAD_DIM))
            cols += list(range(base + (4 + g) * HEAD_DIM, base + (5 + g) * HEAD_DIM))
        return cols
    kv = 2 * N_KV_B * HEAD_DIM
    half = N_KV_B * HEAD_DIM
    cols = []
    cols += paired_heads(_O_QA)
    cols += list(range(_O_QI, _O_QI + N_IDX_HEADS * IDX_DIM))
    cols += paired_heads(_O_QB)
    cols += list(range(_O_KVA, _O_KVA + N_KV_A * HEAD_DIM))
    cols += list(range(_O_KI, _O_KI + IDX_DIM)) * 2
    for r in range(3):
        cols += list(range(_O_KVB + r * kv, _O_KVB + r * kv + half))
    cols += list(range(_O_KVA + N_KV_A * HEAD_DIM, _O_KVA + 2 * N_KV_A * HEAD_DIM))
    for r in range(3):
        cols += list(range(_O_KVB + r * kv + half, _O_KVB + (r + 1) * kv))
    misc = list(range(_O_WI, _O_WI + N_IDX_HEADS)) + list(range(_O_GB, _O_GB + 3 * N_HEADS_B))
    cols += misc + [-1] * (LANES - len(misc))
    cols += list(range(_O_MGA, _O_MGA + 2 * D_MODEL))
    assert len(cols) == _P_END
    return np.asarray(cols, np.int32)


def _proj_kernel(x_ref, mod_ref, cos_ref, sin_ref, w_ref,
                 qa_o, qi_o, qb_o, kva_o, kvab_o, ki_o, kib_o, cmp_o, slc_o, slcb_o,
                 win_o, winb_o, misc_o, mg_o):
    mod = mod_ref[0]
    shift1 = mod[:, :D_MODEL]
    scale1 = mod[:, D_MODEL:]
    h = (x_ref[...] * (1.0 + scale1) + shift1).astype(BF16)
    tm = h.shape[0]
    cos = cos_ref[...]
    sin = sin_ref[...]
    lane = lax.broadcasted_iota(I32, (tm, LANES), 1)
    first_half = (lane & (HEAD_DIM - 1)) < (HEAD_DIM // 2)

    def rope(y):
        partner = jnp.where(first_half, pltpu.roll(y, LANES - HEAD_DIM // 2, 1),
                            pltpu.roll(y, HEAD_DIM // 2, 1))
        return y * cos + partner * sin

    def mm(a, b):
        return _dot(h, w_ref[:, a:b])

    q_scale = HEAD_DIM ** -0.5
    for off, out, scale in ((_P_QA, qa_o, q_scale), (_P_QI, qi_o, 1.0), (_P_QB, qb_o, q_scale)):
        y = mm(off, off + 4 * LANES)
        for j in range(4):
            out[:, j * LANES:(j + 1) * LANES] = (rope(y[:, j * LANES:(j + 1) * LANES]) * scale).astype(BF16)

    yk = mm(_P_K, _P_K + 5 * LANES)
    ka, ki2, kc, ks, kw = [rope(yk[:, j * LANES:(j + 1) * LANES]) for j in range(5)]
    yv = mm(_P_V, _P_V + 4 * LANES)
    va, vc, vs, vw = [yv[:, j * LANES:(j + 1) * LANES] for j in range(4)]

    kva_o[:, :LANES] = ka
    kva_o[:, LANES:] = va
    kvab_o[:, :LANES] = ka.astype(BF16)
    kvab_o[:, LANES:] = va.astype(BF16)
    ki_o[...] = ki2[:, :IDX_DIM]
    kib_o[...] = ki2.astype(BF16)
    cmp_o[:, :LANES] = kc
    cmp_o[:, LANES:] = vc
    slc_o[:, :LANES] = ks
    slc_o[:, LANES:] = vs
    slcb_o[:, :LANES] = ks.astype(BF16)
    slcb_o[:, LANES:] = vs.astype(BF16)
    win_o[:, :LANES] = kw
    win_o[:, LANES:] = vw
    winb_o[:, :LANES] = kw.astype(BF16)
    winb_o[:, LANES:] = vw.astype(BF16)

    ym = mm(_P_MISC, _P_MISC + LANES)
    wi_scale = (N_IDX_HEADS * IDX_DIM) ** -0.5
    misc_o[...] = jnp.where(lane < N_IDX_HEADS, ym * wi_scale, _sigmoid(ym))
    for j in range(2):
        a = _P_MG + j * D_MODEL
        mg_o[:, j * D_MODEL:(j + 1) * D_MODEL] = _sigmoid(mm(a, a + D_MODEL)).astype(BF16)


def _project(x_flat, mod, cos_t, sin_t, w_perm, tm, tiles_per_group, tiles_per_table):
    rows = x_flat.shape[0]
    n_tiles = rows // tm
    r_mod = mod.shape[1]

    def row_spec(width):
        return pl.BlockSpec((tm, width), lambda i: (i, 0))

    out_defs = [(512, BF16), (512, BF16), (512, BF16),
                (256, F32), (256, BF16),
                (IDX_DIM, F32), (LANES, BF16),
                (256, F32),
                (256, F32), (256, BF16),
                (256, F32), (256, BF16),
                (LANES, F32),
                (2 * D_MODEL, BF16)]
    return pl.pallas_call(
        _proj_kernel,
        grid=(n_tiles,),
        in_specs=[row_spec(D_MODEL),
                  pl.BlockSpec((1, r_mod, 2 * D_MODEL), lambda i: (i // tiles_per_group, 0, 0)),
                  pl.BlockSpec((tm, LANES), lambda i: (i % tiles_per_table, 0)),
                  pl.BlockSpec((tm, LANES), lambda i: (i % tiles_per_table, 0)),
                  _const_spec((D_MODEL, _P_END))],
        out_specs=[row_spec(w) for w, _ in out_defs],
        out_shape=[jax.ShapeDtypeStruct((rows, w), dt) for w, dt in out_defs],
        compiler_params=_cparams(1),
        name="proj",
    )(x_flat, mod, cos_t, sin_t, w_perm)


def _rope_tables(pos):
    half = HEAD_DIM // 2
    inv = ROPE_THETA ** (-jnp.arange(half, dtype=F32) / half)
    ang = pos.astype(F32)[:, None] * inv[None, :]
    cos = jnp.tile(jnp.cos(ang), (1, LANES // half))
    sign = np.where((np.arange(LANES) % HEAD_DIM) < half, -1.0, 1.0).astype(np.float32)
    sin = jnp.tile(jnp.sin(ang), (1, LANES // half)) * sign[None, :]
    return cos, sin


def _gather_kernel(pt_ref, *refs, n_groups):
    pages = refs[:PAGES_PER_STEP]
    new_ref = refs[PAGES_PER_STEP]
    out_ref = refs[PAGES_PER_STEP + 1]
    g = pl.program_id(1)

    @pl.when(g < n_groups)
    def _():
        for i in range(PAGES_PER_STEP):
            out_ref[0, i * PAGE_SIZE:(i + 1) * PAGE_SIZE, :] = pages[i][0].astype(BF16)

    @pl.when(g == n_groups)
    def _():
        out_ref[0] = jnp.zeros(out_ref.shape[1:], BF16)
        out_ref[0, :new_ref.shape[1], :] = new_ref[0].astype(BF16)


def _gather_pages(cache, page_table, new_rows):
    n_phys, _, w = cache.shape
    b, n_pages = page_table.shape
    assert n_pages % PAGES_PER_STEP == 0
    n_groups = n_pages // PAGES_PER_STEP
    rows_step = PAGES_PER_STEP * PAGE_SIZE

    def page_spec(i):
        def imap(bi, g, pt):
            gg = jnp.minimum(g, n_groups - 1)
            return (pt[bi, gg * PAGES_PER_STEP + i], 0, 0)
        return pl.BlockSpec((1, PAGE_SIZE, w), imap)

    grid_spec = pltpu.PrefetchScalarGridSpec(
        num_scalar_prefetch=1,
        grid=(b, n_groups + 1),
        in_specs=[page_spec(i) for i in range(PAGES_PER_STEP)]
        + [pl.BlockSpec((1, new_rows.shape[1], w), lambda bi, g, pt: (bi, 0, 0))],
        out_specs=pl.BlockSpec((1, rows_step, w), lambda bi, g, pt: (bi, g, 0)),
    )
    return pl.pallas_call(
        functools.partial(_gather_kernel, n_groups=n_groups),
        grid_spec=grid_spec,
        out_shape=jax.ShapeDtypeStruct((b, (n_groups + 1) * rows_step, w), BF16),
        compiler_params=_cparams(2),
        name="gather_pages",
    )(page_table, *([cache] * PAGES_PER_STEP), new_rows)


def _cmp_kernel(x_ref, wa_ref, wb_ref, pea_ref, peb_ref, b1_ref, w2_ref, b2_ref, o_ref):
    x = x_ref[0].astype(BF16)
    r = x.shape[0]
    ha = _dot(x, wa_ref[...])
    hb = _dot(x, wb_ref[...])
    hpe = _dot(pea_ref[...], wa_ref[...]) + _dot(peb_ref[...], wb_ref[...])
    hpe = hpe[0:1, :] + b1_ref[...]
    hb_next = pltpu.roll(hb, r - 1, 0)
    hid = _gelu_tanh(ha + hb_next + hpe).astype(BF16)
    o_ref[0] = (_dot(hid, w2_ref[...]) + b2_ref[...]).astype(o_ref.dtype)


def _compress_weights(cmp_pe, cmp_w1, cmp_b1, cmp_w2, cmp_b2):
    eye = jnp.eye(2, dtype=F32)
    kvw = 2 * N_KV_B * HEAD_DIM

    def big(w1h):
        t = jnp.einsum('jcdf,jJ,hH->cjhdJHf', w1h, eye, eye)
        return t.reshape(CMP_STRIDE * kvw, 2 * N_KV_B * CMP_HID).astype(BF16)

    def pe_row(peh):
        t = jnp.broadcast_to(peh.transpose(1, 0, 2)[:, :, None, :], (CMP_STRIDE, 2, N_KV_B, HEAD_DIM))
        row = t.reshape(1, CMP_STRIDE * kvw)
        return jnp.concatenate([row, jnp.zeros((15, CMP_STRIDE * kvw), F32)], axis=0).astype(BF16)

    wa = big(cmp_w1[:, :CMP_STRIDE])
    wb = big(cmp_w1[:, CMP_STRIDE:])
    pea = pe_row(cmp_pe[:, :CMP_STRIDE])
    peb = pe_row(cmp_pe[:, CMP_STRIDE:])
    b1 = jnp.broadcast_to(cmp_b1[:, None, :], (2, N_KV_B, CMP_HID)).reshape(1, -1)
    w2 = jnp.einsum('jfd,jJ,hH->jhfJHd', cmp_w2, eye, eye).reshape(2 * N_KV_B * CMP_HID, kvw).astype(BF16)
    b2 = jnp.broadcast_to(cmp_b2[:, None, :], (2, N_KV_B, HEAD_DIM)).reshape(1, -1)
    return wa, wb, pea, peb, b1, w2, b2


def _compress(x_chunks, cw):
    b, r, cwid = x_chunks.shape
    wa, wb, pea, peb, b1, w2, b2 = cw
    return pl.pallas_call(
        _cmp_kernel,
        grid=(b,),
        in_specs=[pl.BlockSpec((1, r, cwid), lambda i: (i, 0, 0)),
                  _const_spec(wa.shape), _const_spec(wb.shape), _const_spec(pea.shape),
                  _const_spec(peb.shape), _const_spec(b1.shape), _const_spec(w2.shape),
                  _const_spec(b2.shape)],
        out_specs=pl.BlockSpec((1, r, w2.shape[1]), lambda i: (i, 0, 0)),
        out_shape=jax.ShapeDtypeStruct((b, r, w2.shape[1]), BF16),
        compiler_params=_cparams(1),
        name="compress",
    )(x_chunks, wa, wb, pea, peb, b1, w2, b2)


def _stack_group_queries(q_ref, tq):
    lane = lax.broadcasted_iota(I32, (tq, LANES), 1)
    low = lane < HEAD_DIM
    q = q_ref[0].astype(F32)
    cols = [q[:, g * LANES:(g + 1) * LANES] for g in range(4)]
    q0 = jnp.concatenate([jnp.where(low, c, 0.0) for c in cols], axis=0).astype(BF16)
    q1 = jnp.concatenate([jnp.where(low, 0.0, c) for c in cols], axis=0).astype(BF16)
    return (q0, q1), low


def _flash_init(m_scr, l_scr, acc_scr):
    m_scr[...] = jnp.full(m_scr.shape, NEG, F32)
    l_scr[...] = jnp.zeros(l_scr.shape, F32)
    acc_scr[...] = jnp.zeros(acc_scr.shape, F32)


def _flash_step(qh, k128, v128, bias, m_scr, l_scr, acc_scr, h, tq):
    s = _dot_nt(qh, k128)
    ps, alphas = [], []
    for g in range(4):
        idx = h * 4 + g
        sg = s[g * tq:(g + 1) * tq] + bias
        m_prev = m_scr[idx][:, :1]
        l_prev = l_scr[idx][:, :1]
        m_new = jnp.maximum(m_prev, jnp.max(sg, axis=1, keepdims=True))
        alpha = jnp.exp(m_prev - m_new)
        p = jnp.exp(sg - m_new)
        l_new = alpha * l_prev + jnp.sum(p, axis=1, keepdims=True)
        m_scr[idx] = jnp.broadcast_to(m_new, (tq, LANES))
        l_scr[idx] = jnp.broadcast_to(l_new, (tq, LANES))
        ps.append(p)
        alphas.append(alpha)
    pv = _dot(jnp.concatenate(ps, axis=0).astype(BF16), v128)
    for g in range(4):
        idx = h * 4 + g
        acc_scr[idx] = alphas[g] * acc_scr[idx] + pv[g * tq:(g + 1) * tq]


def _flash_result(l_scr, acc_scr, idx):
    l = l_scr[idx][:, :1]
    return jnp.where(l > 0.0, acc_scr[idx] / l, 0.0)


def _softmax_rows(sg):
    m = jnp.maximum(jnp.max(sg, axis=1, keepdims=True), NEG)
    p = jnp.exp(sg - m)
    l = jnp.sum(p, axis=1, keepdims=True)
    return jnp.where(l > 0.0, p / l, 0.0)


def _dsa_kernel(qi_ref, qa_ref, misc_ref, ki_ref, kv_ref, o_ref,
                key_scr, jsel_scr, m_scr, l_scr, acc_scr, *, tq, kc, q_pos0, topk, n_chunks):
    q0 = q_pos0 + pl.program_id(1) * tq
    row = lax.broadcasted_iota(I32, (tq, 1), 0)
    qpos = q0 + row
    nkc = jnp.minimum(lax.div(q0 + tq - 1, kc) + 1, n_chunks)
    lane = lax.broadcasted_iota(I32, (tq, LANES), 1)
    low = lane < IDX_DIM
    col_idx = lax.broadcasted_iota(I32, (tq, kc), 1)

    qi = qi_ref[0].astype(F32)
    misc = misc_ref[0]
    q_heads, w_heads = [], []
    for hh in range(N_IDX_HEADS):
        c = qi[:, (hh // 2) * LANES:(hh // 2 + 1) * LANES]
        q_heads.append((jnp.where(low, c, 0.0) if hh % 2 == 0 else jnp.where(low, 0.0, c)).astype(BF16))
        w_heads.append(misc[:, hh:hh + 1])

    def score_body(c, carry):
        k0 = pl.multiple_of(c * kc, kc)
        kk = ki_ref[0, pl.ds(k0, kc), :]
        acc = None
        for hh in range(N_IDX_HEADS):
            t = w_heads[hh] * jnp.maximum(_dot_nt(q_heads[hh], kk), 0.0)
            acc = t if acc is None else acc + t
        acc = jnp.where(acc == 0.0, 0.0, acc)
        sc = jnp.where(k0 + col_idx <= qpos, acc, NEG)
        bits = lax.bitcast_convert_type(sc, I32)
        key_scr[c] = jnp.where(bits < 0, bits ^ 0x7FFFFFFF, bits)
        return carry

    lax.fori_loop(0, nkc, score_body, 0)

    def count_ge(cand):
        def body(c, part):
            hit = jnp.where(key_scr[c] >= cand, 1.0, 0.0)
            for j in range(kc // LANES):
                part = part + hit[:, j * LANES:(j + 1) * LANES]
            return part
        part = lax.fori_loop(0, nkc, body, jnp.zeros((tq, LANES), F32))
        return jnp.sum(part, axis=1, keepdims=True)

    kf = float(topk)

    def bisect_body(i, lo):
        cand = lo + lax.shift_left(jnp.int32(1), 31 - i)
        return jnp.where(count_ge(cand) >= kf, cand, lo)

    lo = lax.fori_loop(0, 32, bisect_body, jnp.full((tq, 1), INT_MIN, I32))
    lo1 = lo + 1
    need = kf - count_ge(lo1)
    excess = count_ge(lo) - kf

    jsel_scr[...] = jnp.full(jsel_scr.shape, n_chunks * kc, I32)
    n_bits = max(1, int(math.ceil(math.log2(n_chunks * kc))))

    @pl.when(jnp.max(excess) > 0.5)
    def _():
        def count_eq_upto(t):
            def body(c, part):
                hit = jnp.where(key_scr[c] == lo, jnp.where(c * kc + col_idx <= t, 1.0, 0.0), 0.0)
                for j in range(kc // LANES):
                    part = part + hit[:, j * LANES:(j + 1) * LANES]
                return part
            part = lax.fori_loop(0, nkc, body, jnp.zeros((tq, LANES), F32))
            return jnp.sum(part, axis=1, keepdims=True)

        def idx_body(i, jcur):
            step = lax.shift_left(jnp.int32(1), n_bits - 1 - i)
            return jnp.where(count_eq_upto(jcur + step - 1) < need, jcur + step, jcur)

        jfin = lax.fori_loop(0, n_bits, idx_body, jnp.zeros((tq, 1), I32))
        jsel_scr[...] = jnp.broadcast_to(jfin, jsel_scr.shape)

    jsel = jsel_scr[:, :1]

    (q0s, q1s), low_q = _stack_group_queries(qa_ref, tq)
    _flash_init(m_scr, l_scr, acc_scr)

    def att_body(c, carry):
        k0 = pl.multiple_of(c * kc, kc)
        kidx = k0 + col_idx
        thr = jnp.where(kidx <= jsel, lo, lo1)
        bias = jnp.where(key_scr[c] >= thr, jnp.where(kidx <= qpos, 0.0, MASK_VALUE), MASK_VALUE)
        kvc = kv_ref[0, pl.ds(k0, kc), :]
        k128 = kvc[:, :LANES]
        v128 = kvc[:, LANES:]
        _flash_step(q0s, k128, v128, bias, m_scr, l_scr, acc_scr, 0, tq)
        _flash_step(q1s, k128, v128, bias, m_scr, l_scr, acc_scr, 1, tq)
        return carry

    lax.fori_loop(0, nkc, att_body, 0)

    for g in range(4):
        o = jnp.where(low_q, _flash_result(l_scr, acc_scr, g), _flash_result(l_scr, acc_scr, 4 + g))
        o_ref[0, :, g * LANES:(g + 1) * LANES] = o.astype(o_ref.dtype)


def _dsa(qi, qa, misc, ki_all, kv_all, *, tq, q_pos0, topk):
    b, tq_all, _ = qa.shape
    lp = ki_all.shape[1]
    kc = KEY_CHUNK
    assert lp % kc == 0 and kc >= topk and tq_all % tq == 0
    n_chunks = lp // kc
    kern = functools.partial(_dsa_kernel, tq=tq, kc=kc, q_pos0=q_pos0, topk=topk, n_chunks=n_chunks)
    qspec = lambda w: pl.BlockSpec((1, tq, w), lambda bi, qi_: (bi, qi_, 0))
    kspec = lambda w: pl.BlockSpec((1, lp, w), lambda bi, qi_: (bi, 0, 0))
    return pl.pallas_call(
        kern,
        grid=(b, tq_all // tq),
        in_specs=[qspec(512), qspec(512), qspec(LANES), kspec(LANES), kspec(256)],
        out_specs=qspec(512),
        out_shape=jax.ShapeDtypeStruct((b, tq_all, 512), BF16),
        scratch_shapes=[pltpu.VMEM((n_chunks, tq, kc), I32),
                        pltpu.VMEM((tq, LANES), I32),
                        pltpu.VMEM((8, tq, LANES), F32),
                        pltpu.VMEM((8, tq, LANES), F32),
                        pltpu.VMEM((8, tq, LANES), F32)],
        compiler_params=_cparams(2),
        name="dsa",
    )(qi, qa, misc, ki_all, kv_all)


def _nsa_kernel(qb_ref, misc_ref, cmp_ref, ks_ref, kw_ref, mcs_ref, o_ref,
                m_scr, l_scr, acc_scr, *, tq, kc, q_pos0, n_cmp, n_chunks, topn,
                win_pos0, win_len, win_rows):
    q0 = q_pos0 + pl.program_id(1) * tq
    row = lax.broadcasted_iota(I32, (tq, 1), 0)
    qpos = q0 + row
    ncp = cmp_ref.shape[1]
    nsp = mcs_ref.shape[1]
    (q0s, q1s), low = _stack_group_queries(qb_ref, tq)
    qs = (q0s, q1s)
    misc = misc_ref[0]

    cmp_all = cmp_ref[0]
    kc128 = cmp_all[:, :LANES]
    vc128 = cmp_all[:, LANES:]
    n_idx = lax.broadcasted_iota(I32, (tq, ncp), 1)
    bias_c = jnp.where(n_idx < n_cmp,
                       jnp.where(n_idx * CMP_STRIDE + (CMP_LEN - 1) <= qpos, 0.0, MASK_VALUE), MASK_VALUE)
    mcs = mcs_ref[...]
    o_cmp, imp = [], []
    for h in range(N_KV_B):
        s = _dot_nt(qs[h], kc128)
        pn = jnp.concatenate([_softmax_rows(s[g * tq:(g + 1) * tq] + bias_c) for g in range(4)],
                             axis=0).astype(BF16)
        o_cmp.append(_dot(pn, vc128))
        i4 = _dot(pn, mcs)
        imp.append(i4[0:tq] + i4[tq:2 * tq] + i4[2 * tq:3 * tq] + i4[3 * tq:4 * tq])

    blk = lax.broadcasted_iota(I32, (tq, nsp), 1)
    blk_f = blk.astype(F32)
    cur = lax.shift_right_logical(qpos, int(math.log2(SLC_BLOCK)))
    future = blk * SLC_BLOCK > qpos
    sel = []
    for h in range(N_KV_B):
        forced = jnp.where(blk == 0, BIG, jnp.where(blk == cur, BIG, jnp.where(blk == cur - 1, BIG, imp[h])))
        work0 = jnp.where(future, NEG, forced)

        def pick_body(i, carry):
            work, chosen = carry
            mx = jnp.max(work, axis=1, keepdims=True)
            first = jnp.min(jnp.where(work == mx, blk_f, 1e9), axis=1, keepdims=True)
            hit = blk_f == first
            return jnp.where(hit, -jnp.inf, work), jnp.where(hit, 1.0, chosen)

        _, chosen = lax.fori_loop(0, topn, pick_body, (work0, jnp.zeros((tq, nsp), F32)))
        sel.append(chosen)
    sel2 = jnp.concatenate(sel, axis=0).astype(BF16)

    nkc = jnp.minimum(lax.div(q0 + tq - 1, kc) + 1, n_chunks)
    col_idx = lax.broadcasted_iota(I32, (tq, kc), 1)
    e_row = lax.broadcasted_iota(I32, (nsp, kc), 0)
    e_col = lax.broadcasted_iota(I32, (nsp, kc), 1)
    _flash_init(m_scr, l_scr, acc_scr)

    def slc_body(c, carry):
        k0 = pl.multiple_of(c * kc, kc)
        expand = jnp.where(lax.shift_right_logical(k0 + e_col, int(math.log2(SLC_BLOCK))) == e_row,
                           1.0, 0.0).astype(BF16)
        kmask = _dot(sel2, expand)
        causal = k0 + col_idx <= qpos
        kvc = ks_ref[0, pl.ds(k0, kc), :]
        k128 = kvc[:, :LANES]
        v128 = kvc[:, LANES:]
        for h in range(N_KV_B):
            bias = jnp.where(kmask[h * tq:(h + 1) * tq] > 0.5, jnp.where(causal, 0.0, MASK_VALUE),
                             MASK_VALUE)
            _flash_step(qs[h], k128, v128, bias, m_scr, l_scr, acc_scr, h, tq)
        return carry

    lax.fori_loop(0, nkc, slc_body, 0)

    ws = jnp.clip(q0 - WINDOW - win_pos0, 0, win_rows - win_len)
    ws = pl.multiple_of(ws, 16)
    kwc = kw_ref[0, pl.ds(ws, win_len), :]
    kw128 = kwc[:, :LANES]
    vw128 = kwc[:, LANES:]
    kpos = win_pos0 + ws + lax.broadcasted_iota(I32, (tq, win_len), 1)
    dist = qpos - kpos
    bias_w = jnp.where(dist >= 0, jnp.where(dist <= WINDOW, 0.0, MASK_VALUE), MASK_VALUE)
    o_win = []
    for h in range(N_KV_B):
        s = _dot_nt(qs[h], kw128)
        pn = jnp.concatenate([_softmax_rows(s[g * tq:(g + 1) * tq] + bias_w) for g in range(4)],
                             axis=0).astype(BF16)
        o_win.append(_dot(pn, vw128))

    def gate(r, h, g):
        ln = _GATE_LANE0 + r * N_HEADS_B + h * G_B + g
        return misc[:, ln:ln + 1]

    for g in range(4):
        halves = []
        for h in range(N_KV_B):
            sl = slice(g * tq, (g + 1) * tq)
            halves.append(gate(0, h, g) * o_cmp[h][sl]
                          + gate(1, h, g) * _flash_result(l_scr, acc_scr, h * 4 + g)
                          + gate(2, h, g) * o_win[h][sl])
        o_ref[0, :, g * LANES:(g + 1) * LANES] = jnp.where(low, halves[0], halves[1]).astype(o_ref.dtype)


def _slc_map(n_cmp, n_slc, ncp, nsp):
    st = np.arange(ncp) * CMP_STRIDE
    bs = np.arange(nsp) * SLC_BLOCK
    m = (st[:, None] < bs[None, :] + SLC_BLOCK) & (st[:, None] + CMP_LEN > bs[None, :])
    m &= (np.arange(ncp)[:, None] < n_cmp) & (np.arange(nsp)[None, :] < n_slc)
    return jnp.asarray(m.astype(np.float32), BF16)


def _nsa(qb, misc, cmp_kv, ks_all, kw_all, *, tq, q_pos0, n_cmp, n_slc, win_pos0, win_len):
    b, tq_all, _ = qb.shape
    lp = ks_all.shape[1]
    kc = KEY_CHUNK
    ncp = cmp_kv.shape[1]
    nsp = -(-n_slc // LANES) * LANES
    assert lp % kc == 0 and ncp % LANES == 0 and ncp >= n_cmp and tq_all % tq == 0
    win_rows = kw_all.shape[1]
    mcs = _slc_map(n_cmp, n_slc, ncp, nsp)
    kern = functools.partial(_nsa_kernel, tq=tq, kc=kc, q_pos0=q_pos0, n_cmp=n_cmp,
                             n_chunks=lp // kc, topn=min(SLC_TOPN, n_slc),
                             win_pos0=win_pos0, win_len=win_len, win_rows=win_rows)
    qspec = lambda w: pl.BlockSpec((1, tq, w), lambda bi, qi_: (bi, qi_, 0))
    full = lambda n, w: pl.BlockSpec((1, n, w), lambda bi, qi_: (bi, 0, 0))
    return pl.pallas_call(
        kern,
        grid=(b, tq_all // tq),
        in_specs=[qspec(512), qspec(LANES), full(ncp, 256), full(lp, 256), full(win_rows, 256),
                  _const_spec(mcs.shape)],
        out_specs=qspec(512),
        out_shape=jax.ShapeDtypeStruct((b, tq_all, 512), BF16),
        scratch_shapes=[pltpu.VMEM((8, tq, LANES), F32),
                        pltpu.VMEM((8, tq, LANES), F32),
                        pltpu.VMEM((8, tq, LANES), F32)],
        compiler_params=_cparams(2),
        name="nsa",
    )(qb, misc, cmp_kv, ks_all, kw_all, mcs)


def _finish_kernel(*refs, tm, packed, t_new, tiles_per_batch):
    if packed:
        (x_ref, oa_ref, ob_ref, mg_ref, mod_ref, e1_ref, e2_ref, tpos_ref, wpa, wpb, wo, ln1g, ln1b,
         wg, wu, cw, cb, wd, ln2g, ln2b, y_o, tail_o, ext) = refs
    else:
        (x_ref, oa_ref, ob_ref, mg_ref, mod_ref, wpa, wpb, wo, ln1g, ln1b, wg, wu,
         cw, cb, wd, ln2g, ln2b, y_o, tail_o, ext) = refs
    d = D_MODEL
    mod = mod_ref[0]
    gate1, shift2, scale2, gate2 = (mod[:, i * d:(i + 1) * d] for i in range(4))
    x = x_ref[...]
    mg = mg_ref[...].astype(F32)
    m = mg[:, :d] * _dot(oa_ref[...], wpa[...]) + mg[:, d:] * _dot(ob_ref[...], wpb[...])
    x1 = _layernorm(DN_ALPHA * x + (1.0 + gate1) * _dot(m.astype(BF16), wo[...]), ln1g[...], ln1b[...])
    h2 = (x1 * (1.0 + scale2) + shift2).astype(BF16)
    ug = _dot(h2, wg[...])
    u = _dot(h2, wu[...])

    if packed:
        ext[0:8, :] = jnp.zeros((8, D_FF), F32)
    else:
        @pl.when(pl.program_id(0) % tiles_per_batch == 0)
        def _():
            ext[0:8, :] = jnp.zeros((8, D_FF), F32)
    ext[8:8 + tm, :] = ug
    prev1 = ext[7:7 + tm, :]
    prev2 = ext[6:6 + tm, :]
    if packed:
        tpos = tpos_ref[:, :1]
        prev1 = jnp.where(tpos >= 1, prev1, e1_ref[...])
        prev2 = jnp.where(tpos >= 2, prev2, e2_ref[...])
    w = cw[...]
    conv = cb[...] + w[2:3, :] * ug
    conv = conv + w[0:1, :] * prev2
    conv = conv + w[1:2, :] * prev1
    tail = ext[tm:tm + 8, :]
    ext[0:8, :] = tail
    if packed:
        tail_o[...] = ug
    else:
        tail_o[0] = tail
    act = (_gelu_tanh(conv) * u).astype(BF16)
    f = _dot(act, wd[...])
    y_o[...] = _layernorm(DN_ALPHA * x1 + (1.0 + gate2) * f, ln2g[...], ln2b[...])


def _finish(x_flat, oa, ob, mg, mod, hist, fw, *, tm, packed, t_new, tiles_per_batch):
    rows = x_flat.shape[0]
    n_tiles = rows // tm
    r_mod = mod.shape[1]
    row_spec = lambda w: pl.BlockSpec((tm, w), lambda i: (i, 0))
    in_specs = [row_spec(D_MODEL), row_spec(512), row_spec(512), row_spec(2 * D_MODEL),
                pl.BlockSpec((1, r_mod, 4 * D_MODEL), lambda i: (i // tiles_per_batch, 0, 0))]
    args = [x_flat, oa, ob, mg, mod]
    if packed:
        in_specs += [row_spec(D_FF), row_spec(D_FF), row_spec(LANES)]
        tpos = jnp.broadcast_to((jnp.arange(rows, dtype=I32) % t_new)[:, None], (rows, LANES))
        args += list(hist) + [tpos]
        tail_shape = jax.ShapeDtypeStruct((rows, D_FF), F32)
        tail_spec = row_spec(D_FF)
    else:
        nb = n_tiles // tiles_per_batch
        tail_shape = jax.ShapeDtypeStruct((nb, 8, D_FF), F32)
        tail_spec = pl.BlockSpec((1, 8, D_FF), lambda i: (i // tiles_per_batch, 0, 0))
    in_specs += [_const_spec(a.shape) for a in fw]
    args += list(fw)
    kern = functools.partial(_finish_kernel, tm=tm, packed=packed, t_new=t_new,
                             tiles_per_batch=tiles_per_batch)
    return pl.pallas_call(
        kern,
        grid=(n_tiles,),
        in_specs=in_specs,
        out_specs=[row_spec(D_MODEL), tail_spec],
        out_shape=[jax.ShapeDtypeStruct((rows, D_MODEL), F32), tail_shape],
        scratch_shapes=[pltpu.VMEM((tm + 8, D_FF), F32)],
        compiler_params=_cparams(1),
        name="finish",
    )(*args)


def kernel(x_prompt, x_sample, cache_a_kv, cache_a_idx, cache_b_cmp_kv, cache_b_slc_kv, state_b_win_kv, state_ffn_conv, page_table, c_prompt, c_sample, w_ada, b_ada, w_in, cmp_pe, cmp_w1, cmp_b1, cmp_w2, cmp_b2, w_pa, w_pb, w_o, ln1_g, ln1_b, w_ffn_gate, w_ffn_up, conv_w, conv_b, w_down, ln2_g, ln2_b):
    d = D_MODEL
    b, s, _ = x_prompt.shape
    bd, tn, _ = x_sample.shape
    n_pages = page_table.shape[1]
    past = n_pages * PAGE_SIZE
    kvw = 2 * N_KV_B * HEAD_DIM
    assert s % PROJ_ROWS == 0 and s % FINISH_ROWS == 0 and s % KEY_CHUNK == 0 and tn <= 8
    assert s >= WINDOW + Q_TILE and state_b_win_kv.shape[1] == WINDOW and tn >= CONV_W - 1

    perm = _proj_perm()
    w_perm = jnp.where(perm[None, :] >= 0, jnp.take(w_in, np.maximum(perm, 0), axis=1), 0.0).astype(BF16)
    head_rows = np.concatenate([np.r_[g * HEAD_DIM:(g + 1) * HEAD_DIM, (4 + g) * HEAD_DIM:(5 + g) * HEAD_DIM]
                                for g in range(4)])
    vec = lambda a: a.reshape(1, -1)
    fw = (w_pa[head_rows].astype(BF16), w_pb[head_rows].astype(BF16), w_o.astype(BF16),
          vec(ln1_g), vec(ln1_b), w_ffn_gate.astype(BF16), w_ffn_up.astype(BF16),
          jnp.concatenate([conv_w, jnp.zeros((8 - CONV_W, D_FF), F32)], axis=0), vec(conv_b),
          w_down.astype(BF16), vec(ln2_g), vec(ln2_b))
    cw = _compress_weights(cmp_pe, cmp_w1, cmp_b1, cmp_w2, cmp_b2)

    rows_c = -(-(b + bd) // 16) * 16
    c_all = jnp.concatenate([c_prompt, c_sample, jnp.zeros((rows_c - b - bd, d), F32)], axis=0)
    ada = _ada(c_all, w_ada.astype(BF16), b_ada)
    ada_p, ada_s = ada[:b], ada[b:b + bd]

    cos_p, sin_p = _rope_tables(jnp.arange(s))
    (qa, qi, qb, kva, kva_bf, ki, ki_bf, kv_cmp, kv_slc, kv_slc_bf, kv_win, kv_win_bf, misc, mg) = _project(
        x_prompt.reshape(b * s, d), ada_p[:, None, :2 * d], cos_p, sin_p, w_perm,
        PROJ_ROWS, s // PROJ_ROWS, s // PROJ_ROWS)
    r3 = lambda a: a.reshape(b, s, a.shape[-1])
    oa = _dsa(r3(qi), r3(qa), r3(misc), r3(ki_bf), r3(kva_bf),
              tq=Q_TILE, q_pos0=0, topk=min(DSA_TOPK, s // 4))
    n_ch = -(-s // CMP_STRIDE)
    assert s % CMP_STRIDE == 0 and n_ch % LANES == 0
    cmp_p = _compress(kv_cmp.reshape(b, n_ch, CMP_STRIDE * kvw), cw)
    ob = _nsa(r3(qb), r3(misc), cmp_p, r3(kv_slc_bf), r3(kv_win_bf),
              tq=Q_TILE, q_pos0=0, n_cmp=n_ch - 1, n_slc=-(-s // SLC_BLOCK),
              win_pos0=0, win_len=WINDOW + Q_TILE)
    y_p, tail_p = _finish(x_prompt.reshape(b * s, d), oa.reshape(b * s, 512), ob.reshape(b * s, 512), mg,
                          ada_p[:, None, 2 * d:], None, fw, tm=FINISH_ROWS, packed=False, t_new=0,
                          tiles_per_batch=s // FINISH_ROWS)

    rows_s = bd * tn
    pos_s = past + jnp.arange(tn)
    cos_s, sin_s = _rope_tables(jnp.tile(pos_s, bd))
    rep = lambda a: jnp.repeat(a, tn, axis=0)[None]
    (qa_s, qi_s, qb_s, kva_s, _, ki_s, _, kv_cmp_s, kv_slc_s, _, kv_win_s, kv_win_s_bf, misc_s, mg_s) = _project(
        x_sample.reshape(rows_s, d), rep(ada_s[:, :2 * d]), cos_s, sin_s, w_perm, rows_s, 1, 1)

    def new16(a):
        a = a.reshape(bd, tn, a.shape[-1])
        return jnp.concatenate([a, jnp.zeros((bd, 16 - tn, a.shape[-1]), F32)], axis=1)

    def q8(a):
        a = a.reshape(bd, tn, a.shape[-1])
        return jnp.concatenate([a, jnp.zeros((bd, 8 - tn, a.shape[-1]), a.dtype)], axis=1)

    ki_rows = _gather_pages(cache_a_idx, page_table, new16(ki_s))
    ki_all = jnp.concatenate([ki_rows, ki_rows], axis=-1)
    kva_all = _gather_pages(cache_a_kv.reshape(-1, PAGE_SIZE, kvw), page_table, new16(kva_s))
    cmp_all = _gather_pages(cache_b_cmp_kv.reshape(-1, PAGE_SIZE, kvw), page_table, new16(kv_cmp_s))
    slc_all = _gather_pages(cache_b_slc_kv.reshape(-1, PAGE_SIZE, kvw), page_table, new16(kv_slc_s))
    l_tot = past + tn
    lp = kva_all.shape[1]
    oa_s = _dsa(q8(qi_s), q8(qa_s), q8(misc_s), ki_all, kva_all,
                tq=8, q_pos0=past, topk=min(DSA_TOPK, l_tot // 4))
    assert (lp // CMP_STRIDE) % LANES == 0
    cmp_s = _compress(cmp_all.reshape(bd, lp // CMP_STRIDE, CMP_STRIDE * kvw), cw)
    win_len = WINDOW + Q_TILE
    win_all = jnp.concatenate(
        [state_b_win_kv.reshape(bd, WINDOW, kvw).astype(BF16), kv_win_s_bf.reshape(bd, tn, kvw),
         jnp.zeros((bd, win_len - WINDOW - tn, kvw), BF16)], axis=1)
    ob_s = _nsa(q8(qb_s), q8(misc_s), cmp_s, slc_all, win_all,
                tq=8, q_pos0=past, n_cmp=-(-l_tot // CMP_STRIDE) - 1, n_slc=-(-l_tot // SLC_BLOCK),
                win_pos0=past - WINDOW, win_len=win_len)
    hist = state_ffn_conv
    zrow = jnp.zeros((bd, 1, D_FF), F32)
    e1 = jnp.concatenate([hist[:, 1:2]] + [zrow] * (tn - 1), axis=1).reshape(rows_s, D_FF)
    e2 = jnp.concatenate([hist[:, 0:1], hist[:, 1:2]] + [zrow] * (tn - 2), axis=1).reshape(rows_s, D_FF)
    y_s, ug_s = _finish(x_sample.reshape(rows_s, d), oa_s[:, :tn].reshape(rows_s, 512),
                        ob_s[:, :tn].reshape(rows_s, 512), mg_s, rep(ada_s[:, 2 * d:]), (e1, e2), fw,
                        tm=rows_s, packed=True, t_new=tn, tiles_per_batch=1)

    kvshape = (2, N_KV_A, HEAD_DIM)
    win_s = jnp.concatenate([state_b_win_kv[:, tn:], kv_win_s.reshape((bd, tn) + kvshape)], axis=1)
    return (y_p.reshape(b, s, d), y_s.reshape(bd, tn, d),
            kva.reshape((b, s) + kvshape), kva_s.reshape((bd, tn) + kvshape),
            ki.reshape(b, s, IDX_DIM), ki_s.reshape(bd, tn, IDX_DIM),
            kv_cmp.reshape((b, s) + kvshape), kv_cmp_s.reshape((bd, tn) + kvshape),
            kv_slc.reshape((b, s) + kvshape), kv_slc_s.reshape((bd, tn) + kvshape),
            kv_win.reshape((b, s) + kvshape)[:, s - min(WINDOW, s):], win_s,
            tail_p[:, 8 - (CONV_W - 1):], ug_s.reshape(bd, tn, D_FF)[:, tn - (CONV_W - 1):])
```

```python
import functools
import math

import numpy as np
import jax
import jax.numpy as jnp
from jax import lax
from jax.experimental import pallas as pl
from jax.experimental.pallas import tpu as pltpu

D_MODEL = 1024
PAGE_SIZE = 128
HEAD_DIM = 64
N_HEADS_A = 8
N_KV_A = 2
N_IDX_HEADS = 8
IDX_DIM = 64
DSA_TOPK = 256
N_HEADS_B = 8
N_KV_B = 2
G_B = N_HEADS_B // N_KV_B
CMP_LEN = 32
CMP_STRIDE = 16
CMP_HID = 2 * HEAD_DIM
SLC_BLOCK = 64
SLC_TOPN = 16
WINDOW = 512
D_FF = 2816
CONV_W = 3
ROPE_THETA = 10000.0
LN_EPS = 1e-5
DEPTH = 1
DN_ALPHA = (2 * DEPTH) ** 0.25
NEG = -1e30
BIG = 1e30
MASK_VALUE = -2e30

LANES = 128
VMEM_LIMIT_BYTES = 56 * 1024 * 1024
Q_TILE = 128
KEY_CHUNK = 512
PAGES_PER_STEP = 16
PROJ_ROWS = 512
FINISH_ROWS = 256

F32 = jnp.float32
BF16 = jnp.bfloat16
I32 = jnp.int32
INT_MIN = -2 ** 31

_O_QA = 0
_O_KVA = _O_QA + N_HEADS_A * HEAD_DIM
_O_QI = _O_KVA + 2 * N_KV_A * HEAD_DIM
_O_KI = _O_QI + N_IDX_HEADS * IDX_DIM
_O_WI = _O_KI + IDX_DIM
_O_QB = _O_WI + N_IDX_HEADS
_O_KVB = _O_QB + N_HEADS_B * HEAD_DIM
_O_GB = _O_KVB + 6 * N_KV_B * HEAD_DIM
_O_MGA = _O_GB + 3 * N_HEADS_B
_O_MGB = _O_MGA + D_MODEL
_N_IN = _O_MGB + D_MODEL
_GATE_LANE0 = N_IDX_HEADS


def _cparams(n_axes):
    return pltpu.CompilerParams(dimension_semantics=("arbitrary",) * n_axes,
                                vmem_limit_bytes=VMEM_LIMIT_BYTES)


def _const_spec(shape):
    nd = len(shape)
    return pl.BlockSpec(shape, lambda *a: (0,) * nd, pipeline_mode=pl.Buffered(1))


def _dot(a, b):
    return jnp.dot(a, b, preferred_element_type=F32)


def _dot_nt(a, b):
    return lax.dot_general(a, b, (((1,), (1,)), ((), ())), preferred_element_type=F32)


def _sigmoid(x):
    return 1.0 / (1.0 + jnp.exp(-x))


def _gelu_tanh(x):
    return 0.5 * x * (1.0 + jnp.tanh(math.sqrt(2.0 / math.pi) * (x + 0.044715 * (x * x * x))))


def _layernorm(x, g, b):
    mu = jnp.mean(x, axis=-1, keepdims=True)
    xc = x - mu
    var = jnp.mean(xc * xc, axis=-1, keepdims=True)
    return xc * lax.rsqrt(var + LN_EPS) * g + b


def _ada_kernel(c_ref, w_ref, b_ref, o_ref):
    c = c_ref[...]
    s = (c * _sigmoid(c)).astype(BF16)
    o_ref[...] = _dot(s, w_ref[...]) + b_ref[...]


def _ada(c_all, w_ada_bf, b_ada):
    rows = c_all.shape[0]
    n_out = w_ada_bf.shape[1]
    tn = D_MODEL
    return pl.pallas_call(
        _ada_kernel,
        grid=(n_out // tn,),
        in_specs=[pl.BlockSpec((rows, D_MODEL), lambda j: (0, 0)),
                  pl.BlockSpec((D_MODEL, tn), lambda j: (0, j)),
                  pl.BlockSpec((1, tn), lambda j: (0, j))],
        out_specs=pl.BlockSpec((rows, tn), lambda j: (0, j)),
        out_shape=jax.ShapeDtypeStruct((rows, n_out), F32),
        compiler_params=_cparams(1),
        name="ada",
    )(c_all, w_ada_bf, b_ada.reshape(1, n_out))


_P_QA, _P_QI, _P_QB = 0, 512, 1024
_P_K = 1536
_P_V = 2176
_P_MISC = 2688
_P_MG = 2816
_P_END = _P_MG + 2 * D_MODEL


def _proj_perm():
    def paired_heads(base):
        cols = []
        for g in range(4):
            cols += list(range(base + g * HEAD_DIM, base + (g + 1) * HEAD_DIM))
            cols += list(range(base + (4 + g) * HEAD_DIM, base + (5 + g) * HEAD_DIM))
        return cols
    kv = 2 * N_KV_B * HEAD_DIM
    half = N_KV_B * HEAD_DIM
    cols = []
    cols += paired_heads(_O_QA)
    cols += list(range(_O_QI, _O_QI + N_IDX_HEADS * IDX_DIM))
    cols += paired_heads(_O_QB)
    cols += list(range(_O_KVA, _O_KVA + N_KV_A * HEAD_DIM))
    cols += list(range(_O_KI, _O_KI + IDX_DIM)) * 2
    for r in range(3):
        cols += list(range(_O_KVB + r * kv, _O_KVB + r * kv + half))
    cols += list(range(_O_KVA + N_KV_A * HEAD_DIM, _O_KVA + 2 * N_KV_A * HEAD_DIM))
    for r in range(3):
        cols += list(range(_O_KVB + r * kv + half, _O_KVB + (r + 1) * kv))
    misc = list(range(_O_WI, _O_WI + N_IDX_HEADS)) + list(range(_O_GB, _O_GB + 3 * N_HEADS_B))
    cols += misc + [-1] * (LANES - len(misc))
    cols += list(range(_O_MGA, _O_MGA + 2 * D_MODEL))
    assert len(cols) == _P_END
    return np.asarray(cols, np.int32)


def _proj_kernel(x_ref, mod_ref, cos_ref, sin_ref, w_ref,
                 qa_o, qi_o, qb_o, kva_o, kvab_o, ki_o, kib_o, cmp_o, slc_o, slcb_o,
                 win_o, winb_o, misc_o, mg_o):
    mod = mod_ref[0]
    shift1 = mod[:, :D_MODEL]
    scale1 = mod[:, D_MODEL:]
    h = (x_ref[...] * (1.0 + scale1) + shift1).astype(BF16)
    tm = h.shape[0]
    cos = cos_ref[...]
    sin = sin_ref[...]
    lane = lax.broadcasted_iota(I32, (tm, LANES), 1)
    first_half = (lane & (HEAD_DIM - 1)) < (HEAD_DIM // 2)

    def rope(y):
        partner = jnp.where(first_half, pltpu.roll(y, LANES - HEAD_DIM // 2, 1),
                            pltpu.roll(y, HEAD_DIM // 2, 1))
        return y * cos + partner * sin

    def mm(a, b):
        return _dot(h, w_ref[:, a:b])

    q_scale = HEAD_DIM ** -0.5
    for off, out, scale in ((_P_QA, qa_o, q_scale), (_P_QI, qi_o, 1.0), (_P_QB, qb_o, q_scale)):
        y = mm(off, off + 4 * LANES)
        for j in range(4):
            out[:, j * LANES:(j + 1) * LANES] = (rope(y[:, j * LANES:(j + 1) * LANES]) * scale).astype(BF16)

    yk = mm(_P_K, _P_K + 5 * LANES)
    ka, ki2, kc, ks, kw = [rope(yk[:, j * LANES:(j + 1) * LANES]) for j in range(5)]
    yv = mm(_P_V, _P_V + 4 * LANES)
    va, vc, vs, vw = [yv[:, j * LANES:(j + 1) * LANES] for j in range(4)]

    kva_o[:, :LANES] = ka
    kva_o[:, LANES:] = va
    kvab_o[:, :LANES] = ka.astype(BF16)
    kvab_o[:, LANES:] = va.astype(BF16)
    ki_o[...] = ki2[:, :IDX_DIM]
    kib_o[...] = ki2.astype(BF16)
    cmp_o[:, :LANES] = kc
    cmp_o[:, LANES:] = vc
    slc_o[:, :LANES] = ks
    slc_o[:, LANES:] = vs
    slcb_o[:, :LANES] = ks.astype(BF16)
    slcb_o[:, LANES:] = vs.astype(BF16)
    win_o[:, :LANES] = kw
    win_o[:, LANES:] = vw
    winb_o[:, :LANES] = kw.astype(BF16)
    winb_o[:, LANES:] = vw.astype(BF16)

    ym = mm(_P_MISC, _P_MISC + LANES)
    wi_scale = (N_IDX_HEADS * IDX_DIM) ** -0.5
    misc_o[...] = jnp.where(lane < N_IDX_HEADS, ym * wi_scale, _sigmoid(ym))
    for j in range(2):
        a = _P_MG + j * D_MODEL
        mg_o[:, j * D_MODEL:(j + 1) * D_MODEL] = _sigmoid(mm(a, a + D_MODEL)).astype(BF16)


def _project(x_flat, mod, cos_t, sin_t, w_perm, tm, tiles_per_group, tiles_per_table):
    rows = x_flat.shape[0]
    n_tiles = rows // tm
    r_mod = mod.shape[1]

    def row_spec(width):
        return pl.BlockSpec((tm, width), lambda i: (i, 0))

    out_defs = [(512, BF16), (512, BF16), (512, BF16),
                (256, F32), (256, BF16),
                (IDX_DIM, F32), (LANES, BF16),
                (256, F32),
                (256, F32), (256, BF16),
                (256, F32), (256, BF16),
                (LANES, F32),
                (2 * D_MODEL, BF16)]
    return pl.pallas_call(
        _proj_kernel,
        grid=(n_tiles,),
        in_specs=[row_spec(D_MODEL),
                  pl.BlockSpec((1, r_mod, 2 * D_MODEL), lambda i: (i // tiles_per_group, 0, 0)),
                  pl.BlockSpec((tm, LANES), lambda i: (i % tiles_per_table, 0)),
                  pl.BlockSpec((tm, LANES), lambda i: (i % tiles_per_table, 0)),
                  _const_spec((D_MODEL, _P_END))],
        out_specs=[row_spec(w) for w, _ in out_defs],
        out_shape=[jax.ShapeDtypeStruct((rows, w), dt) for w, dt in out_defs],
        compiler_params=_cparams(1),
        name="proj",
    )(x_flat, mod, cos_t, sin_t, w_perm)


def _rope_tables(pos):
    half = HEAD_DIM // 2
    inv = ROPE_THETA ** (-jnp.arange(half, dtype=F32) / half)
    ang = pos.astype(F32)[:, None] * inv[None, :]
    cos = jnp.tile(jnp.cos(ang), (1, LANES // half))
    sign = np.where((np.arange(LANES) % HEAD_DIM) < half, -1.0, 1.0).astype(np.float32)
    sin = jnp.tile(jnp.sin(ang), (1, LANES // half)) * sign[None, :]
    return cos, sin


def _gather_kernel(pt_ref, *refs, n_groups):
    pages = refs[:PAGES_PER_STEP]
    new_ref = refs[PAGES_PER_STEP]
    out_ref = refs[PAGES_PER_STEP + 1]
    g = pl.program_id(1)

    @pl.when(g < n_groups)
    def _():
        for i in range(PAGES_PER_STEP):
            out_ref[0, i * PAGE_SIZE:(i + 1) * PAGE_SIZE, :] = pages[i][0].astype(BF16)

    @pl.when(g == n_groups)
    def _():
        out_ref[0] = jnp.zeros(out_ref.shape[1:], BF16)
        out_ref[0, :new_ref.shape[1], :] = new_ref[0].astype(BF16)


def _gather_pages(cache, page_table, new_rows):
    n_phys, _, w = cache.shape
    b, n_pages = page_table.shape
    assert n_pages % PAGES_PER_STEP == 0
    n_groups = n_pages // PAGES_PER_STEP
    rows_step = PAGES_PER_STEP * PAGE_SIZE

    def page_spec(i):
        def imap(bi, g, pt):
            gg = jnp.minimum(g, n_groups - 1)
            return (pt[bi, gg * PAGES_PER_STEP + i], 0, 0)
        return pl.BlockSpec((1, PAGE_SIZE, w), imap)

    grid_spec = pltpu.PrefetchScalarGridSpec(
        num_scalar_prefetch=1,
        grid=(b, n_groups + 1),
        in_specs=[page_spec(i) for i in range(PAGES_PER_STEP)]
        + [pl.BlockSpec((1, new_rows.shape[1], w), lambda bi, g, pt: (bi, 0, 0))],
        out_specs=pl.BlockSpec((1, rows_step, w), lambda bi, g, pt: (bi, g, 0)),
    )
    return pl.pallas_call(
        functools.partial(_gather_kernel, n_groups=n_groups),
        grid_spec=grid_spec,
        out_shape=jax.ShapeDtypeStruct((b, (n_groups + 1) * rows_step, w), BF16),
        compiler_params=_cparams(2),
        name="gather_pages",
    )(page_table, *([cache] * PAGES_PER_STEP), new_rows)


def _cmp_kernel(x_ref, wa_ref, wb_ref, pea_ref, peb_ref, b1_ref, w2_ref, b2_ref, o_ref):
    x = x_ref[0].astype(BF16)
    r = x.shape[0]
    ha = _dot(x, wa_ref[...])
    hb = _dot(x, wb_ref[...])
    hpe = _dot(pea_ref[...], wa_ref[...]) + _dot(peb_ref[...], wb_ref[...])
    hpe = hpe[0:1, :] + b1_ref[...]
    hb_next = pltpu.roll(hb, r - 1, 0)
    hid = _gelu_tanh(ha + hb_next + hpe).astype(BF16)
    o_ref[0] = (_dot(hid, w2_ref[...]) + b2_ref[...]).astype(o_ref.dtype)


def _compress_weights(cmp_pe, cmp_w1, cmp_b1, cmp_w2, cmp_b2):
    eye = jnp.eye(2, dtype=F32)
    kvw = 2 * N_KV_B * HEAD_DIM

    def big(w1h):
        t = jnp.einsum('jcdf,jJ,hH->cjhdJHf', w1h, eye, eye)
        return t.reshape(CMP_STRIDE * kvw, 2 * N_KV_B * CMP_HID).astype(BF16)

    def pe_row(peh):
        t = jnp.broadcast_to(peh.transpose(1, 0, 2)[:, :, None, :], (CMP_STRIDE, 2, N_KV_B, HEAD_DIM))
        row = t.reshape(1, CMP_STRIDE * kvw)
        return jnp.concatenate([row, jnp.zeros((15, CMP_STRIDE * kvw), F32)], axis=0).astype(BF16)

    wa = big(cmp_w1[:, :CMP_STRIDE])
    wb = big(cmp_w1[:, CMP_STRIDE:])
    pea = pe_row(cmp_pe[:, :CMP_STRIDE])
    peb = pe_row(cmp_pe[:, CMP_STRIDE:])
    b1 = jnp.broadcast_to(cmp_b1[:, None, :], (2, N_KV_B, CMP_HID)).reshape(1, -1)
    w2 = jnp.einsum('jfd,jJ,hH->jhfJHd', cmp_w2, eye, eye).reshape(2 * N_KV_B * CMP_HID, kvw).astype(BF16)
    b2 = jnp.broadcast_to(cmp_b2[:, None, :], (2, N_KV_B, HEAD_DIM)).reshape(1, -1)
    return wa, wb, pea, peb, b1, w2, b2


def _compress(x_chunks, cw):
    b, r, cwid = x_chunks.shape
    wa, wb, pea, peb, b1, w2, b2 = cw
    return pl.pallas_call(
        _cmp_kernel,
        grid=(b,),
        in_specs=[pl.BlockSpec((1, r, cwid), lambda i: (i, 0, 0)),
                  _const_spec(wa.shape), _const_spec(wb.shape), _const_spec(pea.shape),
                  _const_spec(peb.shape), _const_spec(b1.shape), _const_spec(w2.shape),
                  _const_spec(b2.shape)],
        out_specs=pl.BlockSpec((1, r, w2.shape[1]), lambda i: (i, 0, 0)),
        out_shape=jax.ShapeDtypeStruct((b, r, w2.shape[1]), BF16),
        compiler_params=_cparams(1),
        name="compress",
    )(x_chunks, wa, wb, pea, peb, b1, w2, b2)


def _stack_group_queries(q_ref, tq):
    lane = lax.broadcasted_iota(I32, (tq, LANES), 1)
    low = lane < HEAD_DIM
    q = q_ref[0].astype(F32)
    cols = [q[:, g * LANES:(g + 1) * LANES] for g in range(4)]
    q0 = jnp.concatenate([jnp.where(low, c, 0.0) for c in cols], axis=0).astype(BF16)
    q1 = jnp.concatenate([jnp.where(low, 0.0, c) for c in cols], axis=0).astype(BF16)
    return (q0, q1), low


def _flash_init(m_scr, l_scr, acc_scr):
    m_scr[...] = jnp.full(m_scr.shape, NEG, F32)
    l_scr[...] = jnp.zeros(l_scr.shape, F32)
    acc_scr[...] = jnp.zeros(acc_scr.shape, F32)


def _flash_step(qs, k128, v128, biases, m_scr, l_scr, acc_scr, tq):
    half = 4 * tq
    kc = k128.shape[0]
    s = jnp.concatenate([_dot_nt(qs[0], k128), _dot_nt(qs[1], k128)], axis=0)
    s = s + jnp.concatenate([biases[0]] * 4 + [biases[1]] * 4, axis=0)
    m_prev = m_scr[...]
    m_new = jnp.maximum(m_prev, jnp.max(s, axis=1, keepdims=True))
    alpha = jnp.exp(m_prev - m_new)
    p = jnp.exp(s - jnp.concatenate([m_new] * (kc // LANES), axis=1))
    l_scr[...] = alpha * l_scr[...] + jnp.sum(p, axis=1, keepdims=True)
    m_scr[...] = m_new
    pb = p.astype(BF16)
    pv = jnp.concatenate([_dot(pb[:half], v128), _dot(pb[half:], v128)], axis=0)
    acc_scr[...] = alpha * acc_scr[...] + pv


def _flash_result(l_scr, acc_scr):
    l = l_scr[...]
    return jnp.where(l > 0.0, acc_scr[...] / l, 0.0)


def _softmax_rows(s):
    m = jnp.maximum(jnp.max(s, axis=1, keepdims=True), NEG)
    p = jnp.exp(s - m)
    l = jnp.sum(p, axis=1, keepdims=True)
    return jnp.where(l > 0.0, p / l, 0.0)


def _dsa_kernel(qi_ref, qa_ref, misc_ref, ki_ref, kv_ref, o_ref,
                key_scr, jsel_scr, m_scr, l_scr, acc_scr, *, tq, kc, q_pos0, topk, n_chunks):
    q0 = q_pos0 + pl.program_id(1) * tq
    row = lax.broadcasted_iota(I32, (tq, 1), 0)
    qpos = q0 + row
    nkc = jnp.minimum(lax.div(q0 + tq - 1, kc) + 1, n_chunks)
    lane = lax.broadcasted_iota(I32, (tq, LANES), 1)
    low = lane < IDX_DIM
    col_idx = lax.broadcasted_iota(I32, (tq, kc), 1)

    qi = qi_ref[0].astype(F32)
    misc = misc_ref[0]
    q_heads, w_heads = [], []
    for hh in range(N_IDX_HEADS):
        c = qi[:, (hh // 2) * LANES:(hh // 2 + 1) * LANES]
        q_heads.append((jnp.where(low, c, 0.0) if hh % 2 == 0 else jnp.where(low, 0.0, c)).astype(BF16))
        w_heads.append(misc[:, hh:hh + 1])

    def score_body(c, carry):
        k0 = pl.multiple_of(c * kc, kc)
        kk = ki_ref[0, pl.ds(k0, kc), :]
        acc = None
        for hh in range(N_IDX_HEADS):
            t = w_heads[hh] * jnp.maximum(_dot_nt(q_heads[hh], kk), 0.0)
            acc = t if acc is None else acc + t
        acc = jnp.where(acc == 0.0, 0.0, acc)
        sc = jnp.where(k0 + col_idx <= qpos, acc, NEG)
        bits = lax.bitcast_convert_type(sc, I32)
        key_scr[c] = jnp.where(bits < 0, bits ^ 0x7FFFFFFF, bits)
        return carry

    lax.fori_loop(0, nkc, score_body, 0)

    def count_ge(cand):
        def body(c, part):
            hit = jnp.where(key_scr[c] >= cand, 1.0, 0.0)
            for j in range(kc // LANES):
                part = part + hit[:, j * LANES:(j + 1) * LANES]
            return part
        part = lax.fori_loop(0, nkc, body, jnp.zeros((tq, LANES), F32))
        return jnp.sum(part, axis=1, keepdims=True)

    kf = float(topk)

    def bisect_cond(carry):
        i, _, todo = carry
        return jnp.logical_and(i < 32, jnp.max(todo) > 0.5)

    def bisect_body(carry):
        i, lo, todo = carry
        cand = lo + lax.shift_left(jnp.int32(1), 31 - i)
        cnt = count_ge(cand)
        lo = jnp.where(todo > 0.5, jnp.where(cnt >= kf, cand, lo), lo)
        todo = jnp.where(cnt == kf, 0.0, todo)
        return i + 1, lo, todo

    _, lo, _ = lax.while_loop(bisect_cond, bisect_body,
                              (jnp.int32(0), jnp.full((tq, 1), INT_MIN, I32), jnp.ones((tq, 1), F32)))
    lo1 = lo + 1
    need = kf - count_ge(lo1)
    excess = count_ge(lo) - kf

    jsel_scr[...] = jnp.full(jsel_scr.shape, n_chunks * kc, I32)
    n_bits = max(1, int(math.ceil(math.log2(n_chunks * kc))))

    @pl.when(jnp.max(excess) > 0.5)
    def _():
        def count_eq_upto(t):
            def body(c, part):
                hit = jnp.where(key_scr[c] == lo, jnp.where(c * kc + col_idx <= t, 1.0, 0.0), 0.0)
                for j in range(kc // LANES):
                    part = part + hit[:, j * LANES:(j + 1) * LANES]
                return part
            part = lax.fori_loop(0, nkc, body, jnp.zeros((tq, LANES), F32))
            return jnp.sum(part, axis=1, keepdims=True)

        def idx_body(i, jcur):
            step = lax.shift_left(jnp.int32(1), n_bits - 1 - i)
            return jnp.where(count_eq_upto(jcur + step - 1) < need, jcur + step, jcur)

        jfin = lax.fori_loop(0, n_bits, idx_body, jnp.zeros((tq, 1), I32))
        jsel_scr[...] = jnp.broadcast_to(jfin, jsel_scr.shape)

    jsel = jsel_scr[:, :1]

    (q0s, q1s), low_q = _stack_group_queries(qa_ref, tq)
    _flash_init(m_scr, l_scr, acc_scr)

    def att_body(c, carry):
        k0 = pl.multiple_of(c * kc, kc)
        kidx = k0 + col_idx
        thr = jnp.where(kidx <= jsel, lo, lo1)
        bias = jnp.where(key_scr[c] >= thr, jnp.where(kidx <= qpos, 0.0, MASK_VALUE), MASK_VALUE)
        kvc = kv_ref[0, pl.ds(k0, kc), :]
        k128 = kvc[:, :LANES]
        v128 = kvc[:, LANES:]
        _flash_step((q0s, q1s), k128, v128, (bias, bias), m_scr, l_scr, acc_scr, tq)
        return carry

    lax.fori_loop(0, nkc, att_body, 0)

    res = _flash_result(l_scr, acc_scr)
    for g in range(4):
        o = jnp.where(low_q, res[g * tq:(g + 1) * tq], res[(4 + g) * tq:(5 + g) * tq])
        o_ref[0, :, g * LANES:(g + 1) * LANES] = o.astype(o_ref.dtype)


def _dsa(qi, qa, misc, ki_all, kv_all, *, tq, q_pos0, topk):
    b, tq_all, _ = qa.shape
    lp = ki_all.shape[1]
    kc = KEY_CHUNK
    assert lp % kc == 0 and kc >= topk and tq_all % tq == 0
    n_chunks = lp // kc
    kern = functools.partial(_dsa_kernel, tq=tq, kc=kc, q_pos0=q_pos0, topk=topk, n_chunks=n_chunks)
    qspec = lambda w: pl.BlockSpec((1, tq, w), lambda bi, qi_: (bi, qi_, 0))
    kspec = lambda w: pl.BlockSpec((1, lp, w), lambda bi, qi_: (bi, 0, 0))
    return pl.pallas_call(
        kern,
        grid=(b, tq_all // tq),
        in_specs=[qspec(512), qspec(512), qspec(LANES), kspec(LANES), kspec(256)],
        out_specs=qspec(512),
        out_shape=jax.ShapeDtypeStruct((b, tq_all, 512), BF16),
        scratch_shapes=[pltpu.VMEM((n_chunks, tq, kc), I32),
                        pltpu.VMEM((tq, LANES), I32),
                        pltpu.VMEM((8 * tq, LANES), F32),
                        pltpu.VMEM((8 * tq, LANES), F32),
                        pltpu.VMEM((8 * tq, LANES), F32)],
        compiler_params=_cparams(2),
        name="dsa",
    )(qi, qa, misc, ki_all, kv_all)


def _nsa_kernel(qb_ref, misc_ref, cmp_ref, ks_ref, kw_ref, mcs_ref, o_ref,
                m_scr, l_scr, acc_scr, *, tq, kc, q_pos0, n_cmp, n_chunks, topn,
                win_pos0, win_len, win_rows):
    q0 = q_pos0 + pl.program_id(1) * tq
    row = lax.broadcasted_iota(I32, (tq, 1), 0)
    qpos = q0 + row
    ncp = cmp_ref.shape[1]
    nsp = mcs_ref.shape[1]
    (q0s, q1s), low = _stack_group_queries(qb_ref, tq)
    qs = (q0s, q1s)
    misc = misc_ref[0]

    cmp_all = cmp_ref[0]
    kc128 = cmp_all[:, :LANES]
    vc128 = cmp_all[:, LANES:]
    n_idx = lax.broadcasted_iota(I32, (tq, ncp), 1)
    bias_c = jnp.where(n_idx < n_cmp,
                       jnp.where(n_idx * CMP_STRIDE + (CMP_LEN - 1) <= qpos, 0.0, MASK_VALUE), MASK_VALUE)
    half = 4 * tq
    s = jnp.concatenate([_dot_nt(q0s, kc128), _dot_nt(q1s, kc128)], axis=0)
    pn = _softmax_rows(s + jnp.concatenate([bias_c] * 8, axis=0)).astype(BF16)
    o_cmp = jnp.concatenate([_dot(pn[:half], vc128), _dot(pn[half:], vc128)], axis=0)
    i8 = _dot(pn, mcs_ref[...])
    imp = [i8[(4 * h) * tq:(4 * h + 1) * tq] + i8[(4 * h + 1) * tq:(4 * h + 2) * tq]
           + i8[(4 * h + 2) * tq:(4 * h + 3) * tq] + i8[(4 * h + 3) * tq:(4 * h + 4) * tq]
           for h in range(N_KV_B)]

    blk = lax.broadcasted_iota(I32, (tq, nsp), 1)
    cur = lax.shift_right_logical(qpos, int(math.log2(SLC_BLOCK)))
    future = blk * SLC_BLOCK > qpos
    work0 = jnp.concatenate(
        [jnp.where(future, NEG, jnp.where(blk == 0, BIG, jnp.where(blk == cur, BIG,
                                                                   jnp.where(blk == cur - 1, BIG, imp[h]))))
         for h in range(N_KV_B)], axis=0)
    blk_f = lax.broadcasted_iota(I32, (2 * tq, nsp), 1).astype(F32)

    def pick_body(i, carry):
        work, chosen = carry
        mx = jnp.max(work, axis=1, keepdims=True)
        first = jnp.min(jnp.where(work == mx, blk_f, 1e9), axis=1, keepdims=True)
        hit = blk_f == first
        return jnp.where(hit, -jnp.inf, work), jnp.where(hit, 1.0, chosen)

    _, chosen = lax.fori_loop(0, topn, pick_body, (work0, jnp.zeros((2 * tq, nsp), F32)))
    sel2 = chosen.astype(BF16)

    nkc = jnp.minimum(lax.div(q0 + tq - 1, kc) + 1, n_chunks)
    col_idx = lax.broadcasted_iota(I32, (tq, kc), 1)
    e_row = lax.broadcasted_iota(I32, (nsp, kc), 0)
    e_col = lax.broadcasted_iota(I32, (nsp, kc), 1)
    _flash_init(m_scr, l_scr, acc_scr)

    def slc_body(c, carry):
        k0 = pl.multiple_of(c * kc, kc)
        expand = jnp.where(lax.shift_right_logical(k0 + e_col, int(math.log2(SLC_BLOCK))) == e_row,
                           1.0, 0.0).astype(BF16)
        kmask = _dot(sel2, expand)
        causal = k0 + col_idx <= qpos
        kvc = ks_ref[0, pl.ds(k0, kc), :]
        k128 = kvc[:, :LANES]
        v128 = kvc[:, LANES:]
        biases = [jnp.where(kmask[h * tq:(h + 1) * tq] > 0.5, jnp.where(causal, 0.0, MASK_VALUE), MASK_VALUE)
                  for h in range(N_KV_B)]
        _flash_step(qs, k128, v128, biases, m_scr, l_scr, acc_scr, tq)
        return carry

    lax.fori_loop(0, nkc, slc_body, 0)

    ws = jnp.clip(q0 - WINDOW - win_pos0, 0, win_rows - win_len)
    ws = pl.multiple_of(ws, 16)
    kwc = kw_ref[0, pl.ds(ws, win_len), :]
    kw128 = kwc[:, :LANES]
    vw128 = kwc[:, LANES:]
    kpos = win_pos0 + ws + lax.broadcasted_iota(I32, (tq, win_len), 1)
    dist = qpos - kpos
    bias_w = jnp.where(dist >= 0, jnp.where(dist <= WINDOW, 0.0, MASK_VALUE), MASK_VALUE)
    s = jnp.concatenate([_dot_nt(q0s, kw128), _dot_nt(q1s, kw128)], axis=0)
    pn = _softmax_rows(s + jnp.concatenate([bias_w] * 8, axis=0)).astype(BF16)
    o_win = jnp.concatenate([_dot(pn[:half], vw128), _dot(pn[half:], vw128)], axis=0)
    o_slc = _flash_result(l_scr, acc_scr)

    def gate(r, h, g):
        ln = _GATE_LANE0 + r * N_HEADS_B + h * G_B + g
        return misc[:, ln:ln + 1]

    for g in range(4):
        halves = []
        for h in range(N_KV_B):
            sl = slice((4 * h + g) * tq, (4 * h + g + 1) * tq)
            halves.append(gate(0, h, g) * o_cmp[sl] + gate(1, h, g) * o_slc[sl] + gate(2, h, g) * o_win[sl])
        o_ref[0, :, g * LANES:(g + 1) * LANES] = jnp.where(low, halves[0], halves[1]).astype(o_ref.dtype)


def _slc_map(n_cmp, n_slc, ncp, nsp):
    st = np.arange(ncp) * CMP_STRIDE
    bs = np.arange(nsp) * SLC_BLOCK
    m = (st[:, None] < bs[None, :] + SLC_BLOCK) & (st[:, None] + CMP_LEN > bs[None, :])
    m &= (np.arange(ncp)[:, None] < n_cmp) & (np.arange(nsp)[None, :] < n_slc)
    return jnp.asarray(m.astype(np.float32), BF16)


def _nsa(qb, misc, cmp_kv, ks_all, kw_all, *, tq, q_pos0, n_cmp, n_slc, win_pos0, win_len):
    b, tq_all, _ = qb.shape
    lp = ks_all.shape[1]
    kc = KEY_CHUNK
    ncp = cmp_kv.shape[1]
    nsp = -(-n_slc // LANES) * LANES
    assert lp % kc == 0 and ncp % LANES == 0 and ncp >= n_cmp and tq_all % tq == 0
    win_rows = kw_all.shape[1]
    mcs = _slc_map(n_cmp, n_slc, ncp, nsp)
    kern = functools.partial(_nsa_kernel, tq=tq, kc=kc, q_pos0=q_pos0, n_cmp=n_cmp,
                             n_chunks=lp // kc, topn=min(SLC_TOPN, n_slc),
                             win_pos0=win_pos0, win_len=win_len, win_rows=win_rows)
    qspec = lambda w: pl.BlockSpec((1, tq, w), lambda bi, qi_: (bi, qi_, 0))
    full = lambda n, w: pl.BlockSpec((1, n, w), lambda bi, qi_: (bi, 0, 0))
    return pl.pallas_call(
        kern,
        grid=(b, tq_all // tq),
        in_specs=[qspec(512), qspec(LANES), full(ncp, 256), full(lp, 256), full(win_rows, 256),
                  _const_spec(mcs.shape)],
        out_specs=qspec(512),
        out_shape=jax.ShapeDtypeStruct((b, tq_all, 512), BF16),
        scratch_shapes=[pltpu.VMEM((8 * tq, LANES), F32),
                        pltpu.VMEM((8 * tq, LANES), F32),
                        pltpu.VMEM((8 * tq, LANES), F32)],
        compiler_params=_cparams(2),
        name="nsa",
    )(qb, misc, cmp_kv, ks_all, kw_all, mcs)


def _finish_kernel(*refs, tm, packed, t_new, tiles_per_batch):
    if packed:
        (x_ref, oa_ref, ob_ref, mg_ref, mod_ref, e1_ref, e2_ref, tpos_ref, wpa, wpb, wo, ln1g, ln1b,
         wg, wu, cw, cb, wd, ln2g, ln2b, y_o, tail_o, ext) = refs
    else:
        (x_ref, oa_ref, ob_ref, mg_ref, mod_ref, wpa, wpb, wo, ln1g, ln1b, wg, wu,
         cw, cb, wd, ln2g, ln2b, y_o, tail_o, ext) = refs
    d = D_MODEL
    mod = mod_ref[0]
    gate1, shift2, scale2, gate2 = (mod[:, i * d:(i + 1) * d] for i in range(4))
    x = x_ref[...]
    mg = mg_ref[...].astype(F32)
    m = mg[:, :d] * _dot(oa_ref[...], wpa[...]) + mg[:, d:] * _dot(ob_ref[...], wpb[...])
    x1 = _layernorm(DN_ALPHA * x + (1.0 + gate1) * _dot(m.astype(BF16), wo[...]), ln1g[...], ln1b[...])
    h2 = (x1 * (1.0 + scale2) + shift2).astype(BF16)
    ug = _dot(h2, wg[...])
    u = _dot(h2, wu[...])

    if packed:
        ext[0:8, :] = jnp.zeros((8, D_FF), F32)
    else:
        @pl.when(pl.program_id(0) % tiles_per_batch == 0)
        def _():
            ext[0:8, :] = jnp.zeros((8, D_FF), F32)
    ext[8:8 + tm, :] = ug
    prev1 = ext[7:7 + tm, :]
    prev2 = ext[6:6 + tm, :]
    if packed:
        tpos = tpos_ref[:, :1]
        prev1 = jnp.where(tpos >= 1, prev1, e1_ref[...])
        prev2 = jnp.where(tpos >= 2, prev2, e2_ref[...])
    w = cw[...]
    conv = cb[...] + w[2:3, :] * ug
    conv = conv + w[0:1, :] * prev2
    conv = conv + w[1:2, :] * prev1
    tail = ext[tm:tm + 8, :]
    ext[0:8, :] = tail
    if packed:
        tail_o[...] = ug
    else:
        tail_o[0] = tail
    act = (_gelu_tanh(conv) * u).astype(BF16)
    f = _dot(act, wd[...])
    y_o[...] = _layernorm(DN_ALPHA * x1 + (1.0 + gate2) * f, ln2g[...], ln2b[...])


def _finish(x_flat, oa, ob, mg, mod, hist, fw, *, tm, packed, t_new, tiles_per_batch):
    rows = x_flat.shape[0]
    n_tiles = rows // tm
    r_mod = mod.shape[1]
    row_spec = lambda w: pl.BlockSpec((tm, w), lambda i: (i, 0))
    in_specs = [row_spec(D_MODEL), row_spec(512), row_spec(512), row_spec(2 * D_MODEL),
                pl.BlockSpec((1, r_mod, 4 * D_MODEL), lambda i: (i // tiles_per_batch, 0, 0))]
    args = [x_flat, oa, ob, mg, mod]
    if packed:
        in_specs += [row_spec(D_FF), row_spec(D_FF), row_spec(LANES)]
        tpos = jnp.broadcast_to((jnp.arange(rows, dtype=I32) % t_new)[:, None], (rows, LANES))
        args += list(hist) + [tpos]
        tail_shape = jax.ShapeDtypeStruct((rows, D_FF), F32)
        tail_spec = row_spec(D_FF)
    else:
        nb = n_tiles // tiles_per_batch
        tail_shape = jax.ShapeDtypeStruct((nb, 8, D_FF), F32)
        tail_spec = pl.BlockSpec((1, 8, D_FF), lambda i: (i // tiles_per_batch, 0, 0))
    in_specs += [_const_spec(a.shape) for a in fw]
    args += list(fw)
    kern = functools.partial(_finish_kernel, tm=tm, packed=packed, t_new=t_new,
                             tiles_per_batch=tiles_per_batch)
    return pl.pallas_call(
        kern,
        grid=(n_tiles,),
        in_specs=in_specs,
        out_specs=[row_spec(D_MODEL), tail_spec],
        out_shape=[jax.ShapeDtypeStruct((rows, D_MODEL), F32), tail_shape],
        scratch_shapes=[pltpu.VMEM((tm + 8, D_FF), F32)],
        compiler_params=_cparams(1),
        name="finish",
    )(*args)


def kernel(x_prompt, x_sample, cache_a_kv, cache_a_idx, cache_b_cmp_kv, cache_b_slc_kv, state_b_win_kv, state_ffn_conv, page_table, c_prompt, c_sample, w_ada, b_ada, w_in, cmp_pe, cmp_w1, cmp_b1, cmp_w2, cmp_b2, w_pa, w_pb, w_o, ln1_g, ln1_b, w_ffn_gate, w_ffn_up, conv_w, conv_b, w_down, ln2_g, ln2_b):
    d = D_MODEL
    b, s, _ = x_prompt.shape
    bd, tn, _ = x_sample.shape
    n_pages = page_table.shape[1]
    past = n_pages * PAGE_SIZE
    kvw = 2 * N_KV_B * HEAD_DIM
    assert s % PROJ_ROWS == 0 and s % FINISH_ROWS == 0 and s % KEY_CHUNK == 0 and tn <= 8
    assert s >= WINDOW + Q_TILE and state_b_win_kv.shape[1] == WINDOW and tn >= CONV_W - 1

    perm = _proj_perm()
    w_perm = jnp.where(perm[None, :] >= 0, jnp.take(w_in, np.maximum(perm, 0), axis=1), 0.0).astype(BF16)
    head_rows = np.concatenate([np.r_[g * HEAD_DIM:(g + 1) * HEAD_DIM, (4 + g) * HEAD_DIM:(5 + g) * HEAD_DIM]
                                for g in range(4)])
    vec = lambda a: a.reshape(1, -1)
    fw = (w_pa[head_rows].astype(BF16), w_pb[head_rows].astype(BF16), w_o.astype(BF16),
          vec(ln1_g), vec(ln1_b), w_ffn_gate.astype(BF16), w_ffn_up.astype(BF16),
          jnp.concatenate([conv_w, jnp.zeros((8 - CONV_W, D_FF), F32)], axis=0), vec(conv_b),
          w_down.astype(BF16), vec(ln2_g), vec(ln2_b))
    cw = _compress_weights(cmp_pe, cmp_w1, cmp_b1, cmp_w2, cmp_b2)

    rows_c = -(-(b + bd) // 16) * 16
    c_all = jnp.concatenate([c_prompt, c_sample, jnp.zeros((rows_c - b - bd, d), F32)], axis=0)
    ada = _ada(c_all, w_ada.astype(BF16), b_ada)
    ada_p, ada_s = ada[:b], ada[b:b + bd]

    cos_p, sin_p = _rope_tables(jnp.arange(s))
    (qa, qi, qb, kva, kva_bf, ki, ki_bf, kv_cmp, kv_slc, kv_slc_bf, kv_win, kv_win_bf, misc, mg) = _project(
        x_prompt.reshape(b * s, d), ada_p[:, None, :2 * d], cos_p, sin_p, w_perm,
        PROJ_ROWS, s // PROJ_ROWS, s // PROJ_ROWS)
    r3 = lambda a: a.reshape(b, s, a.shape[-1])
    oa = _dsa(r3(qi), r3(qa), r3(misc), r3(ki_bf), r3(kva_bf),
              tq=Q_TILE, q_pos0=0, topk=min(DSA_TOPK, s // 4))
    n_ch = -(-s // CMP_STRIDE)
    assert s % CMP_STRIDE == 0 and n_ch % LANES == 0
    cmp_p = _compress(kv_cmp.reshape(b, n_ch, CMP_STRIDE * kvw), cw)
    ob = _nsa(r3(qb), r3(misc), cmp_p, r3(kv_slc_bf), r3(kv_win_bf),
              tq=Q_TILE, q_pos0=0, n_cmp=n_ch - 1, n_slc=-(-s // SLC_BLOCK),
              win_pos0=0, win_len=WINDOW + Q_TILE)
    y_p, tail_p = _finish(x_prompt.reshape(b * s, d), oa.reshape(b * s, 512), ob.reshape(b * s, 512), mg,
                          ada_p[:, None, 2 * d:], None, fw, tm=FINISH_ROWS, packed=False, t_new=0,
                          tiles_per_batch=s // FINISH_ROWS)

    rows_s = bd * tn
    pos_s = past + jnp.arange(tn)
    cos_s, sin_s = _rope_tables(jnp.tile(pos_s, bd))
    rep = lambda a: jnp.repeat(a, tn, axis=0)[None]
    (qa_s, qi_s, qb_s, kva_s, _, ki_s, _, kv_cmp_s, kv_slc_s, _, kv_win_s, kv_win_s_bf, misc_s, mg_s) = _project(
        x_sample.reshape(rows_s, d), rep(ada_s[:, :2 * d]), cos_s, sin_s, w_perm, rows_s, 1, 1)

    def new16(a):
        a = a.reshape(bd, tn, a.shape[-1])
        return jnp.concatenate([a, jnp.zeros((bd, 16 - tn, a.shape[-1]), F32)], axis=1)

    def q8(a):
        a = a.reshape(bd, tn, a.shape[-1])
        return jnp.concatenate([a, jnp.zeros((bd, 8 - tn, a.shape[-1]), a.dtype)], axis=1)

    ki_rows = _gather_pages(cache_a_idx, page_table, new16(ki_s))
    ki_all = jnp.concatenate([ki_rows, ki_rows], axis=-1)
    kva_all = _gather_pages(cache_a_kv.reshape(-1, PAGE_SIZE, kvw), page_table, new16(kva_s))
    cmp_all = _gather_pages(cache_b_cmp_kv.reshape(-1, PAGE_SIZE, kvw), page_table, new16(kv_cmp_s))
    slc_all = _gather_pages(cache_b_slc_kv.reshape(-1, PAGE_SIZE, kvw), page_table, new16(kv_slc_s))
    l_tot = past + tn
    lp = kva_all.shape[1]
    oa_s = _dsa(q8(qi_s), q8(qa_s), q8(misc_s), ki_all, kva_all,
                tq=8, q_pos0=past, topk=min(DSA_TOPK, l_tot // 4))
    assert (lp // CMP_STRIDE) % LANES == 0
    cmp_s = _compress(cmp_all.reshape(bd, lp // CMP_STRIDE, CMP_STRIDE * kvw), cw)
    win_len = WINDOW + Q_TILE
    win_all = jnp.concatenate(
        [state_b_win_kv.reshape(bd, WINDOW, kvw).astype(BF16), kv_win_s_bf.reshape(bd, tn, kvw),
         jnp.zeros((bd, win_len - WINDOW - tn, kvw), BF16)], axis=1)
    ob_s = _nsa(q8(qb_s), q8(misc_s), cmp_s, slc_all, win_all,
                tq=8, q_pos0=past, n_cmp=-(-l_tot // CMP_STRIDE) - 1, n_slc=-(-l_tot // SLC_BLOCK),
                win_pos0=past - WINDOW, win_len=win_len)
    hist = state_ffn_conv
    zrow = jnp.zeros((bd, 1, D_FF), F32)
    e1 = jnp.concatenate([hist[:, 1:2]] + [zrow] * (tn - 1), axis=1).reshape(rows_s, D_FF)
    e2 = jnp.concatenate([hist[:, 0:1], hist[:, 1:2]] + [zrow] * (tn - 2), axis=1).reshape(rows_s, D_FF)
    y_s, ug_s = _finish(x_sample.reshape(rows_s, d), oa_s[:, :tn].reshape(rows_s, 512),
                        ob_s[:, :tn].reshape(rows_s, 512), mg_s, rep(ada_s[:, 2 * d:]), (e1, e2), fw,
                        tm=rows_s, packed=True, t_new=tn, tiles_per_batch=1)

    kvshape = (2, N_KV_A, HEAD_DIM)
    win_s = jnp.concatenate([state_b_win_kv[:, tn:], kv_win_s.reshape((bd, tn) + kvshape)], axis=1)
    return (y_p.reshape(b, s, d), y_s.reshape(bd, tn, d),
            kva.reshape((b, s) + kvshape), kva_s.reshape((bd, tn) + kvshape),
            ki.reshape(b, s, IDX_DIM), ki_s.reshape(bd, tn, IDX_DIM),
            kv_cmp.reshape((b, s) + kvshape), kv_cmp_s.reshape((bd, tn) + kvshape),
            kv_slc.reshape((b, s) + kvshape), kv_slc_s.reshape((bd, tn) + kvshape),
            kv_win.reshape((b, s) + kvshape)[:, s - min(WINDOW, s):], win_s,
            tail_p[:, 8 - (CONV_W - 1):], ug_s.reshape(bd, tn, D_FF)[:, tn - (CONV_W - 1):])
```

```python
import functools
import math

import numpy as np
import jax
import jax.numpy as jnp
from jax import lax
from jax.experimental import pallas as pl
from jax.experimental.pallas import tpu as pltpu

D_MODEL = 1024
PAGE_SIZE = 128
HEAD_DIM = 64
N_HEADS_A = 8
N_KV_A = 2
N_IDX_HEADS = 8
IDX_DIM = 64
DSA_TOPK = 256
N_HEADS_B = 8
N_KV_B = 2
G_B = N_HEADS_B // N_KV_B
CMP_LEN = 32
CMP_STRIDE = 16
CMP_HID = 2 * HEAD_DIM
SLC_BLOCK = 64
SLC_TOPN = 16
WINDOW = 512
D_FF = 2816
CONV_W = 3
ROPE_THETA = 10000.0
LN_EPS = 1e-5
DEPTH = 1
DN_ALPHA = (2 * DEPTH) ** 0.25
NEG = -1e30
BIG = 1e30
MASK_VALUE = -2e30

LANES = 128
VMEM_LIMIT_BYTES = 56 * 1024 * 1024
Q_TILE = 256
KEY_CHUNK = 512
PAGES_PER_STEP = 16
PROJ_ROWS = 512
FINISH_ROWS = 256

F32 = jnp.float32
BF16 = jnp.bfloat16
I32 = jnp.int32
INT_MIN = -2 ** 31

_O_QA = 0
_O_KVA = _O_QA + N_HEADS_A * HEAD_DIM
_O_QI = _O_KVA + 2 * N_KV_A * HEAD_DIM
_O_KI = _O_QI + N_IDX_HEADS * IDX_DIM
_O_WI = _O_KI + IDX_DIM
_O_QB = _O_WI + N_IDX_HEADS
_O_KVB = _O_QB + N_HEADS_B * HEAD_DIM
_O_GB = _O_KVB + 6 * N_KV_B * HEAD_DIM
_O_MGA = _O_GB + 3 * N_HEADS_B
_O_MGB = _O_MGA + D_MODEL
_N_IN = _O_MGB + D_MODEL
_GATE_LANE0 = N_IDX_HEADS


def _cparams(n_axes):
    return pltpu.CompilerParams(dimension_semantics=("arbitrary",) * n_axes,
                                vmem_limit_bytes=VMEM_LIMIT_BYTES)


def _const_spec(shape):
    nd = len(shape)
    return pl.BlockSpec(shape, lambda *a: (0,) * nd, pipeline_mode=pl.Buffered(1))


def _dot(a, b):
    return jnp.dot(a, b, preferred_element_type=F32)


def _dot_nt(a, b):
    return lax.dot_general(a, b, (((1,), (1,)), ((), ())), preferred_element_type=F32)


def _sigmoid(x):
    return 1.0 / (1.0 + jnp.exp(-x))


def _gelu_tanh(x):
    return 0.5 * x * (1.0 + jnp.tanh(math.sqrt(2.0 / math.pi) * (x + 0.044715 * (x * x * x))))


def _layernorm(x, g, b):
    mu = jnp.mean(x, axis=-1, keepdims=True)
    xc = x - mu
    var = jnp.mean(xc * xc, axis=-1, keepdims=True)
    return xc * lax.rsqrt(var + LN_EPS) * g + b


def _ada_kernel(c_ref, w_ref, b_ref, o_ref):
    c = c_ref[...]
    s = (c * _sigmoid(c)).astype(BF16)
    o_ref[...] = _dot(s, w_ref[...]) + b_ref[...]


def _ada(c_all, w_ada_bf, b_ada):
    rows = c_all.shape[0]
    n_out = w_ada_bf.shape[1]
    tn = D_MODEL
    return pl.pallas_call(
        _ada_kernel,
        grid=(n_out // tn,),
        in_specs=[pl.BlockSpec((rows, D_MODEL), lambda j: (0, 0)),
                  pl.BlockSpec((D_MODEL, tn), lambda j: (0, j)),
                  pl.BlockSpec((1, tn), lambda j: (0, j))],
        out_specs=pl.BlockSpec((rows, tn), lambda j: (0, j)),
        out_shape=jax.ShapeDtypeStruct((rows, n_out), F32),
        compiler_params=_cparams(1),
        name="ada",
    )(c_all, w_ada_bf, b_ada.reshape(1, n_out))


_P_QA, _P_QI, _P_QB = 0, 512, 1024
_P_K = 1536
_P_V = 2176
_P_MISC = 2688
_P_MG = 2816
_P_END = _P_MG + 2 * D_MODEL


def _proj_perm():
    def paired_heads(base):
        cols = []
        for g in range(4):
            cols += list(range(base + g * HEAD_DIM, base + (g + 1) * HEAD_DIM))
            cols += list(range(base + (4 + g) * HEAD_DIM, base + (5 + g) * HEAD_DIM))
        return cols
    kv = 2 * N_KV_B * HEAD_DIM
    half = N_KV_B * HEAD_DIM
    cols = []
    cols += paired_heads(_O_QA)
    cols += list(range(_O_QI, _O_QI + N_IDX_HEADS * IDX_DIM))
    cols += paired_heads(_O_QB)
    cols += list(range(_O_KVA, _O_KVA + N_KV_A * HEAD_DIM))
    cols += list(range(_O_KI, _O_KI + IDX_DIM)) * 2
    for r in range(3):
        cols += list(range(_O_KVB + r * kv, _O_KVB + r * kv + half))
    cols += list(range(_O_KVA + N_KV_A * HEAD_DIM, _O_KVA + 2 * N_KV_A * HEAD_DIM))
    for r in range(3):
        cols += list(range(_O_KVB + r * kv + half, _O_KVB + (r + 1) * kv))
    misc = list(range(_O_WI, _O_WI + N_IDX_HEADS)) + list(range(_O_GB, _O_GB + 3 * N_HEADS_B))
    cols += misc + [-1] * (LANES - len(misc))
    cols += list(range(_O_MGA, _O_MGA + 2 * D_MODEL))
    assert len(cols) == _P_END
    return np.asarray(cols, np.int32)


def _with_ones(v, low):
    return jnp.where(low, v, 1.0), jnp.where(low, 1.0, v)


def _proj_kernel(x_ref, mod_ref, cos_ref, sin_ref, w_ref,
                 qa_o, qi_o, qb_o, kvab_o, kib_o, cmpb_o, slcb_o, winb_o, misc_o, mg_o,
                 kva_o, ki_o, cmp_o, slc_o, win_o, *, transposed):
    mod = mod_ref[0]
    shift1 = mod[:, :D_MODEL]
    scale1 = mod[:, D_MODEL:]
    h = (x_ref[...] * (1.0 + scale1) + shift1).astype(BF16)
    tm = h.shape[0]
    cos = cos_ref[...]
    sin = sin_ref[...]
    lane = lax.broadcasted_iota(I32, (tm, LANES), 1)
    first_half = (lane & (HEAD_DIM - 1)) < (HEAD_DIM // 2)
    low = lane < HEAD_DIM

    def rope(y):
        partner = jnp.where(first_half, pltpu.roll(y, LANES - HEAD_DIM // 2, 1),
                            pltpu.roll(y, HEAD_DIM // 2, 1))
        return y * cos + partner * sin

    def mm(a, b):
        return _dot(h, w_ref[:, a:b])

    q_scale = HEAD_DIM ** -0.5 * math.log2(math.e)
    for off, out, scale in ((_P_QA, qa_o, q_scale), (_P_QI, qi_o, 1.0), (_P_QB, qb_o, q_scale)):
        y = mm(off, off + 4 * LANES)
        for j in range(4):
            out[:, j * LANES:(j + 1) * LANES] = (rope(y[:, j * LANES:(j + 1) * LANES]) * scale).astype(BF16)

    yk = mm(_P_K, _P_K + 5 * LANES)
    ka, ki2, kc, ks, kw = [rope(yk[:, j * LANES:(j + 1) * LANES]) for j in range(5)]
    yv = mm(_P_V, _P_V + 4 * LANES)
    va, vc, vs, vw = [yv[:, j * LANES:(j + 1) * LANES] for j in range(4)]

    for out, k, v in ((kvab_o, ka, va), (slcb_o, ks, vs)):
        v0, v1 = _with_ones(v, low)
        out[:, :LANES] = k.astype(BF16)
        out[:, LANES:2 * LANES] = v0.astype(BF16)
        out[:, 2 * LANES:] = v1.astype(BF16)
    kib_o[...] = ki2.astype(BF16)
    for out, k, v in ((cmpb_o, kc, vc), (winb_o, kw, vw)):
        out[:, :LANES] = k.astype(BF16)
        out[:, LANES:] = v.astype(BF16)

    for out, k, v in ((kva_o, ka, va), (cmp_o, kc, vc), (slc_o, ks, vs), (win_o, kw, vw)):
        if transposed:
            out[0, :LANES, :] = k.T
            out[0, LANES:, :] = v.T
        else:
            out[:, :LANES] = k
            out[:, LANES:] = v
    if transposed:
        ki_o[0] = ki2.T[:IDX_DIM, :]
    else:
        ki_o[...] = ki2[:, :IDX_DIM]

    ym = mm(_P_MISC, _P_MISC + LANES)
    wi_scale = (N_IDX_HEADS * IDX_DIM) ** -0.5
    misc_o[...] = jnp.where(lane < N_IDX_HEADS, ym * wi_scale, _sigmoid(ym))
    for j in range(2):
        a = _P_MG + j * D_MODEL
        mg_o[:, j * D_MODEL:(j + 1) * D_MODEL] = _sigmoid(mm(a, a + D_MODEL)).astype(BF16)


def _project(x_flat, mod, cos_t, sin_t, w_perm, tm, tiles_per_group, tiles_per_table, n_batch=None):
    rows = x_flat.shape[0]
    n_tiles = rows // tm
    r_mod = mod.shape[1]
    transposed = n_batch is not None

    def row_spec(width):
        return pl.BlockSpec((tm, width), lambda i: (i, 0))

    bf_defs = [(512, BF16), (512, BF16), (512, BF16),
               (384, BF16), (LANES, BF16),
               (256, BF16), (384, BF16), (256, BF16),
               (LANES, F32),
               (2 * D_MODEL, BF16)]
    f32_widths = [256, IDX_DIM, 256, 256, 256]
    out_specs = [row_spec(w) for w, _ in bf_defs]
    out_shape = [jax.ShapeDtypeStruct((rows, w), dt) for w, dt in bf_defs]
    if transposed:
        s = rows // n_batch
        tpb = s // tm
        out_specs += [pl.BlockSpec((1, w, tm), lambda i: (i // tpb, 0, i % tpb)) for w in f32_widths]
        out_shape += [jax.ShapeDtypeStruct((n_batch, w, s), F32) for w in f32_widths]
    else:
        out_specs += [row_spec(w) for w in f32_widths]
        out_shape += [jax.ShapeDtypeStruct((rows, w), F32) for w in f32_widths]
    return pl.pallas_call(
        functools.partial(_proj_kernel, transposed=transposed),
        grid=(n_tiles,),
        in_specs=[row_spec(D_MODEL),
                  pl.BlockSpec((1, r_mod, 2 * D_MODEL), lambda i: (i // tiles_per_group, 0, 0)),
                  pl.BlockSpec((tm, LANES), lambda i: (i % tiles_per_table, 0)),
                  pl.BlockSpec((tm, LANES), lambda i: (i % tiles_per_table, 0)),
                  _const_spec((D_MODEL, _P_END))],
        out_specs=out_specs,
        out_shape=out_shape,
        compiler_params=_cparams(1),
        name="proj",
    )(x_flat, mod, cos_t, sin_t, w_perm)


def _rope_tables(pos):
    half = HEAD_DIM // 2
    inv = ROPE_THETA ** (-jnp.arange(half, dtype=F32) / half)
    ang = pos.astype(F32)[:, None] * inv[None, :]
    cos = jnp.tile(jnp.cos(ang), (1, LANES // half))
    sign = np.where((np.arange(LANES) % HEAD_DIM) < half, -1.0, 1.0).astype(np.float32)
    sin = jnp.tile(jnp.sin(ang), (1, LANES // half)) * sign[None, :]
    return cos, sin


def _gather_kernel(pt_ref, *refs, n_groups, mode):
    pages = refs[:PAGES_PER_STEP]
    new_ref = refs[PAGES_PER_STEP]
    out_ref = refs[PAGES_PER_STEP + 1]
    g = pl.program_id(1)

    def put(r0, rows):
        n = rows.shape[0]
        if mode == "ones":
            low = lax.broadcasted_iota(I32, (n, LANES), 1) < HEAD_DIM
            v0, v1 = _with_ones(rows[:, LANES:], low)
            out_ref[0, r0:r0 + n, :LANES] = rows[:, :LANES].astype(BF16)
            out_ref[0, r0:r0 + n, LANES:2 * LANES] = v0.astype(BF16)
            out_ref[0, r0:r0 + n, 2 * LANES:] = v1.astype(BF16)
        elif mode == "dup":
            out_ref[0, r0:r0 + n, :] = jnp.concatenate([rows, rows], axis=1).astype(BF16)
        else:
            out_ref[0, r0:r0 + n, :] = rows.astype(BF16)

    @pl.when(g < n_groups)
    def _():
        for i in range(PAGES_PER_STEP):
            put(i * PAGE_SIZE, pages[i][0])

    @pl.when(g == n_groups)
    def _():
        out_ref[0] = jnp.zeros(out_ref.shape[1:], BF16)
        put(0, new_ref[0])


def _gather_pages(cache, page_table, new_rows, mode="plain"):
    n_phys, _, w = cache.shape
    w_out = {"plain": w, "ones": w + LANES, "dup": 2 * w}[mode]
    b, n_pages = page_table.shape
    assert n_pages % PAGES_PER_STEP == 0
    n_groups = n_pages // PAGES_PER_STEP
    rows_step = PAGES_PER_STEP * PAGE_SIZE

    def page_spec(i):
        def imap(bi, g, pt):
            gg = jnp.minimum(g, n_groups - 1)
            return (pt[bi, gg * PAGES_PER_STEP + i], 0, 0)
        return pl.BlockSpec((1, PAGE_SIZE, w), imap)

    grid_spec = pltpu.PrefetchScalarGridSpec(
        num_scalar_prefetch=1,
        grid=(b, n_groups + 1),
        in_specs=[page_spec(i) for i in range(PAGES_PER_STEP)]
        + [pl.BlockSpec((1, new_rows.shape[1], w), lambda bi, g, pt: (bi, 0, 0))],
        out_specs=pl.BlockSpec((1, rows_step, w_out), lambda bi, g, pt: (bi, g, 0)),
    )
    return pl.pallas_call(
        functools.partial(_gather_kernel, n_groups=n_groups, mode=mode),
        grid_spec=grid_spec,
        out_shape=jax.ShapeDtypeStruct((b, (n_groups + 1) * rows_step, w_out), BF16),
        compiler_params=_cparams(2),
        name="gather_pages",
    )(page_table, *([cache] * PAGES_PER_STEP), new_rows)


def _cmp_kernel(x_ref, wa_ref, wb_ref, pea_ref, peb_ref, b1_ref, w2_ref, b2_ref, o_ref):
    x = x_ref[0].astype(BF16)
    r = x.shape[0]
    ha = _dot(x, wa_ref[...])
    hb = _dot(x, wb_ref[...])
    hpe = _dot(pea_ref[...], wa_ref[...]) + _dot(peb_ref[...], wb_ref[...])
    hpe = hpe[0:1, :] + b1_ref[...]
    hb_next = pltpu.roll(hb, r - 1, 0)
    hid = _gelu_tanh(ha + hb_next + hpe).astype(BF16)
    o_ref[0] = (_dot(hid, w2_ref[...]) + b2_ref[...]).astype(o_ref.dtype)


def _compress_weights(cmp_pe, cmp_w1, cmp_b1, cmp_w2, cmp_b2):
    eye = jnp.eye(2, dtype=F32)
    kvw = 2 * N_KV_B * HEAD_DIM

    def big(w1h):
        t = jnp.einsum('jcdf,jJ,hH->cjhdJHf', w1h, eye, eye)
        return t.reshape(CMP_STRIDE * kvw, 2 * N_KV_B * CMP_HID).astype(BF16)

    def pe_row(peh):
        t = jnp.broadcast_to(peh.transpose(1, 0, 2)[:, :, None, :], (CMP_STRIDE, 2, N_KV_B, HEAD_DIM))
        row = t.reshape(1, CMP_STRIDE * kvw)
        return jnp.concatenate([row, jnp.zeros((15, CMP_STRIDE * kvw), F32)], axis=0).astype(BF16)

    wa = big(cmp_w1[:, :CMP_STRIDE])
    wb = big(cmp_w1[:, CMP_STRIDE:])
    pea = pe_row(cmp_pe[:, :CMP_STRIDE])
    peb = pe_row(cmp_pe[:, CMP_STRIDE:])
    b1 = jnp.broadcast_to(cmp_b1[:, None, :], (2, N_KV_B, CMP_HID)).reshape(1, -1)
    w2 = jnp.einsum('jfd,jJ,hH->jhfJHd', cmp_w2, eye, eye).reshape(2 * N_KV_B * CMP_HID, kvw).astype(BF16)
    b2 = jnp.broadcast_to(cmp_b2[:, None, :], (2, N_KV_B, HEAD_DIM)).reshape(1, -1)
    return wa, wb, pea, peb, b1, w2, b2


def _compress(x_chunks, cw):
    b, r, cwid = x_chunks.shape
    wa, wb, pea, peb, b1, w2, b2 = cw
    return pl.pallas_call(
        _cmp_kernel,
        grid=(b,),
        in_specs=[pl.BlockSpec((1, r, cwid), lambda i: (i, 0, 0)),
                  _const_spec(wa.shape), _const_spec(wb.shape), _const_spec(pea.shape),
                  _const_spec(peb.shape), _const_spec(b1.shape), _const_spec(w2.shape),
                  _const_spec(b2.shape)],
        out_specs=pl.BlockSpec((1, r, w2.shape[1]), lambda i: (i, 0, 0)),
        out_shape=jax.ShapeDtypeStruct((b, r, w2.shape[1]), BF16),
        compiler_params=_cparams(1),
        name="compress",
    )(x_chunks, wa, wb, pea, peb, b1, w2, b2)


def _stack_group_queries(q_ref, tq):
    lane = lax.broadcasted_iota(I32, (tq, LANES), 1)
    low = lane < HEAD_DIM
    q = q_ref[0].astype(F32)
    cols = [q[:, g * LANES:(g + 1) * LANES] for g in range(4)]
    q0 = jnp.concatenate([jnp.where(low, c, 0.0) for c in cols], axis=0).astype(BF16)
    q1 = jnp.concatenate([jnp.where(low, 0.0, c) for c in cols], axis=0).astype(BF16)
    return (q0, q1), low


def _flash_init(m_scr, acc_scr):
    m_scr[...] = jnp.full(m_scr.shape, NEG, F32)
    acc_scr[...] = jnp.zeros(acc_scr.shape, F32)


def _flash_step(qs, k128, v0, v1, biases, m_scr, acc_scr, tq):
    half = 4 * tq
    kc = k128.shape[0]
    s = jnp.concatenate([_dot_nt(qs[0], k128), _dot_nt(qs[1], k128)], axis=0)
    s = s + jnp.concatenate([biases[0]] * 4 + [biases[1]] * 4, axis=0)
    m_prev = m_scr[...]
    m_new = jnp.maximum(m_prev, jnp.max(s, axis=1, keepdims=True))
    alpha = jnp.exp2(m_prev - m_new)
    p = jnp.exp2(s - jnp.concatenate([m_new] * (kc // LANES), axis=1)).astype(BF16)
    m_scr[...] = m_new
    pv = jnp.concatenate([_dot(p[:half], v0), _dot(p[half:], v1)], axis=0)
    acc_scr[...] = alpha * acc_scr[...] + pv


def _flash_result(acc_scr):
    acc = acc_scr[...]
    l = pltpu.roll(acc, HEAD_DIM, 1)
    return jnp.where(l > 0.0, acc / l, 0.0)


def _softmax_rows(s):
    m = jnp.maximum(jnp.max(s, axis=1, keepdims=True), NEG)
    p = jnp.exp2(s - m)
    l = jnp.sum(p, axis=1, keepdims=True)
    return jnp.where(l > 0.0, p / l, 0.0)


def _key_of(v):
    bits = lax.bitcast_convert_type(v, I32)
    return jnp.where(bits < 0, bits ^ 0x7FFFFFFF, bits)


def _value_of(key):
    return lax.bitcast_convert_type(jnp.where(key < 0, key ^ 0x7FFFFFFF, key), F32)


VALUE_SEARCH_ROUNDS = 26
MAX_SEARCH_ROUNDS = VALUE_SEARCH_ROUNDS + 34
ROUNDS_PER_EXIT_TEST = 3


def _dsa_kernel(qi_ref, qa_ref, misc_ref, ki_ref, kv_ref, o_ref,
                key_scr, jsel_scr, m_scr, acc_scr, *, tq, kc, q_pos0, topk, n_chunks):
    q0 = q_pos0 + pl.program_id(1) * tq
    row = lax.broadcasted_iota(I32, (tq, 1), 0)
    qpos = q0 + row
    whole = tq <= 16
    nkc = n_chunks if whole else jnp.minimum(lax.div(q0 + tq - 1, kc) + 1, n_chunks)
    lane = lax.broadcasted_iota(I32, (tq, LANES), 1)
    low = lane < IDX_DIM
    col_idx = lax.broadcasted_iota(I32, (tq, kc), 1)
    n_tiles = kc // LANES

    qi = qi_ref[0].astype(F32)
    misc = misc_ref[0]
    q_heads, w_heads = [], []
    for hh in range(N_IDX_HEADS):
        c = qi[:, (hh // 2) * LANES:(hh // 2 + 1) * LANES]
        q_heads.append(jnp.where(low, c, 0.0) if hh % 2 == 0 else jnp.where(low, 0.0, c))
        w_heads.append(misc[:, hh:hh + 1])
    q_all = jnp.concatenate(q_heads, axis=0).astype(BF16)

    def score_body(c, carry):
        vmin, vmax = carry
        k0 = pl.multiple_of(c * kc, kc)
        d = _dot_nt(q_all, ki_ref[0, pl.ds(k0, kc), :])
        acc = None
        for hh in range(N_IDX_HEADS):
            t = w_heads[hh] * jnp.maximum(d[hh * tq:(hh + 1) * tq], 0.0)
            acc = t if acc is None else acc + t
        acc = jnp.where(acc == 0.0, 0.0, acc)
        adm = k0 + col_idx <= qpos
        key_scr[c] = _key_of(jnp.where(adm, acc, NEG))
        lo_c = jnp.where(adm, acc, jnp.inf)
        hi_c = jnp.where(adm, acc, -jnp.inf)
        for j in range(n_tiles):
            vmin = jnp.minimum(vmin, lo_c[:, j * LANES:(j + 1) * LANES])
            vmax = jnp.maximum(vmax, hi_c[:, j * LANES:(j + 1) * LANES])
        return vmin, vmax

    vmin, vmax = lax.fori_loop(0, nkc, score_body, (jnp.full((tq, LANES), jnp.inf, F32),
                                                    jnp.full((tq, LANES), -jnp.inf, F32)))
    vmin = jnp.min(vmin, axis=1, keepdims=True)
    vmax = jnp.max(vmax, axis=1, keepdims=True)

    def count_ge(cands):
        if whole:
            keys = key_scr[...]
            counts = []
            for cand in cands:
                hit = jnp.sum(jnp.where(keys >= cand[None], 1.0, 0.0), axis=0)
                part = hit[:, :LANES]
                for j in range(1, n_tiles):
                    part = part + hit[:, j * LANES:(j + 1) * LANES]
                counts.append(jnp.sum(part, axis=1, keepdims=True))
            return counts

        def body(c, parts):
            keys = key_scr[c]
            out = []
            for cand, part in zip(cands, parts):
                hit = jnp.where(keys >= cand, 1.0, 0.0)
                for j in range(n_tiles):
                    part = part + hit[:, j * LANES:(j + 1) * LANES]
                out.append(part)
            return tuple(out)
        parts = lax.fori_loop(0, nkc, body, tuple(jnp.zeros((tq, LANES), F32) for _ in cands))
        return [jnp.sum(part, axis=1, keepdims=True) for part in parts]

    kf = float(topk)
    kmin = _key_of(vmin)
    zero = jnp.zeros((tq, 1), I32)
    c_all, c_nonneg, c_pos = count_ge([kmin, zero, zero + 1])
    at_zero = jnp.logical_and(c_pos < kf, c_nonneg >= kf)
    above = c_pos >= kf
    lo0 = jnp.where(at_zero, 0, jnp.where(above, 1, kmin))
    clo0 = jnp.where(at_zero, c_nonneg, jnp.where(above, c_pos, c_all))
    hi0 = jnp.where(above, _key_of(vmax) + 1, 0)
    chi0 = jnp.where(above, 0.0, c_nonneg)
    take_all = c_all <= kf
    lo0 = jnp.where(take_all, kmin, lo0)
    clo0 = jnp.where(take_all, c_all, clo0)
    todo0 = jnp.where(jnp.logical_or(jnp.logical_or(take_all, at_zero), clo0 == kf), 0.0, 1.0)

    def search_cond(carry):
        r, _, _, _, _, todo = carry
        return jnp.logical_and(r < MAX_SEARCH_ROUNDS, jnp.max(todo) > 0.5)

    def search_body(carry):
        return lax.fori_loop(0, ROUNDS_PER_EXIT_TEST, lambda _, c: search_round(c), carry)

    def search_round(carry):
        r, lo, hi, clo, chi, todo = carry
        vlo = _value_of(lo)
        vmid = vlo + (_value_of(hi) - vlo) * 0.5
        kmid = lax.shift_right_arithmetic(lo, 1) + lax.shift_right_arithmetic(hi, 1) + (lo & hi & 1)
        mid = jnp.where(r < VALUE_SEARCH_ROUNDS, _key_of(vmid), kmid)
        mid = jnp.minimum(jnp.maximum(mid, lo + 1), hi - 1)
        cnt, = count_ge([mid])
        active = todo > 0.5
        up = jnp.logical_and(active, cnt >= kf)
        down = jnp.logical_and(active, cnt < kf)
        lo = jnp.where(up, mid, lo)
        clo = jnp.where(up, cnt, clo)
        hi = jnp.where(down, mid, hi)
        chi = jnp.where(down, cnt, chi)
        done = jnp.logical_or(clo == kf, hi == lo + 1)
        return r + 1, lo, hi, clo, chi, jnp.where(done, 0.0, todo)

    _, lo, _, clo, _, _ = lax.while_loop(
        search_cond, search_body, (jnp.int32(0), lo0, hi0, clo0, chi0, todo0))
    lo1 = lo + 1
    excess = jnp.where(take_all, 0.0, clo - kf)

    jsel_scr[...] = jnp.full(jsel_scr.shape, n_chunks * kc, I32)
    n_bits = max(1, int(math.ceil(math.log2(n_chunks * kc))))

    @pl.when(jnp.max(excess) > 0.5)
    def _():
        c_above, = count_ge([lo1])
        need = kf - c_above

        def count_eq_upto(t):
            def body(c, part):
                hit = jnp.where(key_scr[c] == lo, jnp.where(c * kc + col_idx <= t, 1.0, 0.0), 0.0)
                for j in range(n_tiles):
                    part = part + hit[:, j * LANES:(j + 1) * LANES]
                return part
            part = lax.fori_loop(0, nkc, body, jnp.zeros((tq, LANES), F32))
            return jnp.sum(part, axis=1, keepdims=True)

        def idx_body(i, jcur):
            step = lax.shift_left(jnp.int32(1), n_bits - 1 - i)
            return jnp.where(count_eq_upto(jcur + step - 1) < need, jcur + step, jcur)

        jfin = lax.fori_loop(0, n_bits, idx_body, jnp.zeros((tq, 1), I32))
        jsel_scr[...] = jnp.broadcast_to(jnp.where(excess > 0.5, jfin, n_chunks * kc), jsel_scr.shape)

    jsel = jsel_scr[:, :1]

    qs, low_q = _stack_group_queries(qa_ref, tq)
    _flash_init(m_scr, acc_scr)

    def att_body(c, carry):
        k0 = pl.multiple_of(c * kc, kc)
        kidx = k0 + col_idx
        thr = jnp.where(kidx <= jsel, lo, lo1)
        bias = jnp.where(key_scr[c] >= thr, jnp.where(kidx <= qpos, 0.0, MASK_VALUE), MASK_VALUE)
        kvc = kv_ref[0, pl.ds(k0, kc), :]
        _flash_step(qs, kvc[:, :LANES], kvc[:, LANES:2 * LANES], kvc[:, 2 * LANES:], (bias, bias),
                    m_scr, acc_scr, tq)
        return carry

    lax.fori_loop(0, nkc, att_body, 0)

    res = _flash_result(acc_scr)
    for g in range(4):
        o = jnp.where(low_q, res[g * tq:(g + 1) * tq], res[(4 + g) * tq:(5 + g) * tq])
        o_ref[0, :, g * LANES:(g + 1) * LANES] = o.astype(o_ref.dtype)


def _dsa(qi, qa, misc, ki_all, kv_all, *, tq, q_pos0, topk):
    b, tq_all, _ = qa.shape
    lp = ki_all.shape[1]
    kc = KEY_CHUNK
    assert lp % kc == 0 and kc >= topk and tq_all % tq == 0
    n_chunks = lp // kc
    kern = functools.partial(_dsa_kernel, tq=tq, kc=kc, q_pos0=q_pos0, topk=topk, n_chunks=n_chunks)
    qspec = lambda w: pl.BlockSpec((1, tq, w), lambda bi, qi_: (bi, qi_, 0))
    kspec = lambda w: pl.BlockSpec((1, lp, w), lambda bi, qi_: (bi, 0, 0))
    return pl.pallas_call(
        kern,
        grid=(b, tq_all // tq),
        in_specs=[qspec(512), qspec(512), qspec(LANES), kspec(LANES), kspec(3 * LANES)],
        out_specs=qspec(512),
        out_shape=jax.ShapeDtypeStruct((b, tq_all, 512), BF16),
        scratch_shapes=[pltpu.VMEM((n_chunks, tq, kc), I32),
                        pltpu.VMEM((tq, LANES), I32),
                        pltpu.VMEM((8 * tq, LANES), F32),
                        pltpu.VMEM((8 * tq, LANES), F32)],
        compiler_params=_cparams(2),
        name="dsa",
    )(qi, qa, misc, ki_all, kv_all)


def _nsa_kernel(qb_ref, misc_ref, cmp_ref, ks_ref, kw_ref, mcs_ref, o_ref,
                m_scr, acc_scr, *, tq, kc, q_pos0, n_cmp, n_chunks, topn,
                win_pos0, win_len, win_rows):
    q0 = q_pos0 + pl.program_id(1) * tq
    row = lax.broadcasted_iota(I32, (tq, 1), 0)
    qpos = q0 + row
    ncp = cmp_ref.shape[1]
    nsp = mcs_ref.shape[1]
    qs, low = _stack_group_queries(qb_ref, tq)
    q0s, q1s = qs
    misc = misc_ref[0]

    cmp_all = cmp_ref[0]
    kc128 = cmp_all[:, :LANES]
    vc128 = cmp_all[:, LANES:]
    n_idx = lax.broadcasted_iota(I32, (tq, ncp), 1)
    bias_c = jnp.where(n_idx < n_cmp,
                       jnp.where(n_idx * CMP_STRIDE + (CMP_LEN - 1) <= qpos, 0.0, MASK_VALUE), MASK_VALUE)
    half = 4 * tq
    s = jnp.concatenate([_dot_nt(q0s, kc128), _dot_nt(q1s, kc128)], axis=0)
    pn = _softmax_rows(s + jnp.concatenate([bias_c] * 8, axis=0)).astype(BF16)
    o_cmp = jnp.concatenate([_dot(pn[:half], vc128), _dot(pn[half:], vc128)], axis=0)
    i8 = _dot(pn, mcs_ref[...])
    imp = [i8[(4 * h) * tq:(4 * h + 1) * tq] + i8[(4 * h + 1) * tq:(4 * h + 2) * tq]
           + i8[(4 * h + 2) * tq:(4 * h + 3) * tq] + i8[(4 * h + 3) * tq:(4 * h + 4) * tq]
           for h in range(N_KV_B)]

    blk = lax.broadcasted_iota(I32, (tq, nsp), 1)
    cur = lax.shift_right_logical(qpos, int(math.log2(SLC_BLOCK)))
    future = blk * SLC_BLOCK > qpos
    work0 = jnp.concatenate(
        [jnp.where(future, NEG, jnp.where(blk == 0, BIG, jnp.where(blk == cur, BIG,
                                                                   jnp.where(blk == cur - 1, BIG, imp[h]))))
         for h in range(N_KV_B)], axis=0)
    blk_f = lax.broadcasted_iota(I32, (2 * tq, nsp), 1).astype(F32)

    def pick_body(i, carry):
        work, chosen = carry
        mx = jnp.max(work, axis=1, keepdims=True)
        first = jnp.min(jnp.where(work == mx, blk_f, 1e9), axis=1, keepdims=True)
        hit = blk_f == first
        return jnp.where(hit, -jnp.inf, work), jnp.where(hit, 1.0, chosen)

    _, chosen = lax.fori_loop(0, topn, pick_body, (work0, jnp.zeros((2 * tq, nsp), F32)))
    sel2 = chosen.astype(BF16)

    nkc = jnp.minimum(lax.div(q0 + tq - 1, kc) + 1, n_chunks)
    col_idx = lax.broadcasted_iota(I32, (tq, kc), 1)
    e_row = lax.broadcasted_iota(I32, (nsp, kc), 0)
    e_col = lax.broadcasted_iota(I32, (nsp, kc), 1)
    _flash_init(m_scr, acc_scr)

    def slc_body(c, carry):
        k0 = pl.multiple_of(c * kc, kc)
        expand = jnp.where(lax.shift_right_logical(k0 + e_col, int(math.log2(SLC_BLOCK))) == e_row,
                           1.0, 0.0).astype(BF16)
        kmask = _dot(sel2, expand)
        causal = k0 + col_idx <= qpos
        kvc = ks_ref[0, pl.ds(k0, kc), :]
        biases = [jnp.where(kmask[h * tq:(h + 1) * tq] > 0.5, jnp.where(causal, 0.0, MASK_VALUE), MASK_VALUE)
                  for h in range(N_KV_B)]
        _flash_step(qs, kvc[:, :LANES], kvc[:, LANES:2 * LANES], kvc[:, 2 * LANES:], biases,
                    m_scr, acc_scr, tq)
        return carry

    lax.fori_loop(0, nkc, slc_body, 0)

    ws = jnp.clip(q0 - WINDOW - win_pos0, 0, win_rows - win_len)
    ws = pl.multiple_of(ws, 16)
    kwc = kw_ref[0, pl.ds(ws, win_len), :]
    kw128 = kwc[:, :LANES]
    vw128 = kwc[:, LANES:]
    kpos = win_pos0 + ws + lax.broadcasted_iota(I32, (tq, win_len), 1)
    dist = qpos - kpos
    bias_w = jnp.where(dist >= 0, jnp.where(dist <= WINDOW, 0.0, MASK_VALUE), MASK_VALUE)
    s = jnp.concatenate([_dot_nt(q0s, kw128), _dot_nt(q1s, kw128)], axis=0)
    pn = _softmax_rows(s + jnp.concatenate([bias_w] * 8, axis=0)).astype(BF16)
    o_win = jnp.concatenate([_dot(pn[:half], vw128), _dot(pn[half:], vw128)], axis=0)
    o_slc = _flash_result(acc_scr)

    def gate(r, h, g):
        ln = _GATE_LANE0 + r * N_HEADS_B + h * G_B + g
        return misc[:, ln:ln + 1]

    for g in range(4):
        halves = []
        for h in range(N_KV_B):
            sl = slice((4 * h + g) * tq, (4 * h + g + 1) * tq)
            halves.append(gate(0, h, g) * o_cmp[sl] + gate(1, h, g) * o_slc[sl] + gate(2, h, g) * o_win[sl])
        o_ref[0, :, g * LANES:(g + 1) * LANES] = jnp.where(low, halves[0], halves[1]).astype(o_ref.dtype)


def _slc_map(n_cmp, n_slc, ncp, nsp):
    st = np.arange(ncp) * CMP_STRIDE
    bs = np.arange(nsp) * SLC_BLOCK
    m = (st[:, None] < bs[None, :] + SLC_BLOCK) & (st[:, None] + CMP_LEN > bs[None, :])
    m &= (np.arange(ncp)[:, None] < n_cmp) & (np.arange(nsp)[None, :] < n_slc)
    return jnp.asarray(m.astype(np.float32), BF16)


def _nsa(qb, misc, cmp_kv, ks_all, kw_all, *, tq, q_pos0, n_cmp, n_slc, win_pos0, win_len):
    b, tq_all, _ = qb.shape
    lp = ks_all.shape[1]
    kc = KEY_CHUNK
    ncp = cmp_kv.shape[1]
    nsp = -(-n_slc // LANES) * LANES
    assert lp % kc == 0 and ncp % LANES == 0 and ncp >= n_cmp and tq_all % tq == 0
    win_rows = kw_all.shape[1]
    mcs = _slc_map(n_cmp, n_slc, ncp, nsp)
    kern = functools.partial(_nsa_kernel, tq=tq, kc=kc, q_pos0=q_pos0, n_cmp=n_cmp,
                             n_chunks=lp // kc, topn=min(SLC_TOPN, n_slc),
                             win_pos0=win_pos0, win_len=win_len, win_rows=win_rows)
    qspec = lambda w: pl.BlockSpec((1, tq, w), lambda bi, qi_: (bi, qi_, 0))
    full = lambda n, w: pl.BlockSpec((1, n, w), lambda bi, qi_: (bi, 0, 0))
    return pl.pallas_call(
        kern,
        grid=(b, tq_all // tq),
        in_specs=[qspec(512), qspec(LANES), full(ncp, 256), full(lp, 3 * LANES), full(win_rows, 256),
                  _const_spec(mcs.shape)],
        out_specs=qspec(512),
        out_shape=jax.ShapeDtypeStruct((b, tq_all, 512), BF16),
        scratch_shapes=[pltpu.VMEM((8 * tq, LANES), F32),
                        pltpu.VMEM((8 * tq, LANES), F32)],
        compiler_params=_cparams(2),
        name="nsa",
    )(qb, misc, cmp_kv, ks_all, kw_all, mcs)


def _finish_kernel(*refs, tm, packed, t_new, tiles_per_batch):
    if packed:
        (x_ref, oa_ref, ob_ref, mg_ref, mod_ref, e1_ref, e2_ref, tpos_ref, wpa, wpb, wo, ln1g, ln1b,
         wg, wu, cw, cb, wd, ln2g, ln2b, y_o, tail_o, ext) = refs
    else:
        (x_ref, oa_ref, ob_ref, mg_ref, mod_ref, wpa, wpb, wo, ln1g, ln1b, wg, wu,
         cw, cb, wd, ln2g, ln2b, y_o, tail_o, ext) = refs
    d = D_MODEL
    mod = mod_ref[0]
    gate1, shift2, scale2, gate2 = (mod[:, i * d:(i + 1) * d] for i in range(4))
    x = x_ref[...]
    mg = mg_ref[...].astype(F32)
    m = mg[:, :d] * _dot(oa_ref[...], wpa[...]) + mg[:, d:] * _dot(ob_ref[...], wpb[...])
    x1 = _layernorm(DN_ALPHA * x + (1.0 + gate1) * _dot(m.astype(BF16), wo[...]), ln1g[...], ln1b[...])
    h2 = (x1 * (1.0 + scale2) + shift2).astype(BF16)
    ug = _dot(h2, wg[...])
    u = _dot(h2, wu[...])

    if packed:
        ext[0:8, :] = jnp.zeros((8, D_FF), F32)
    else:
        @pl.when(pl.program_id(0) % tiles_per_batch == 0)
        def _():
            ext[0:8, :] = jnp.zeros((8, D_FF), F32)
    ext[8:8 + tm, :] = ug
    prev1 = ext[7:7 + tm, :]
    prev2 = ext[6:6 + tm, :]
    if packed:
        tpos = tpos_ref[:, :1]
        prev1 = jnp.where(tpos >= 1, prev1, e1_ref[...])
        prev2 = jnp.where(tpos >= 2, prev2, e2_ref[...])
    w = cw[...]
    conv = cb[...] + w[2:3, :] * ug
    conv = conv + w[0:1, :] * prev2
    conv = conv + w[1:2, :] * prev1
    tail = ext[tm:tm + 8, :]
    ext[0:8, :] = tail
    if packed:
        tail_o[...] = ug
    else:
        tail_o[0] = tail
    act = (_gelu_tanh(conv) * u).astype(BF16)
    f = _dot(act, wd[...])
    y_o[...] = _layernorm(DN_ALPHA * x1 + (1.0 + gate2) * f, ln2g[...], ln2b[...])


def _finish(x_flat, oa, ob, mg, mod, hist, fw, *, tm, packed, t_new, tiles_per_batch):
    rows = x_flat.shape[0]
    n_tiles = rows // tm
    r_mod = mod.shape[1]
    row_spec = lambda w: pl.BlockSpec((tm, w), lambda i: (i, 0))
    in_specs = [row_spec(D_MODEL), row_spec(512), row_spec(512), row_spec(2 * D_MODEL),
                pl.BlockSpec((1, r_mod, 4 * D_MODEL), lambda i: (i // tiles_per_batch, 0, 0))]
    args = [x_flat, oa, ob, mg, mod]
    if packed:
        in_specs += [row_spec(D_FF), row_spec(D_FF), row_spec(LANES)]
        tpos = jnp.broadcast_to((jnp.arange(rows, dtype=I32) % t_new)[:, None], (rows, LANES))
        args += list(hist) + [tpos]
        tail_shape = jax.ShapeDtypeStruct((rows, D_FF), F32)
        tail_spec = row_spec(D_FF)
    else:
        nb = n_tiles // tiles_per_batch
        tail_shape = jax.ShapeDtypeStruct((nb, 8, D_FF), F32)
        tail_spec = pl.BlockSpec((1, 8, D_FF), lambda i: (i // tiles_per_batch, 0, 0))
    in_specs += [_const_spec(a.shape) for a in fw]
    args += list(fw)
    kern = functools.partial(_finish_kernel, tm=tm, packed=packed, t_new=t_new,
                             tiles_per_batch=tiles_per_batch)
    return pl.pallas_call(
        kern,
        grid=(n_tiles,),
        in_specs=in_specs,
        out_specs=[row_spec(D_MODEL), tail_spec],
        out_shape=[jax.ShapeDtypeStruct((rows, D_MODEL), F32), tail_shape],
        scratch_shapes=[pltpu.VMEM((tm + 8, D_FF), F32)],
        compiler_params=_cparams(1),
        name="finish",
    )(*args)


def kernel(x_prompt, x_sample, cache_a_kv, cache_a_idx, cache_b_cmp_kv, cache_b_slc_kv, state_b_win_kv, state_ffn_conv, page_table, c_prompt, c_sample, w_ada, b_ada, w_in, cmp_pe, cmp_w1, cmp_b1, cmp_w2, cmp_b2, w_pa, w_pb, w_o, ln1_g, ln1_b, w_ffn_gate, w_ffn_up, conv_w, conv_b, w_down, ln2_g, ln2_b):
    d = D_MODEL
    b, s, _ = x_prompt.shape
    bd, tn, _ = x_sample.shape
    n_pages = page_table.shape[1]
    past = n_pages * PAGE_SIZE
    kvw = 2 * N_KV_B * HEAD_DIM
    assert s % PROJ_ROWS == 0 and s % FINISH_ROWS == 0 and s % KEY_CHUNK == 0 and tn <= 8
    assert s >= WINDOW + Q_TILE and state_b_win_kv.shape[1] == WINDOW and tn >= CONV_W - 1

    perm = _proj_perm()
    w_perm = jnp.where(perm[None, :] >= 0, jnp.take(w_in, np.maximum(perm, 0), axis=1), 0.0).astype(BF16)
    head_rows = np.concatenate([np.r_[g * HEAD_DIM:(g + 1) * HEAD_DIM, (4 + g) * HEAD_DIM:(5 + g) * HEAD_DIM]
                                for g in range(4)])
    vec = lambda a: a.reshape(1, -1)
    fw = (w_pa[head_rows].astype(BF16), w_pb[head_rows].astype(BF16), w_o.astype(BF16),
          vec(ln1_g), vec(ln1_b), w_ffn_gate.astype(BF16), w_ffn_up.astype(BF16),
          jnp.concatenate([conv_w, jnp.zeros((8 - CONV_W, D_FF), F32)], axis=0), vec(conv_b),
          w_down.astype(BF16), vec(ln2_g), vec(ln2_b))
    cw = _compress_weights(cmp_pe, cmp_w1, cmp_b1, cmp_w2, cmp_b2)

    rows_c = -(-(b + bd) // 16) * 16
    c_all = jnp.concatenate([c_prompt, c_sample, jnp.zeros((rows_c - b - bd, d), F32)], axis=0)
    ada = _ada(c_all, w_ada.astype(BF16), b_ada)
    ada_p, ada_s = ada[:b], ada[b:b + bd]

    cos_p, sin_p = _rope_tables(jnp.arange(s))
    (qa, qi, qb, kva_bf, ki_bf, kv_cmp_bf, kv_slc_bf, kv_win_bf, misc, mg,
     kva_t, ki_t, kv_cmp_t, kv_slc_t, kv_win_t) = _project(
        x_prompt.reshape(b * s, d), ada_p[:, None, :2 * d], cos_p, sin_p, w_perm,
        PROJ_ROWS, s // PROJ_ROWS, s // PROJ_ROWS, n_batch=b)
    r3 = lambda a: a.reshape(b, s, a.shape[-1])
    oa = _dsa(r3(qi), r3(qa), r3(misc), r3(ki_bf), r3(kva_bf),
              tq=Q_TILE, q_pos0=0, topk=min(DSA_TOPK, s // 4))
    n_ch = -(-s // CMP_STRIDE)
    assert s % CMP_STRIDE == 0 and n_ch % LANES == 0
    cmp_p = _compress(kv_cmp_bf.reshape(b, n_ch, CMP_STRIDE * kvw), cw)
    ob = _nsa(r3(qb), r3(misc), cmp_p, r3(kv_slc_bf), r3(kv_win_bf),
              tq=Q_TILE, q_pos0=0, n_cmp=n_ch - 1, n_slc=-(-s // SLC_BLOCK),
              win_pos0=0, win_len=WINDOW + Q_TILE)
    y_p, tail_p = _finish(x_prompt.reshape(b * s, d), oa.reshape(b * s, 512), ob.reshape(b * s, 512), mg,
                          ada_p[:, None, 2 * d:], None, fw, tm=FINISH_ROWS, packed=False, t_new=0,
                          tiles_per_batch=s // FINISH_ROWS)

    rows_s = bd * tn
    pos_s = past + jnp.arange(tn)
    cos_s, sin_s = _rope_tables(jnp.tile(pos_s, bd))
    rep = lambda a: jnp.repeat(a, tn, axis=0)[None]
    (qa_s, qi_s, qb_s, _, _, _, _, kv_win_s_bf, misc_s, mg_s,
     kva_s, ki_s, kv_cmp_s, kv_slc_s, kv_win_s) = _project(
        x_sample.reshape(rows_s, d), rep(ada_s[:, :2 * d]), cos_s, sin_s, w_perm, rows_s, 1, 1)

    def new16(a):
        a = a.reshape(bd, tn, a.shape[-1])
        return jnp.concatenate([a, jnp.zeros((bd, 16 - tn, a.shape[-1]), F32)], axis=1)

    def q8(a):
        a = a.reshape(bd, tn, a.shape[-1])
        return jnp.concatenate([a, jnp.zeros((bd, 8 - tn, a.shape[-1]), a.dtype)], axis=1)

    ki_all = _gather_pages(cache_a_idx, page_table, new16(ki_s), mode="dup")
    kva_all = _gather_pages(cache_a_kv.reshape(-1, PAGE_SIZE, kvw), page_table, new16(kva_s), mode="ones")
    cmp_all = _gather_pages(cache_b_cmp_kv.reshape(-1, PAGE_SIZE, kvw), page_table, new16(kv_cmp_s))
    slc_all = _gather_pages(cache_b_slc_kv.reshape(-1, PAGE_SIZE, kvw), page_table, new16(kv_slc_s), mode="ones")
    l_tot = past + tn
    lp = kva_all.shape[1]
    oa_s = _dsa(q8(qi_s), q8(qa_s), q8(misc_s), ki_all, kva_all,
                tq=8, q_pos0=past, topk=min(DSA_TOPK, l_tot // 4))
    assert (lp // CMP_STRIDE) % LANES == 0
    cmp_s = _compress(cmp_all.reshape(bd, lp // CMP_STRIDE, CMP_STRIDE * kvw), cw)
    win_len = WINDOW + Q_TILE
    win_all = jnp.concatenate(
        [state_b_win_kv.reshape(bd, WINDOW, kvw).astype(BF16), kv_win_s_bf.reshape(bd, tn, kvw),
         jnp.zeros((bd, win_len - WINDOW - tn, kvw), BF16)], axis=1)
    ob_s = _nsa(q8(qb_s), q8(misc_s), cmp_s, slc_all, win_all,
                tq=8, q_pos0=past, n_cmp=-(-l_tot // CMP_STRIDE) - 1, n_slc=-(-l_tot // SLC_BLOCK),
                win_pos0=past - WINDOW, win_len=win_len)
    hist = state_ffn_conv
    zrow = jnp.zeros((bd, 1, D_FF), F32)
    e1 = jnp.concatenate([hist[:, 1:2]] + [zrow] * (tn - 1), axis=1).reshape(rows_s, D_FF)
    e2 = jnp.concatenate([hist[:, 0:1], hist[:, 1:2]] + [zrow] * (tn - 2), axis=1).reshape(rows_s, D_FF)
    y_s, ug_s = _finish(x_sample.reshape(rows_s, d), oa_s[:, :tn].reshape(rows_s, 512),
                        ob_s[:, :tn].reshape(rows_s, 512), mg_s, rep(ada_s[:, 2 * d:]), (e1, e2), fw,
                        tm=rows_s, packed=True, t_new=tn, tiles_per_batch=1)

    kvshape = (2, N_KV_A, HEAD_DIM)
    win_s = jnp.concatenate([state_b_win_kv[:, tn:], kv_win_s.reshape((bd, tn) + kvshape)], axis=1)
    untr = lambda a: a.reshape((b,) + kvshape + (a.shape[-1],)).transpose(0, 4, 1, 2, 3)
    return (y_p.reshape(b, s, d), y_s.reshape(bd, tn, d),
            untr(kva_t), kva_s.reshape((bd, tn) + kvshape),
            ki_t.transpose(0, 2, 1), ki_s.reshape(bd, tn, IDX_DIM),
            untr(kv_cmp_t), kv_cmp_s.reshape((bd, tn) + kvshape),
            untr(kv_slc_t), kv_slc_s.reshape((bd, tn) + kvshape),
            untr(kv_win_t[:, :, s - min(WINDOW, s):]), win_s,
            tail_p[:, 8 - (CONV_W - 1):], ug_s.reshape(bd, tn, D_FF)[:, tn - (CONV_W - 1):])
```

```python
import functools
import math

import numpy as np
import jax
import jax.numpy as jnp
from jax import lax
from jax.experimental import pallas as pl
from jax.experimental.pallas import tpu as pltpu

D_MODEL = 1024
PAGE_SIZE = 128
HEAD_DIM = 64
N_HEADS_A = 8
N_KV_A = 2
N_IDX_HEADS = 8
IDX_DIM = 64
DSA_TOPK = 256
N_HEADS_B = 8
N_KV_B = 2
G_B = N_HEADS_B // N_KV_B
CMP_LEN = 32
CMP_STRIDE = 16
CMP_HID = 2 * HEAD_DIM
SLC_BLOCK = 64
SLC_TOPN = 16
WINDOW = 512
D_FF = 2816
CONV_W = 3
ROPE_THETA = 10000.0
LN_EPS = 1e-5
DEPTH = 1
DN_ALPHA = (2 * DEPTH) ** 0.25
NEG = -1e30
BIG = 1e30
MASK_VALUE = -2e30

LANES = 128
VMEM_LIMIT_BYTES = 56 * 1024 * 1024
Q_TILE = 256
KEY_CHUNK = 512
DECODE_KEY_CHUNK = 2048
PAGES_PER_STEP = 16
PROJ_ROWS = 512
FINISH_ROWS = 256

F32 = jnp.float32
BF16 = jnp.bfloat16
I32 = jnp.int32
INT_MIN = -2 ** 31

_O_QA = 0
_O_KVA = _O_QA + N_HEADS_A * HEAD_DIM
_O_QI = _O_KVA + 2 * N_KV_A * HEAD_DIM
_O_KI = _O_QI + N_IDX_HEADS * IDX_DIM
_O_WI = _O_KI + IDX_DIM
_O_QB = _O_WI + N_IDX_HEADS
_O_KVB = _O_QB + N_HEADS_B * HEAD_DIM
_O_GB = _O_KVB + 6 * N_KV_B * HEAD_DIM
_O_MGA = _O_GB + 3 * N_HEADS_B
_O_MGB = _O_MGA + D_MODEL
_N_IN = _O_MGB + D_MODEL
_GATE_LANE0 = N_IDX_HEADS


def _cparams(n_axes):
    return pltpu.CompilerParams(dimension_semantics=("arbitrary",) * n_axes,
                                vmem_limit_bytes=VMEM_LIMIT_BYTES)


def _const_spec(shape):
    nd = len(shape)
    return pl.BlockSpec(shape, lambda *a: (0,) * nd, pipeline_mode=pl.Buffered(1))


def _dot(a, b):
    return jnp.dot(a, b, preferred_element_type=F32)


def _dot_nt(a, b):
    return lax.dot_general(a, b, (((1,), (1,)), ((), ())), preferred_element_type=F32)


def _sigmoid(x):
    return 1.0 / (1.0 + jnp.exp(-x))


def _gelu_tanh(x):
    return 0.5 * x * (1.0 + jnp.tanh(math.sqrt(2.0 / math.pi) * (x + 0.044715 * (x * x * x))))


def _layernorm(x, g, b):
    mu = jnp.mean(x, axis=-1, keepdims=True)
    xc = x - mu
    var = jnp.mean(xc * xc, axis=-1, keepdims=True)
    return xc * lax.rsqrt(var + LN_EPS) * g + b


def _ada_kernel(c_ref, w_ref, b_ref, o_ref):
    c = c_ref[...]
    s = (c * _sigmoid(c)).astype(BF16)
    o_ref[...] = _dot(s, w_ref[...]) + b_ref[...]


def _ada(c_all, w_ada_bf, b_ada):
    rows = c_all.shape[0]
    n_out = w_ada_bf.shape[1]
    tn = D_MODEL
    return pl.pallas_call(
        _ada_kernel,
        grid=(n_out // tn,),
        in_specs=[pl.BlockSpec((rows, D_MODEL), lambda j: (0, 0)),
                  pl.BlockSpec((D_MODEL, tn), lambda j: (0, j)),
                  pl.BlockSpec((1, tn), lambda j: (0, j))],
        out_specs=pl.BlockSpec((rows, tn), lambda j: (0, j)),
        out_shape=jax.ShapeDtypeStruct((rows, n_out), F32),
        compiler_params=_cparams(1),
        name="ada",
    )(c_all, w_ada_bf, b_ada.reshape(1, n_out))


_P_QA, _P_QI, _P_QB = 0, 512, 1024
_P_K = 1536
_P_V = 2176
_P_MISC = 2688
_P_MG = 2816
_P_END = _P_MG + 2 * D_MODEL


def _proj_perm():
    def paired_heads(base):
        cols = []
        for g in range(4):
            cols += list(range(base + g * HEAD_DIM, base + (g + 1) * HEAD_DIM))
            cols += list(range(base + (4 + g) * HEAD_DIM, base + (5 + g) * HEAD_DIM))
        return cols
    kv = 2 * N_KV_B * HEAD_DIM
    half = N_KV_B * HEAD_DIM
    cols = []
    cols += paired_heads(_O_QA)
    cols += list(range(_O_QI, _O_QI + N_IDX_HEADS * IDX_DIM))
    cols += paired_heads(_O_QB)
    cols += list(range(_O_KVA, _O_KVA + N_KV_A * HEAD_DIM))
    cols += list(range(_O_KI, _O_KI + IDX_DIM)) * 2
    for r in range(3):
        cols += list(range(_O_KVB + r * kv, _O_KVB + r * kv + half))
    cols += list(range(_O_KVA + N_KV_A * HEAD_DIM, _O_KVA + 2 * N_KV_A * HEAD_DIM))
    for r in range(3):
        cols += list(range(_O_KVB + r * kv + half, _O_KVB + (r + 1) * kv))
    misc = list(range(_O_WI, _O_WI + N_IDX_HEADS)) + list(range(_O_GB, _O_GB + 3 * N_HEADS_B))
    cols += misc + [-1] * (LANES - len(misc))
    cols += list(range(_O_MGA, _O_MGA + 2 * D_MODEL))
    assert len(cols) == _P_END
    return np.asarray(cols, np.int32)


def _with_ones(v, low):
    return jnp.where(low, v, 1.0), jnp.where(low, 1.0, v)


def _proj_kernel(x_ref, mod_ref, cos_ref, sin_ref, w_ref,
                 qa_o, qi_o, qb_o, kvab_o, kib_o, cmpb_o, slcb_o, winb_o, misc_o, mg_o,
                 kva_o, ki_o, cmp_o, slc_o, win_o, *, transposed):
    mod = mod_ref[0]
    shift1 = mod[:, :D_MODEL]
    scale1 = mod[:, D_MODEL:]
    h = (x_ref[...] * (1.0 + scale1) + shift1).astype(BF16)
    tm = h.shape[0]
    cos = cos_ref[...]
    sin = sin_ref[...]
    lane = lax.broadcasted_iota(I32, (tm, LANES), 1)
    first_half = (lane & (HEAD_DIM - 1)) < (HEAD_DIM // 2)
    low = lane < HEAD_DIM

    def rope(y):
        partner = jnp.where(first_half, pltpu.roll(y, LANES - HEAD_DIM // 2, 1),
                            pltpu.roll(y, HEAD_DIM // 2, 1))
        return y * cos + partner * sin

    def mm(a, b):
        return _dot(h, w_ref[:, a:b])

    q_scale = HEAD_DIM ** -0.5 * math.log2(math.e)
    for off, out, scale in ((_P_QA, qa_o, q_scale), (_P_QI, qi_o, 1.0), (_P_QB, qb_o, q_scale)):
        y = mm(off, off + 4 * LANES)
        for j in range(4):
            out[:, j * LANES:(j + 1) * LANES] = (rope(y[:, j * LANES:(j + 1) * LANES]) * scale).astype(BF16)

    yk = mm(_P_K, _P_K + 5 * LANES)
    ka, ki2, kc, ks, kw = [rope(yk[:, j * LANES:(j + 1) * LANES]) for j in range(5)]
    yv = mm(_P_V, _P_V + 4 * LANES)
    va, vc, vs, vw = [yv[:, j * LANES:(j + 1) * LANES] for j in range(4)]

    for out, k, v in ((kvab_o, ka, va), (slcb_o, ks, vs)):
        v0, v1 = _with_ones(v, low)
        out[:, :LANES] = k.astype(BF16)
        out[:, LANES:2 * LANES] = v0.astype(BF16)
        out[:, 2 * LANES:] = v1.astype(BF16)
    kib_o[...] = ki2.astype(BF16)
    for out, k, v in ((cmpb_o, kc, vc), (winb_o, kw, vw)):
        out[:, :LANES] = k.astype(BF16)
        out[:, LANES:] = v.astype(BF16)

    for out, k, v in ((kva_o, ka, va), (cmp_o, kc, vc), (slc_o, ks, vs), (win_o, kw, vw)):
        if transposed:
            out[0, :LANES, :] = k.T
            out[0, LANES:, :] = v.T
        else:
            out[:, :LANES] = k
            out[:, LANES:] = v
    if transposed:
        ki_o[0] = ki2.T[:IDX_DIM, :]
    else:
        ki_o[...] = ki2[:, :IDX_DIM]

    ym = mm(_P_MISC, _P_MISC + LANES)
    wi_scale = (N_IDX_HEADS * IDX_DIM) ** -0.5
    misc_o[...] = jnp.where(lane < N_IDX_HEADS, ym * wi_scale, _sigmoid(ym))
    for j in range(2):
        a = _P_MG + j * D_MODEL
        mg_o[:, j * D_MODEL:(j + 1) * D_MODEL] = _sigmoid(mm(a, a + D_MODEL)).astype(BF16)


def _project(x_flat, mod, cos_t, sin_t, w_perm, tm, tiles_per_group, tiles_per_table, n_batch=None):
    rows = x_flat.shape[0]
    n_tiles = rows // tm
    r_mod = mod.shape[1]
    transposed = n_batch is not None

    def row_spec(width):
        return pl.BlockSpec((tm, width), lambda i: (i, 0))

    bf_defs = [(512, BF16), (512, BF16), (512, BF16),
               (384, BF16), (LANES, BF16),
               (256, BF16), (384, BF16), (256, BF16),
               (LANES, F32),
               (2 * D_MODEL, BF16)]
    f32_widths = [256, IDX_DIM, 256, 256, 256]
    out_specs = [row_spec(w) for w, _ in bf_defs]
    out_shape = [jax.ShapeDtypeStruct((rows, w), dt) for w, dt in bf_defs]
    if transposed:
        s = rows // n_batch
        tpb = s // tm
        out_specs += [pl.BlockSpec((1, w, tm), lambda i: (i // tpb, 0, i % tpb)) for w in f32_widths]
        out_shape += [jax.ShapeDtypeStruct((n_batch, w, s), F32) for w in f32_widths]
    else:
        out_specs += [row_spec(w) for w in f32_widths]
        out_shape += [jax.ShapeDtypeStruct((rows, w), F32) for w in f32_widths]
    return pl.pallas_call(
        functools.partial(_proj_kernel, transposed=transposed),
        grid=(n_tiles,),
        in_specs=[row_spec(D_MODEL),
                  pl.BlockSpec((1, r_mod, 2 * D_MODEL), lambda i: (i // tiles_per_group, 0, 0)),
                  pl.BlockSpec((tm, LANES), lambda i: (i % tiles_per_table, 0)),
                  pl.BlockSpec((tm, LANES), lambda i: (i % tiles_per_table, 0)),
                  _const_spec((D_MODEL, _P_END))],
        out_specs=out_specs,
        out_shape=out_shape,
        compiler_params=_cparams(1),
        name="proj",
    )(x_flat, mod, cos_t, sin_t, w_perm)


def _rope_tables(pos):
    half = HEAD_DIM // 2
    inv = ROPE_THETA ** (-jnp.arange(half, dtype=F32) / half)
    ang = pos.astype(F32)[:, None] * inv[None, :]
    cos = jnp.tile(jnp.cos(ang), (1, LANES // half))
    sign = np.where((np.arange(LANES) % HEAD_DIM) < half, -1.0, 1.0).astype(np.float32)
    sin = jnp.tile(jnp.sin(ang), (1, LANES // half)) * sign[None, :]
    return cos, sin


def _gather_kernel(pt_ref, *refs, n_groups, mode):
    pages = refs[:PAGES_PER_STEP]
    new_ref = refs[PAGES_PER_STEP]
    out_ref = refs[PAGES_PER_STEP + 1]
    g = pl.program_id(1)

    if mode == "chunks":
        xs = refs[PAGES_PER_STEP + 2]
        n_rows = PAGES_PER_STEP * PAGE_SIZE // CMP_STRIDE
        w = 2 * LANES

        @pl.when(g < n_groups)
        def _():
            for i in range(PAGES_PER_STEP):
                x = pages[i][0].T
                xs[0, i * PAGE_SIZE:(i + 1) * PAGE_SIZE, :] = x[:, :LANES]
                xs[1, i * PAGE_SIZE:(i + 1) * PAGE_SIZE, :] = x[:, LANES:]
            for c in range(CMP_STRIDE):
                for hf in range(2):
                    a = c * w + hf * LANES
                    out_ref[0, :, a:a + LANES] = xs[hf, pl.ds(c, n_rows, stride=CMP_STRIDE), :].astype(BF16)

        @pl.when(g == n_groups)
        def _():
            new = new_ref[0]
            first = jnp.concatenate([new[c:c + 1, :] for c in range(CMP_STRIDE)], axis=1)
            out_ref[0] = jnp.zeros(out_ref.shape[1:], BF16)
            out_ref[0, :16, :] = jnp.concatenate([first, jnp.zeros((15, CMP_STRIDE * w), F32)],
                                                 axis=0).astype(BF16)
        return

    def put(r0, rows):
        n = rows.shape[0]
        if mode == "ones":
            low = lax.broadcasted_iota(I32, (n, LANES), 1) < HEAD_DIM
            v0, v1 = _with_ones(rows[:, LANES:], low)
            out_ref[0, r0:r0 + n, :LANES] = rows[:, :LANES].astype(BF16)
            out_ref[0, r0:r0 + n, LANES:2 * LANES] = v0.astype(BF16)
            out_ref[0, r0:r0 + n, 2 * LANES:] = v1.astype(BF16)
        elif mode == "dup":
            out_ref[0, r0:r0 + n, :] = jnp.concatenate([rows, rows], axis=1).astype(BF16)
        else:
            out_ref[0, r0:r0 + n, :] = rows.astype(BF16)

    @pl.when(g < n_groups)
    def _():
        for i in range(PAGES_PER_STEP):
            put(i * PAGE_SIZE, pages[i][0].T)

    @pl.when(g == n_groups)
    def _():
        out_ref[0] = jnp.zeros(out_ref.shape[1:], BF16)
        put(0, new_ref[0])


def _gather_pages(cache_t, page_table, new_rows, mode="plain"):
    n_phys, w, _ = cache_t.shape
    b, n_pages = page_table.shape
    assert n_pages % PAGES_PER_STEP == 0
    n_groups = n_pages // PAGES_PER_STEP
    rows_step = PAGES_PER_STEP * PAGE_SIZE
    scratch = []
    if mode == "chunks":
        assert w == 2 * LANES
        rows_step, w_out = rows_step // CMP_STRIDE, CMP_STRIDE * w
        scratch = [pltpu.VMEM((2, PAGES_PER_STEP * PAGE_SIZE, LANES), F32)]
    else:
        w_out = {"plain": w, "ones": w + LANES, "dup": 2 * w}[mode]

    def page_spec(i):
        def imap(bi, g, pt):
            gg = jnp.minimum(g, n_groups - 1)
            return (pt[bi, gg * PAGES_PER_STEP + i], 0, 0)
        return pl.BlockSpec((1, w, PAGE_SIZE), imap)

    grid_spec = pltpu.PrefetchScalarGridSpec(
        num_scalar_prefetch=1,
        grid=(b, n_groups + 1),
        in_specs=[page_spec(i) for i in range(PAGES_PER_STEP)]
        + [pl.BlockSpec((1, new_rows.shape[1], w), lambda bi, g, pt: (bi, 0, 0))],
        out_specs=pl.BlockSpec((1, rows_step, w_out), lambda bi, g, pt: (bi, g, 0)),
        scratch_shapes=scratch,
    )
    return pl.pallas_call(
        functools.partial(_gather_kernel, n_groups=n_groups, mode=mode),
        grid_spec=grid_spec,
        out_shape=jax.ShapeDtypeStruct((b, (n_groups + 1) * rows_step, w_out), BF16),
        compiler_params=_cparams(2),
        name="gather_pages",
    )(page_table, *([cache_t] * PAGES_PER_STEP), new_rows)


def _cmp_kernel(x_ref, wa_ref, wb_ref, pea_ref, peb_ref, b1_ref, w2_ref, b2_ref, o_ref):
    x = x_ref[0].astype(BF16)
    r = x.shape[0]
    ha = _dot(x, wa_ref[...])
    hb = _dot(x, wb_ref[...])
    hpe = _dot(pea_ref[...], wa_ref[...]) + _dot(peb_ref[...], wb_ref[...])
    hpe = hpe[0:1, :] + b1_ref[...]
    hb_next = pltpu.roll(hb, r - 1, 0)
    hid = _gelu_tanh(ha + hb_next + hpe).astype(BF16)
    o_ref[0] = (_dot(hid, w2_ref[...]) + b2_ref[...]).astype(o_ref.dtype)


def _compress_weights(cmp_pe, cmp_w1, cmp_b1, cmp_w2, cmp_b2):
    eye = jnp.eye(2, dtype=F32)
    kvw = 2 * N_KV_B * HEAD_DIM

    def big(w1h):
        t = jnp.einsum('jcdf,jJ,hH->cjhdJHf', w1h, eye, eye)
        return t.reshape(CMP_STRIDE * kvw, 2 * N_KV_B * CMP_HID).astype(BF16)

    def pe_row(peh):
        t = jnp.broadcast_to(peh.transpose(1, 0, 2)[:, :, None, :], (CMP_STRIDE, 2, N_KV_B, HEAD_DIM))
        row = t.reshape(1, CMP_STRIDE * kvw)
        return jnp.concatenate([row, jnp.zeros((15, CMP_STRIDE * kvw), F32)], axis=0).astype(BF16)

    wa = big(cmp_w1[:, :CMP_STRIDE])
    wb = big(cmp_w1[:, CMP_STRIDE:])
    pea = pe_row(cmp_pe[:, :CMP_STRIDE])
    peb = pe_row(cmp_pe[:, CMP_STRIDE:])
    b1 = jnp.broadcast_to(cmp_b1[:, None, :], (2, N_KV_B, CMP_HID)).reshape(1, -1)
    w2 = jnp.einsum('jfd,jJ,hH->jhfJHd', cmp_w2, eye, eye).reshape(2 * N_KV_B * CMP_HID, kvw).astype(BF16)
    b2 = jnp.broadcast_to(cmp_b2[:, None, :], (2, N_KV_B, HEAD_DIM)).reshape(1, -1)
    return wa, wb, pea, peb, b1, w2, b2


def _compress(x_chunks, cw):
    b, r, cwid = x_chunks.shape
    wa, wb, pea, peb, b1, w2, b2 = cw
    return pl.pallas_call(
        _cmp_kernel,
        grid=(b,),
        in_specs=[pl.BlockSpec((1, r, cwid), lambda i: (i, 0, 0)),
                  _const_spec(wa.shape), _const_spec(wb.shape), _const_spec(pea.shape),
                  _const_spec(peb.shape), _const_spec(b1.shape), _const_spec(w2.shape),
                  _const_spec(b2.shape)],
        out_specs=pl.BlockSpec((1, r, w2.shape[1]), lambda i: (i, 0, 0)),
        out_shape=jax.ShapeDtypeStruct((b, r, w2.shape[1]), BF16),
        compiler_params=_cparams(1),
        name="compress",
    )(x_chunks, wa, wb, pea, peb, b1, w2, b2)


def _stack_group_queries(q_ref, tq):
    lane = lax.broadcasted_iota(I32, (tq, LANES), 1)
    low = lane < HEAD_DIM
    q = q_ref[0].astype(F32)
    cols = [q[:, g * LANES:(g + 1) * LANES] for g in range(4)]
    q0 = jnp.concatenate([jnp.where(low, c, 0.0) for c in cols], axis=0).astype(BF16)
    q1 = jnp.concatenate([jnp.where(low, 0.0, c) for c in cols], axis=0).astype(BF16)
    return (q0, q1), low


def _flash_init(m_scr, acc_scr):
    m_scr[...] = jnp.full(m_scr.shape, NEG, F32)
    acc_scr[...] = jnp.zeros(acc_scr.shape, F32)


def _flash_step(qs, k128, v0, v1, biases, m_scr, acc_scr, tq):
    half = 4 * tq
    kc = k128.shape[0]
    s = jnp.concatenate([_dot_nt(qs[0], k128), _dot_nt(qs[1], k128)], axis=0)
    s = s + jnp.concatenate([biases[0]] * 4 + [biases[1]] * 4, axis=0)
    m_prev = m_scr[...]
    m_new = jnp.maximum(m_prev, jnp.max(s, axis=1, keepdims=True))
    alpha = jnp.exp2(m_prev - m_new)
    p = jnp.exp2(s - jnp.concatenate([m_new] * (kc // LANES), axis=1)).astype(BF16)
    m_scr[...] = m_new
    pv = jnp.concatenate([_dot(p[:half], v0), _dot(p[half:], v1)], axis=0)
    acc_scr[...] = alpha * acc_scr[...] + pv


def _flash_result(acc_scr):
    acc = acc_scr[...]
    l = pltpu.roll(acc, HEAD_DIM, 1)
    return jnp.where(l > 0.0, acc / l, 0.0)


def _softmax_rows(s):
    m = jnp.maximum(jnp.max(s, axis=1, keepdims=True), NEG)
    p = jnp.exp2(s - m)
    l = jnp.sum(p, axis=1, keepdims=True)
    return jnp.where(l > 0.0, p / l, 0.0)


def _key_of(v):
    bits = lax.bitcast_convert_type(v, I32)
    return jnp.where(bits < 0, bits ^ 0x7FFFFFFF, bits)


def _value_of(key):
    return lax.bitcast_convert_type(jnp.where(key < 0, key ^ 0x7FFFFFFF, key), F32)


VALUE_SEARCH_ROUNDS = 26
MAX_SEARCH_ROUNDS = VALUE_SEARCH_ROUNDS + 34
ROUNDS_PER_EXIT_TEST = 3


def _dsa_kernel(qi_ref, qa_ref, misc_ref, ki_ref, kv_ref, o_ref,
                key_scr, jsel_scr, m_scr, acc_scr, *, tq, kc, q_pos0, topk, n_chunks):
    q0 = q_pos0 + pl.program_id(1) * tq
    row = lax.broadcasted_iota(I32, (tq, 1), 0)
    qpos = q0 + row
    whole = tq <= 16
    nkc = n_chunks if whole else jnp.minimum(lax.div(q0 + tq - 1, kc) + 1, n_chunks)
    lane = lax.broadcasted_iota(I32, (tq, LANES), 1)
    low = lane < IDX_DIM
    col_idx = lax.broadcasted_iota(I32, (tq, kc), 1)
    n_tiles = kc // LANES

    qi = qi_ref[0].astype(F32)
    misc = misc_ref[0]
    q_heads, w_heads = [], []
    for hh in range(N_IDX_HEADS):
        c = qi[:, (hh // 2) * LANES:(hh // 2 + 1) * LANES]
        q_heads.append(jnp.where(low, c, 0.0) if hh % 2 == 0 else jnp.where(low, 0.0, c))
        w_heads.append(misc[:, hh:hh + 1])
    q_all = jnp.concatenate(q_heads, axis=0).astype(BF16)

    def score_body(c, carry):
        vmin, vmax = carry
        k0 = pl.multiple_of(c * kc, kc)
        d = _dot_nt(q_all, ki_ref[0, pl.ds(k0, kc), :])
        acc = None
        for hh in range(N_IDX_HEADS):
            t = w_heads[hh] * jnp.maximum(d[hh * tq:(hh + 1) * tq], 0.0)
            acc = t if acc is None else acc + t
        acc = jnp.where(acc == 0.0, 0.0, acc)
        adm = k0 + col_idx <= qpos
        key_scr[c] = _key_of(jnp.where(adm, acc, NEG))
        lo_c = jnp.where(adm, acc, jnp.inf)
        hi_c = jnp.where(adm, acc, -jnp.inf)
        for j in range(n_tiles):
            vmin = jnp.minimum(vmin, lo_c[:, j * LANES:(j + 1) * LANES])
            vmax = jnp.maximum(vmax, hi_c[:, j * LANES:(j + 1) * LANES])
        return vmin, vmax

    vmin, vmax = lax.fori_loop(0, nkc, score_body, (jnp.full((tq, LANES), jnp.inf, F32),
                                                    jnp.full((tq, LANES), -jnp.inf, F32)))
    vmin = jnp.min(vmin, axis=1, keepdims=True)
    vmax = jnp.max(vmax, axis=1, keepdims=True)

    def count_ge(cands):
        if whole:
            keys = key_scr[...]
            counts = []
            for cand in cands:
                hit = jnp.sum(jnp.where(keys >= cand[None], 1.0, 0.0), axis=0)
                part = hit[:, :LANES]
                for j in range(1, n_tiles):
                    part = part + hit[:, j * LANES:(j + 1) * LANES]
                counts.append(jnp.sum(part, axis=1, keepdims=True))
            return counts

        def body(c, parts):
            keys = key_scr[c]
            out = []
            for cand, part in zip(cands, parts):
                hit = jnp.where(keys >= cand, 1.0, 0.0)
                for j in range(n_tiles):
                    part = part + hit[:, j * LANES:(j + 1) * LANES]
                out.append(part)
            return tuple(out)
        parts = lax.fori_loop(0, nkc, body, tuple(jnp.zeros((tq, LANES), F32) for _ in cands))
        return [jnp.sum(part, axis=1, keepdims=True) for part in parts]

    kf = float(topk)
    kmin = _key_of(vmin)
    zero = jnp.zeros((tq, 1), I32)
    c_all, c_nonneg, c_pos = count_ge([kmin, zero, zero + 1])
    at_zero = jnp.logical_and(c_pos < kf, c_nonneg >= kf)
    above = c_pos >= kf
    lo0 = jnp.where(at_zero, 0, jnp.where(above, 1, kmin))
    clo0 = jnp.where(at_zero, c_nonneg, jnp.where(above, c_pos, c_all))
    hi0 = jnp.where(above, _key_of(vmax) + 1, 0)
    chi0 = jnp.where(above, 0.0, c_nonneg)
    take_all = c_all <= kf
    lo0 = jnp.where(take_all, kmin, lo0)
    clo0 = jnp.where(take_all, c_all, clo0)
    todo0 = jnp.where(jnp.logical_or(jnp.logical_or(take_all, at_zero), clo0 == kf), 0.0, 1.0)

    def search_cond(carry):
        r, _, _, _, _, todo = carry
        return jnp.logical_and(r < MAX_SEARCH_ROUNDS, jnp.max(todo) > 0.5)

    def search_body(carry):
        return lax.fori_loop(0, ROUNDS_PER_EXIT_TEST, lambda _, c: search_round(c), carry)

    def search_round(carry):
        r, lo, hi, clo, chi, todo = carry
        vlo = _value_of(lo)
        vmid = vlo + (_value_of(hi) - vlo) * 0.5
        kmid = lax.shift_right_arithmetic(lo, 1) + lax.shift_right_arithmetic(hi, 1) + (lo & hi & 1)
        mid = jnp.where(r < VALUE_SEARCH_ROUNDS, _key_of(vmid), kmid)
        mid = jnp.minimum(jnp.maximum(mid, lo + 1), hi - 1)
        cnt, = count_ge([mid])
        active = todo > 0.5
        up = jnp.logical_and(active, cnt >= kf)
        down = jnp.logical_and(active, cnt < kf)
        lo = jnp.where(up, mid, lo)
        clo = jnp.where(up, cnt, clo)
        hi = jnp.where(down, mid, hi)
        chi = jnp.where(down, cnt, chi)
        done = jnp.logical_or(clo == kf, hi == lo + 1)
        return r + 1, lo, hi, clo, chi, jnp.where(done, 0.0, todo)

    _, lo, _, clo, _, _ = lax.while_loop(
        search_cond, search_body, (jnp.int32(0), lo0, hi0, clo0, chi0, todo0))
    lo1 = lo + 1
    excess = jnp.where(take_all, 0.0, clo - kf)

    jsel_scr[...] = jnp.full(jsel_scr.shape, n_chunks * kc, I32)
    n_bits = max(1, int(math.ceil(math.log2(n_chunks * kc))))

    @pl.when(jnp.max(excess) > 0.5)
    def _():
        c_above, = count_ge([lo1])
        need = kf - c_above

        def count_eq_upto(t):
            def body(c, part):
                hit = jnp.where(key_scr[c] == lo, jnp.where(c * kc + col_idx <= t, 1.0, 0.0), 0.0)
                for j in range(n_tiles):
                    part = part + hit[:, j * LANES:(j + 1) * LANES]
                return part
            part = lax.fori_loop(0, nkc, body, jnp.zeros((tq, LANES), F32))
            return jnp.sum(part, axis=1, keepdims=True)

        def idx_body(i, jcur):
            step = lax.shift_left(jnp.int32(1), n_bits - 1 - i)
            return jnp.where(count_eq_upto(jcur + step - 1) < need, jcur + step, jcur)

        jfin = lax.fori_loop(0, n_bits, idx_body, jnp.zeros((tq, 1), I32))
        jsel_scr[...] = jnp.broadcast_to(jnp.where(excess > 0.5, jfin, n_chunks * kc), jsel_scr.shape)

    jsel = jsel_scr[:, :1]

    qs, low_q = _stack_group_queries(qa_ref, tq)
    _flash_init(m_scr, acc_scr)

    def att_body(c, carry):
        k0 = pl.multiple_of(c * kc, kc)
        kidx = k0 + col_idx
        thr = jnp.where(kidx <= jsel, lo, lo1)
        bias = jnp.where(key_scr[c] >= thr, jnp.where(kidx <= qpos, 0.0, MASK_VALUE), MASK_VALUE)
        kvc = kv_ref[0, pl.ds(k0, kc), :]
        _flash_step(qs, kvc[:, :LANES], kvc[:, LANES:2 * LANES], kvc[:, 2 * LANES:], (bias, bias),
                    m_scr, acc_scr, tq)
        return carry

    lax.fori_loop(0, nkc, att_body, 0)

    res = _flash_result(acc_scr)
    for g in range(4):
        o = jnp.where(low_q, res[g * tq:(g + 1) * tq], res[(4 + g) * tq:(5 + g) * tq])
        o_ref[0, :, g * LANES:(g + 1) * LANES] = o.astype(o_ref.dtype)


def _dsa(qi, qa, misc, ki_all, kv_all, *, tq, q_pos0, topk, kc=KEY_CHUNK):
    b, tq_all, _ = qa.shape
    lp = ki_all.shape[1]
    assert lp % kc == 0 and kc >= topk and tq_all % tq == 0
    n_chunks = lp // kc
    kern = functools.partial(_dsa_kernel, tq=tq, kc=kc, q_pos0=q_pos0, topk=topk, n_chunks=n_chunks)
    qspec = lambda w: pl.BlockSpec((1, tq, w), lambda bi, qi_: (bi, qi_, 0))
    kspec = lambda w: pl.BlockSpec((1, lp, w), lambda bi, qi_: (bi, 0, 0))
    return pl.pallas_call(
        kern,
        grid=(b, tq_all // tq),
        in_specs=[qspec(512), qspec(512), qspec(LANES), kspec(LANES), kspec(3 * LANES)],
        out_specs=qspec(512),
        out_shape=jax.ShapeDtypeStruct((b, tq_all, 512), BF16),
        scratch_shapes=[pltpu.VMEM((n_chunks, tq, kc), I32),
                        pltpu.VMEM((tq, LANES), I32),
                        pltpu.VMEM((8 * tq, LANES), F32),
                        pltpu.VMEM((8 * tq, LANES), F32)],
        compiler_params=_cparams(2),
        name="dsa",
    )(qi, qa, misc, ki_all, kv_all)


def _nsa_kernel(qb_ref, misc_ref, cmp_ref, ks_ref, kw_ref, mcs_ref, o_ref,
                m_scr, acc_scr, *, tq, kc, q_pos0, n_cmp, n_chunks, topn,
                win_pos0, win_len, win_rows):
    q0 = q_pos0 + pl.program_id(1) * tq
    row = lax.broadcasted_iota(I32, (tq, 1), 0)
    qpos = q0 + row
    ncp = cmp_ref.shape[1]
    nsp = mcs_ref.shape[1]
    qs, low = _stack_group_queries(qb_ref, tq)
    q0s, q1s = qs
    misc = misc_ref[0]

    cmp_all = cmp_ref[0]
    kc128 = cmp_all[:, :LANES]
    vc128 = cmp_all[:, LANES:]
    n_idx = lax.broadcasted_iota(I32, (tq, ncp), 1)
    bias_c = jnp.where(n_idx < n_cmp,
                       jnp.where(n_idx * CMP_STRIDE + (CMP_LEN - 1) <= qpos, 0.0, MASK_VALUE), MASK_VALUE)
    half = 4 * tq
    s = jnp.concatenate([_dot_nt(q0s, kc128), _dot_nt(q1s, kc128)], axis=0)
    pn = _softmax_rows(s + jnp.concatenate([bias_c] * 8, axis=0)).astype(BF16)
    o_cmp = jnp.concatenate([_dot(pn[:half], vc128), _dot(pn[half:], vc128)], axis=0)
    i8 = _dot(pn, mcs_ref[...])
    imp = [i8[(4 * h) * tq:(4 * h + 1) * tq] + i8[(4 * h + 1) * tq:(4 * h + 2) * tq]
           + i8[(4 * h + 2) * tq:(4 * h + 3) * tq] + i8[(4 * h + 3) * tq:(4 * h + 4) * tq]
           for h in range(N_KV_B)]

    blk = lax.broadcasted_iota(I32, (tq, nsp), 1)
    cur = lax.shift_right_logical(qpos, int(math.log2(SLC_BLOCK)))
    future = blk * SLC_BLOCK > qpos
    work0 = jnp.concatenate(
        [jnp.where(future, NEG, jnp.where(blk == 0, BIG, jnp.where(blk == cur, BIG,
                                                                   jnp.where(blk == cur - 1, BIG, imp[h]))))
         for h in range(N_KV_B)], axis=0)
    blk_f = lax.broadcasted_iota(I32, (2 * tq, nsp), 1).astype(F32)

    def pick_body(i, carry):
        work, chosen = carry
        mx = jnp.max(work, axis=1, keepdims=True)
        first = jnp.min(jnp.where(work == mx, blk_f, 1e9), axis=1, keepdims=True)
        hit = blk_f == first
        return jnp.where(hit, -jnp.inf, work), jnp.where(hit, 1.0, chosen)

    _, chosen = lax.fori_loop(0, topn, pick_body, (work0, jnp.zeros((2 * tq, nsp), F32)))
    sel2 = chosen.astype(BF16)

    nkc = jnp.minimum(lax.div(q0 + tq - 1, kc) + 1, n_chunks)
    col_idx = lax.broadcasted_iota(I32, (tq, kc), 1)
    e_row = lax.broadcasted_iota(I32, (nsp, kc), 0)
    e_col = lax.broadcasted_iota(I32, (nsp, kc), 1)
    _flash_init(m_scr, acc_scr)

    def slc_body(c, carry):
        k0 = pl.multiple_of(c * kc, kc)
        expand = jnp.where(lax.shift_right_logical(k0 + e_col, int(math.log2(SLC_BLOCK))) == e_row,
                           1.0, 0.0).astype(BF16)
        kmask = _dot(sel2, expand)
        causal = k0 + col_idx <= qpos
        kvc = ks_ref[0, pl.ds(k0, kc), :]
        biases = [jnp.where(kmask[h * tq:(h + 1) * tq] > 0.5, jnp.where(causal, 0.0, MASK_VALUE), MASK_VALUE)
                  for h in range(N_KV_B)]
        _flash_step(qs, kvc[:, :LANES], kvc[:, LANES:2 * LANES], kvc[:, 2 * LANES:], biases,
                    m_scr, acc_scr, tq)
        return carry

    lax.fori_loop(0, nkc, slc_body, 0)

    ws = jnp.clip(q0 - WINDOW - win_pos0, 0, win_rows - win_len)
    ws = pl.multiple_of(ws, 16)
    kwc = kw_ref[0, pl.ds(ws, win_len), :]
    kw128 = kwc[:, :LANES]
    vw128 = kwc[:, LANES:]
    kpos = win_pos0 + ws + lax.broadcasted_iota(I32, (tq, win_len), 1)
    dist = qpos - kpos
    bias_w = jnp.where(dist >= 0, jnp.where(dist <= WINDOW, 0.0, MASK_VALUE), MASK_VALUE)
    s = jnp.concatenate([_dot_nt(q0s, kw128), _dot_nt(q1s, kw128)], axis=0)
    pn = _softmax_rows(s + jnp.concatenate([bias_w] * 8, axis=0)).astype(BF16)
    o_win = jnp.concatenate([_dot(pn[:half], vw128), _dot(pn[half:], vw128)], axis=0)
    o_slc = _flash_result(acc_scr)

    def gate(r, h, g):
        ln = _GATE_LANE0 + r * N_HEADS_B + h * G_B + g
        return misc[:, ln:ln + 1]

    for g in range(4):
        halves = []
        for h in range(N_KV_B):
            sl = slice((4 * h + g) * tq, (4 * h + g + 1) * tq)
            halves.append(gate(0, h, g) * o_cmp[sl] + gate(1, h, g) * o_slc[sl] + gate(2, h, g) * o_win[sl])
        o_ref[0, :, g * LANES:(g + 1) * LANES] = jnp.where(low, halves[0], halves[1]).astype(o_ref.dtype)


def _slc_map(n_cmp, n_slc, ncp, nsp):
    st = np.arange(ncp) * CMP_STRIDE
    bs = np.arange(nsp) * SLC_BLOCK
    m = (st[:, None] < bs[None, :] + SLC_BLOCK) & (st[:, None] + CMP_LEN > bs[None, :])
    m &= (np.arange(ncp)[:, None] < n_cmp) & (np.arange(nsp)[None, :] < n_slc)
    return jnp.asarray(m.astype(np.float32), BF16)


def _nsa(qb, misc, cmp_kv, ks_all, kw_all, *, tq, q_pos0, n_cmp, n_slc, win_pos0, win_len, kc=KEY_CHUNK):
    b, tq_all, _ = qb.shape
    lp = ks_all.shape[1]
    ncp = cmp_kv.shape[1]
    nsp = -(-n_slc // LANES) * LANES
    assert lp % kc == 0 and ncp % LANES == 0 and ncp >= n_cmp and tq_all % tq == 0
    win_rows = kw_all.shape[1]
    mcs = _slc_map(n_cmp, n_slc, ncp, nsp)
    kern = functools.partial(_nsa_kernel, tq=tq, kc=kc, q_pos0=q_pos0, n_cmp=n_cmp,
                             n_chunks=lp // kc, topn=min(SLC_TOPN, n_slc),
                             win_pos0=win_pos0, win_len=win_len, win_rows=win_rows)
    qspec = lambda w: pl.BlockSpec((1, tq, w), lambda bi, qi_: (bi, qi_, 0))
    full = lambda n, w: pl.BlockSpec((1, n, w), lambda bi, qi_: (bi, 0, 0))
    return pl.pallas_call(
        kern,
        grid=(b, tq_all // tq),
        in_specs=[qspec(512), qspec(LANES), full(ncp, 256), full(lp, 3 * LANES), full(win_rows, 256),
                  _const_spec(mcs.shape)],
        out_specs=qspec(512),
        out_shape=jax.ShapeDtypeStruct((b, tq_all, 512), BF16),
        scratch_shapes=[pltpu.VMEM((8 * tq, LANES), F32),
                        pltpu.VMEM((8 * tq, LANES), F32)],
        compiler_params=_cparams(2),
        name="nsa",
    )(qb, misc, cmp_kv, ks_all, kw_all, mcs)


def _finish_kernel(*refs, tm, packed, t_new, tiles_per_batch):
    if packed:
        (x_ref, oa_ref, ob_ref, mg_ref, mod_ref, e1_ref, e2_ref, tpos_ref, wpa, wpb, wo, ln1g, ln1b,
         wg, wu, cw, cb, wd, ln2g, ln2b, y_o, tail_o, ext) = refs
    else:
        (x_ref, oa_ref, ob_ref, mg_ref, mod_ref, wpa, wpb, wo, ln1g, ln1b, wg, wu,
         cw, cb, wd, ln2g, ln2b, y_o, tail_o, ext) = refs
    d = D_MODEL
    mod = mod_ref[0]
    gate1, shift2, scale2, gate2 = (mod[:, i * d:(i + 1) * d] for i in range(4))
    x = x_ref[...]
    mg = mg_ref[...].astype(F32)
    m = mg[:, :d] * _dot(oa_ref[...], wpa[...]) + mg[:, d:] * _dot(ob_ref[...], wpb[...])
    x1 = _layernorm(DN_ALPHA * x + (1.0 + gate1) * _dot(m.astype(BF16), wo[...]), ln1g[...], ln1b[...])
    h2 = (x1 * (1.0 + scale2) + shift2).astype(BF16)
    ug = _dot(h2, wg[...])
    u = _dot(h2, wu[...])

    if packed:
        ext[0:8, :] = jnp.zeros((8, D_FF), F32)
    else:
        @pl.when(pl.program_id(0) % tiles_per_batch == 0)
        def _():
            ext[0:8, :] = jnp.zeros((8, D_FF), F32)
    ext[8:8 + tm, :] = ug
    prev1 = ext[7:7 + tm, :]
    prev2 = ext[6:6 + tm, :]
    if packed:
        tpos = tpos_ref[:, :1]
        prev1 = jnp.where(tpos >= 1, prev1, e1_ref[...])
        prev2 = jnp.where(tpos >= 2, prev2, e2_ref[...])
    w = cw[...]
    conv = cb[...] + w[2:3, :] * ug
    conv = conv + w[0:1, :] * prev2
    conv = conv + w[1:2, :] * prev1
    tail = ext[tm:tm + 8, :]
    ext[0:8, :] = tail
    if packed:
        tail_o[...] = ug
    else:
        tail_o[0] = tail
    act = (_gelu_tanh(conv) * u).astype(BF16)
    f = _dot(act, wd[...])
    y_o[...] = _layernorm(DN_ALPHA * x1 + (1.0 + gate2) * f, ln2g[...], ln2b[...])


def _finish(x_flat, oa, ob, mg, mod, hist, fw, *, tm, packed, t_new, tiles_per_batch):
    rows = x_flat.shape[0]
    n_tiles = rows // tm
    r_mod = mod.shape[1]
    row_spec = lambda w: pl.BlockSpec((tm, w), lambda i: (i, 0))
    in_specs = [row_spec(D_MODEL), row_spec(512), row_spec(512), row_spec(2 * D_MODEL),
                pl.BlockSpec((1, r_mod, 4 * D_MODEL), lambda i: (i // tiles_per_batch, 0, 0))]
    args = [x_flat, oa, ob, mg, mod]
    if packed:
        in_specs += [row_spec(D_FF), row_spec(D_FF), row_spec(LANES)]
        tpos = jnp.broadcast_to((jnp.arange(rows, dtype=I32) % t_new)[:, None], (rows, LANES))
        args += list(hist) + [tpos]
        tail_shape = jax.ShapeDtypeStruct((rows, D_FF), F32)
        tail_spec = row_spec(D_FF)
    else:
        nb = n_tiles // tiles_per_batch
        tail_shape = jax.ShapeDtypeStruct((nb, 8, D_FF), F32)
        tail_spec = pl.BlockSpec((1, 8, D_FF), lambda i: (i // tiles_per_batch, 0, 0))
    in_specs += [_const_spec(a.shape) for a in fw]
    args += list(fw)
    kern = functools.partial(_finish_kernel, tm=tm, packed=packed, t_new=t_new,
                             tiles_per_batch=tiles_per_batch)
    return pl.pallas_call(
        kern,
        grid=(n_tiles,),
        in_specs=in_specs,
        out_specs=[row_spec(D_MODEL), tail_spec],
        out_shape=[jax.ShapeDtypeStruct((rows, D_MODEL), F32), tail_shape],
        scratch_shapes=[pltpu.VMEM((tm + 8, D_FF), F32)],
        compiler_params=_cparams(1),
        name="finish",
    )(*args)


def kernel(x_prompt, x_sample, cache_a_kv, cache_a_idx, cache_b_cmp_kv, cache_b_slc_kv, state_b_win_kv, state_ffn_conv, page_table, c_prompt, c_sample, w_ada, b_ada, w_in, cmp_pe, cmp_w1, cmp_b1, cmp_w2, cmp_b2, w_pa, w_pb, w_o, ln1_g, ln1_b, w_ffn_gate, w_ffn_up, conv_w, conv_b, w_down, ln2_g, ln2_b):
    d = D_MODEL
    b, s, _ = x_prompt.shape
    bd, tn, _ = x_sample.shape
    n_pages = page_table.shape[1]
    past = n_pages * PAGE_SIZE
    kvw = 2 * N_KV_B * HEAD_DIM
    assert s % PROJ_ROWS == 0 and s % FINISH_ROWS == 0 and s % KEY_CHUNK == 0 and tn <= 8
    assert s >= WINDOW + Q_TILE and state_b_win_kv.shape[1] == WINDOW and tn >= CONV_W - 1

    perm = _proj_perm()
    w_perm = jnp.where(perm[None, :] >= 0, jnp.take(w_in, np.maximum(perm, 0), axis=1), 0.0).astype(BF16)
    head_rows = np.concatenate([np.r_[g * HEAD_DIM:(g + 1) * HEAD_DIM, (4 + g) * HEAD_DIM:(5 + g) * HEAD_DIM]
                                for g in range(4)])
    vec = lambda a: a.reshape(1, -1)
    fw = (w_pa[head_rows].astype(BF16), w_pb[head_rows].astype(BF16), w_o.astype(BF16),
          vec(ln1_g), vec(ln1_b), w_ffn_gate.astype(BF16), w_ffn_up.astype(BF16),
          jnp.concatenate([conv_w, jnp.zeros((8 - CONV_W, D_FF), F32)], axis=0), vec(conv_b),
          w_down.astype(BF16), vec(ln2_g), vec(ln2_b))
    cw = _compress_weights(cmp_pe, cmp_w1, cmp_b1, cmp_w2, cmp_b2)

    rows_c = -(-(b + bd) // 16) * 16
    c_all = jnp.concatenate([c_prompt, c_sample, jnp.zeros((rows_c - b - bd, d), F32)], axis=0)
    ada = _ada(c_all, w_ada.astype(BF16), b_ada)
    ada_p, ada_s = ada[:b], ada[b:b + bd]

    cos_p, sin_p = _rope_tables(jnp.arange(s))
    (qa, qi, qb, kva_bf, ki_bf, kv_cmp_bf, kv_slc_bf, kv_win_bf, misc, mg,
     kva_t, ki_t, kv_cmp_t, kv_slc_t, kv_win_t) = _project(
        x_prompt.reshape(b * s, d), ada_p[:, None, :2 * d], cos_p, sin_p, w_perm,
        PROJ_ROWS, s // PROJ_ROWS, s // PROJ_ROWS, n_batch=b)
    r3 = lambda a: a.reshape(b, s, a.shape[-1])
    oa = _dsa(r3(qi), r3(qa), r3(misc), r3(ki_bf), r3(kva_bf),
              tq=Q_TILE, q_pos0=0, topk=min(DSA_TOPK, s // 4))
    n_ch = -(-s // CMP_STRIDE)
    assert s % CMP_STRIDE == 0 and n_ch % LANES == 0
    cmp_p = _compress(kv_cmp_bf.reshape(b, n_ch, CMP_STRIDE * kvw), cw)
    ob = _nsa(r3(qb), r3(misc), cmp_p, r3(kv_slc_bf), r3(kv_win_bf),
              tq=Q_TILE, q_pos0=0, n_cmp=n_ch - 1, n_slc=-(-s // SLC_BLOCK),
              win_pos0=0, win_len=WINDOW + Q_TILE)
    y_p, tail_p = _finish(x_prompt.reshape(b * s, d), oa.reshape(b * s, 512), ob.reshape(b * s, 512), mg,
                          ada_p[:, None, 2 * d:], None, fw, tm=FINISH_ROWS, packed=False, t_new=0,
                          tiles_per_batch=s // FINISH_ROWS)

    rows_s = bd * tn
    pos_s = past + jnp.arange(tn)
    cos_s, sin_s = _rope_tables(jnp.tile(pos_s, bd))
    rep = lambda a: jnp.repeat(a, tn, axis=0)[None]
    (qa_s, qi_s, qb_s, _, _, _, _, kv_win_s_bf, misc_s, mg_s,
     kva_s, ki_s, kv_cmp_s, kv_slc_s, kv_win_s) = _project(
        x_sample.reshape(rows_s, d), rep(ada_s[:, :2 * d]), cos_s, sin_s, w_perm, rows_s, 1, 1)

    def new16(a):
        a = a.reshape(bd, tn, a.shape[-1])
        return jnp.concatenate([a, jnp.zeros((bd, 16 - tn, a.shape[-1]), F32)], axis=1)

    def q8(a):
        a = a.reshape(bd, tn, a.shape[-1])
        return jnp.concatenate([a, jnp.zeros((bd, 8 - tn, a.shape[-1]), a.dtype)], axis=1)

    pages_t = lambda c: jnp.moveaxis(c, 1, -1).reshape(c.shape[0], -1, PAGE_SIZE)
    ki_all = _gather_pages(pages_t(cache_a_idx), page_table, new16(ki_s), mode="dup")
    kva_all = _gather_pages(pages_t(cache_a_kv), page_table, new16(kva_s), mode="ones")
    cmp_all = _gather_pages(pages_t(cache_b_cmp_kv), page_table, new16(kv_cmp_s), mode="chunks")
    slc_all = _gather_pages(pages_t(cache_b_slc_kv), page_table, new16(kv_slc_s), mode="ones")
    l_tot = past + tn
    lp = kva_all.shape[1]
    oa_s = _dsa(q8(qi_s), q8(qa_s), q8(misc_s), ki_all, kva_all,
                tq=8, q_pos0=past, topk=min(DSA_TOPK, l_tot // 4), kc=DECODE_KEY_CHUNK)
    assert cmp_all.shape[1] % LANES == 0
    cmp_s = _compress(cmp_all, cw)
    win_len = WINDOW + Q_TILE
    win_all = jnp.concatenate(
        [state_b_win_kv.reshape(bd, WINDOW, kvw).astype(BF16), kv_win_s_bf.reshape(bd, tn, kvw),
         jnp.zeros((bd, win_len - WINDOW - tn, kvw), BF16)], axis=1)
    ob_s = _nsa(q8(qb_s), q8(misc_s), cmp_s, slc_all, win_all,
                tq=8, kc=DECODE_KEY_CHUNK, q_pos0=past, n_cmp=-(-l_tot // CMP_STRIDE) - 1, n_slc=-(-l_tot // SLC_BLOCK),
                win_pos0=past - WINDOW, win_len=win_len)
    hist = state_ffn_conv
    zrow = jnp.zeros((bd, 1, D_FF), F32)
    e1 = jnp.concatenate([hist[:, 1:2]] + [zrow] * (tn - 1), axis=1).reshape(rows_s, D_FF)
    e2 = jnp.concatenate([hist[:, 0:1], hist[:, 1:2]] + [zrow] * (tn - 2), axis=1).reshape(rows_s, D_FF)
    y_s, ug_s = _finish(x_sample.reshape(rows_s, d), oa_s[:, :tn].reshape(rows_s, 512),
                        ob_s[:, :tn].reshape(rows_s, 512), mg_s, rep(ada_s[:, 2 * d:]), (e1, e2), fw,
                        tm=rows_s, packed=True, t_new=tn, tiles_per_batch=1)

    kvshape = (2, N_KV_A, HEAD_DIM)
    win_s = jnp.concatenate([state_b_win_kv[:, tn:], kv_win_s.reshape((bd, tn) + kvshape)], axis=1)
    untr = lambda a: a.reshape((b,) + kvshape + (a.shape[-1],)).transpose(0, 4, 1, 2, 3)
    return (y_p.reshape(b, s, d), y_s.reshape(bd, tn, d),
            untr(kva_t), kva_s.reshape((bd, tn) + kvshape),
            ki_t.transpose(0, 2, 1), ki_s.reshape(bd, tn, IDX_DIM),
            untr(kv_cmp_t), kv_cmp_s.reshape((bd, tn) + kvshape),
            untr(kv_slc_t), kv_slc_s.reshape((bd, tn) + kvshape),
            untr(kv_win_t[:, :, s - min(WINDOW, s):]), win_s,
            tail_p[:, 8 - (CONV_W - 1):], ug_s.reshape(bd, tn, D_FF)[:, tn - (CONV_W - 1):])
```

```python
import functools
import math

import numpy as np
import jax
import jax.numpy as jnp
from jax import lax
from jax.experimental import pallas as pl
from jax.experimental.pallas import tpu as pltpu

D_MODEL = 1024
PAGE_SIZE = 128
HEAD_DIM = 64
N_HEADS_A = 8
N_KV_A = 2
N_IDX_HEADS = 8
IDX_DIM = 64
DSA_TOPK = 256
N_HEADS_B = 8
N_KV_B = 2
G_B = N_HEADS_B // N_KV_B
CMP_LEN = 32
CMP_STRIDE = 16
CMP_HID = 2 * HEAD_DIM
SLC_BLOCK = 64
SLC_TOPN = 16
WINDOW = 512
D_FF = 2816
CONV_W = 3
ROPE_THETA = 10000.0
LN_EPS = 1e-5
DEPTH = 1
DN_ALPHA = (2 * DEPTH) ** 0.25
NEG = -1e30
BIG = 1e30
MASK_VALUE = -2e30

LANES = 128
VMEM_LIMIT_BYTES = 56 * 1024 * 1024
Q_TILE = 256
KEY_CHUNK = 512
DECODE_KEY_CHUNK = 2048
PAGES_PER_STEP = 16
PROJ_ROWS = 512
FINISH_ROWS = 256

F32 = jnp.float32
BF16 = jnp.bfloat16
I32 = jnp.int32
INT_MIN = -2 ** 31

_O_QA = 0
_O_KVA = _O_QA + N_HEADS_A * HEAD_DIM
_O_QI = _O_KVA + 2 * N_KV_A * HEAD_DIM
_O_KI = _O_QI + N_IDX_HEADS * IDX_DIM
_O_WI = _O_KI + IDX_DIM
_O_QB = _O_WI + N_IDX_HEADS
_O_KVB = _O_QB + N_HEADS_B * HEAD_DIM
_O_GB = _O_KVB + 6 * N_KV_B * HEAD_DIM
_O_MGA = _O_GB + 3 * N_HEADS_B
_O_MGB = _O_MGA + D_MODEL
_N_IN = _O_MGB + D_MODEL
_GATE_LANE0 = N_IDX_HEADS


def _cparams(n_axes):
    return pltpu.CompilerParams(dimension_semantics=("arbitrary",) * n_axes,
                                vmem_limit_bytes=VMEM_LIMIT_BYTES)


def _const_spec(shape):
    nd = len(shape)
    return pl.BlockSpec(shape, lambda *a: (0,) * nd, pipeline_mode=pl.Buffered(1))


def _dot(a, b):
    return jnp.dot(a, b, preferred_element_type=F32)


def _dot_nt(a, b):
    return lax.dot_general(a, b, (((1,), (1,)), ((), ())), preferred_element_type=F32)


def _sigmoid(x):
    return 1.0 / (1.0 + jnp.exp(-x))


def _gelu_tanh(x):
    return 0.5 * x * (1.0 + jnp.tanh(math.sqrt(2.0 / math.pi) * (x + 0.044715 * (x * x * x))))


def _layernorm(x, g, b):
    mu = jnp.mean(x, axis=-1, keepdims=True)
    xc = x - mu
    var = jnp.mean(xc * xc, axis=-1, keepdims=True)
    return xc * lax.rsqrt(var + LN_EPS) * g + b


def _ada_kernel(c_ref, w_ref, b_ref, o_ref):
    c = c_ref[...]
    s = (c * _sigmoid(c)).astype(BF16)
    o_ref[...] = _dot(s, w_ref[...]) + b_ref[...]


def _ada(c_all, w_ada_bf, b_ada):
    rows = c_all.shape[0]
    n_out = w_ada_bf.shape[1]
    tn = D_MODEL
    return pl.pallas_call(
        _ada_kernel,
        grid=(n_out // tn,),
        in_specs=[pl.BlockSpec((rows, D_MODEL), lambda j: (0, 0)),
                  pl.BlockSpec((D_MODEL, tn), lambda j: (0, j)),
                  pl.BlockSpec((1, tn), lambda j: (0, j))],
        out_specs=pl.BlockSpec((rows, tn), lambda j: (0, j)),
        out_shape=jax.ShapeDtypeStruct((rows, n_out), F32),
        compiler_params=_cparams(1),
        name="ada",
    )(c_all, w_ada_bf, b_ada.reshape(1, n_out))


_P_QA, _P_QI, _P_QB = 0, 512, 1024
_P_K = 1536
_P_V = 2176
_P_MISC = 2688
_P_MG = 2816
_P_END = _P_MG + 2 * D_MODEL


def _proj_perm():
    def paired_heads(base):
        cols = []
        for g in range(4):
            cols += list(range(base + g * HEAD_DIM, base + (g + 1) * HEAD_DIM))
            cols += list(range(base + (4 + g) * HEAD_DIM, base + (5 + g) * HEAD_DIM))
        return cols
    kv = 2 * N_KV_B * HEAD_DIM
    half = N_KV_B * HEAD_DIM
    cols = []
    cols += paired_heads(_O_QA)
    cols += list(range(_O_QI, _O_QI + N_IDX_HEADS * IDX_DIM))
    cols += paired_heads(_O_QB)
    cols += list(range(_O_KVA, _O_KVA + N_KV_A * HEAD_DIM))
    cols += list(range(_O_KI, _O_KI + IDX_DIM)) * 2
    for r in range(3):
        cols += list(range(_O_KVB + r * kv, _O_KVB + r * kv + half))
    cols += list(range(_O_KVA + N_KV_A * HEAD_DIM, _O_KVA + 2 * N_KV_A * HEAD_DIM))
    for r in range(3):
        cols += list(range(_O_KVB + r * kv + half, _O_KVB + (r + 1) * kv))
    misc = list(range(_O_WI, _O_WI + N_IDX_HEADS)) + list(range(_O_GB, _O_GB + 3 * N_HEADS_B))
    cols += misc + [-1] * (LANES - len(misc))
    cols += list(range(_O_MGA, _O_MGA + 2 * D_MODEL))
    assert len(cols) == _P_END
    return np.asarray(cols, np.int32)


def _with_ones(v, low):
    return jnp.where(low, v, 1.0), jnp.where(low, 1.0, v)


def _proj_kernel(x_ref, mod_ref, cos_ref, sin_ref, w_ref,
                 qa_o, qi_o, qb_o, kvab_o, kib_o, cmpb_o, slcb_o, winb_o, misc_o, mg_o,
                 kva_o, ki_o, cmp_o, slc_o, win_o, *, transposed):
    mod = mod_ref[0]
    shift1 = mod[:, :D_MODEL]
    scale1 = mod[:, D_MODEL:]
    h = (x_ref[...] * (1.0 + scale1) + shift1).astype(BF16)
    tm = h.shape[0]
    cos = cos_ref[...]
    sin = sin_ref[...]
    lane = lax.broadcasted_iota(I32, (tm, LANES), 1)
    first_half = (lane & (HEAD_DIM - 1)) < (HEAD_DIM // 2)
    low = lane < HEAD_DIM

    def rope(y):
        partner = jnp.where(first_half, pltpu.roll(y, LANES - HEAD_DIM // 2, 1),
                            pltpu.roll(y, HEAD_DIM // 2, 1))
        return y * cos + partner * sin

    def mm(a, b):
        return _dot(h, w_ref[:, a:b])

    q_scale = HEAD_DIM ** -0.5 * math.log2(math.e)
    for off, out, scale in ((_P_QA, qa_o, q_scale), (_P_QI, qi_o, 1.0), (_P_QB, qb_o, q_scale)):
        y = mm(off, off + 4 * LANES)
        for j in range(4):
            out[:, j * LANES:(j + 1) * LANES] = (rope(y[:, j * LANES:(j + 1) * LANES]) * scale).astype(BF16)

    yk = mm(_P_K, _P_K + 5 * LANES)
    ka, ki2, kc, ks, kw = [rope(yk[:, j * LANES:(j + 1) * LANES]) for j in range(5)]
    yv = mm(_P_V, _P_V + 4 * LANES)
    va, vc, vs, vw = [yv[:, j * LANES:(j + 1) * LANES] for j in range(4)]

    for out, k, v in ((kvab_o, ka, va), (slcb_o, ks, vs)):
        v0, v1 = _with_ones(v, low)
        out[:, :LANES] = k.astype(BF16)
        out[:, LANES:2 * LANES] = v0.astype(BF16)
        out[:, 2 * LANES:] = v1.astype(BF16)
    kib_o[...] = ki2.astype(BF16)
    for out, k, v in ((cmpb_o, kc, vc), (winb_o, kw, vw)):
        out[:, :LANES] = k.astype(BF16)
        out[:, LANES:] = v.astype(BF16)

    for out, k, v in ((kva_o, ka, va), (cmp_o, kc, vc), (slc_o, ks, vs), (win_o, kw, vw)):
        if transposed:
            out[0, :LANES, :] = k.T
            out[0, LANES:, :] = v.T
        else:
            out[:, :LANES] = k
            out[:, LANES:] = v
    if transposed:
        ki_o[0] = ki2.T[:IDX_DIM, :]
    else:
        ki_o[...] = ki2[:, :IDX_DIM]

    ym = mm(_P_MISC, _P_MISC + LANES)
    wi_scale = (N_IDX_HEADS * IDX_DIM) ** -0.5
    misc_o[...] = jnp.where(lane < N_IDX_HEADS, ym * wi_scale, _sigmoid(ym))
    for j in range(2):
        a = _P_MG + j * D_MODEL
        mg_o[:, j * D_MODEL:(j + 1) * D_MODEL] = _sigmoid(mm(a, a + D_MODEL)).astype(BF16)


def _project(x_flat, mod, cos_t, sin_t, w_perm, tm, tiles_per_group, tiles_per_table, n_batch=None):
    rows = x_flat.shape[0]
    n_tiles = rows // tm
    r_mod = mod.shape[1]
    transposed = n_batch is not None

    def row_spec(width):
        return pl.BlockSpec((tm, width), lambda i: (i, 0))

    bf_defs = [(512, BF16), (512, BF16), (512, BF16),
               (384, BF16), (LANES, BF16),
               (256, BF16), (384, BF16), (256, BF16),
               (LANES, F32),
               (2 * D_MODEL, BF16)]
    f32_widths = [256, IDX_DIM, 256, 256, 256]
    out_specs = [row_spec(w) for w, _ in bf_defs]
    out_shape = [jax.ShapeDtypeStruct((rows, w), dt) for w, dt in bf_defs]
    if transposed:
        s = rows // n_batch
        tpb = s // tm
        out_specs += [pl.BlockSpec((1, w, tm), lambda i: (i // tpb, 0, i % tpb)) for w in f32_widths]
        out_shape += [jax.ShapeDtypeStruct((n_batch, w, s), F32) for w in f32_widths]
    else:
        out_specs += [row_spec(w) for w in f32_widths]
        out_shape += [jax.ShapeDtypeStruct((rows, w), F32) for w in f32_widths]
    return pl.pallas_call(
        functools.partial(_proj_kernel, transposed=transposed),
        grid=(n_tiles,),
        in_specs=[row_spec(D_MODEL),
                  pl.BlockSpec((1, r_mod, 2 * D_MODEL), lambda i: (i // tiles_per_group, 0, 0)),
                  pl.BlockSpec((tm, LANES), lambda i: (i % tiles_per_table, 0)),
                  pl.BlockSpec((tm, LANES), lambda i: (i % tiles_per_table, 0)),
                  _const_spec((D_MODEL, _P_END))],
        out_specs=out_specs,
        out_shape=out_shape,
        compiler_params=_cparams(1),
        name="proj",
    )(x_flat, mod, cos_t, sin_t, w_perm)


def _rope_tables(pos):
    half = HEAD_DIM // 2
    inv = ROPE_THETA ** (-jnp.arange(half, dtype=F32) / half)
    ang = pos.astype(F32)[:, None] * inv[None, :]
    cos = jnp.tile(jnp.cos(ang), (1, LANES // half))
    sign = np.where((np.arange(LANES) % HEAD_DIM) < half, -1.0, 1.0).astype(np.float32)
    sin = jnp.tile(jnp.sin(ang), (1, LANES // half)) * sign[None, :]
    return cos, sin


def _gather_kernel(pt_ref, *refs, n_groups, mode):
    pages = refs[:PAGES_PER_STEP]
    new_ref = refs[PAGES_PER_STEP]
    out_ref = refs[PAGES_PER_STEP + 1]
    g = pl.program_id(1)

    if mode == "chunks":
        xs = refs[PAGES_PER_STEP + 2]
        n_rows = PAGES_PER_STEP * PAGE_SIZE // CMP_STRIDE
        w = 2 * LANES

        @pl.when(g < n_groups)
        def _():
            for i in range(PAGES_PER_STEP):
                x = pages[i][0].T
                xs[0, i * PAGE_SIZE:(i + 1) * PAGE_SIZE, :] = x[:, :LANES]
                xs[1, i * PAGE_SIZE:(i + 1) * PAGE_SIZE, :] = x[:, LANES:]
            for c in range(CMP_STRIDE):
                for hf in range(2):
                    a = c * w + hf * LANES
                    out_ref[0, :, a:a + LANES] = xs[hf, pl.ds(c, n_rows, stride=CMP_STRIDE), :].astype(BF16)

        @pl.when(g == n_groups)
        def _():
            new = new_ref[0]
            first = jnp.concatenate([new[c:c + 1, :] for c in range(CMP_STRIDE)], axis=1)
            out_ref[0] = jnp.zeros(out_ref.shape[1:], BF16)
            out_ref[0, :16, :] = jnp.concatenate([first, jnp.zeros((15, CMP_STRIDE * w), F32)],
                                                 axis=0).astype(BF16)
        return

    def put(r0, rows):
        n = rows.shape[0]
        if mode == "ones":
            low = lax.broadcasted_iota(I32, (n, LANES), 1) < HEAD_DIM
            v0, v1 = _with_ones(rows[:, LANES:], low)
            out_ref[0, r0:r0 + n, :LANES] = rows[:, :LANES].astype(BF16)
            out_ref[0, r0:r0 + n, LANES:2 * LANES] = v0.astype(BF16)
            out_ref[0, r0:r0 + n, 2 * LANES:] = v1.astype(BF16)
        elif mode == "dup":
            out_ref[0, r0:r0 + n, :] = jnp.concatenate([rows, rows], axis=1).astype(BF16)
        else:
            out_ref[0, r0:r0 + n, :] = rows.astype(BF16)

    @pl.when(g < n_groups)
    def _():
        for i in range(PAGES_PER_STEP):
            put(i * PAGE_SIZE, pages[i][0].T)

    @pl.when(g == n_groups)
    def _():
        out_ref[0] = jnp.zeros(out_ref.shape[1:], BF16)
        put(0, new_ref[0])


def _gather_pages(cache_t, page_table, new_rows, mode="plain"):
    n_phys, w, _ = cache_t.shape
    b, n_pages = page_table.shape
    assert n_pages % PAGES_PER_STEP == 0
    n_groups = n_pages // PAGES_PER_STEP
    rows_step = PAGES_PER_STEP * PAGE_SIZE
    scratch = []
    if mode == "chunks":
        assert w == 2 * LANES
        rows_step, w_out = rows_step // CMP_STRIDE, CMP_STRIDE * w
        scratch = [pltpu.VMEM((2, PAGES_PER_STEP * PAGE_SIZE, LANES), F32)]
    else:
        w_out = {"plain": w, "ones": w + LANES, "dup": 2 * w}[mode]

    def page_spec(i):
        def imap(bi, g, pt):
            gg = jnp.minimum(g, n_groups - 1)
            return (pt[bi, gg * PAGES_PER_STEP + i], 0, 0)
        return pl.BlockSpec((1, w, PAGE_SIZE), imap)

    grid_spec = pltpu.PrefetchScalarGridSpec(
        num_scalar_prefetch=1,
        grid=(b, n_groups + 1),
        in_specs=[page_spec(i) for i in range(PAGES_PER_STEP)]
        + [pl.BlockSpec((1, new_rows.shape[1], w), lambda bi, g, pt: (bi, 0, 0))],
        out_specs=pl.BlockSpec((1, rows_step, w_out), lambda bi, g, pt: (bi, g, 0)),
        scratch_shapes=scratch,
    )
    return pl.pallas_call(
        functools.partial(_gather_kernel, n_groups=n_groups, mode=mode),
        grid_spec=grid_spec,
        out_shape=jax.ShapeDtypeStruct((b, (n_groups + 1) * rows_step, w_out), BF16),
        compiler_params=_cparams(2),
        name="gather_pages",
    )(page_table, *([cache_t] * PAGES_PER_STEP), new_rows)


def _cmp_kernel(x_ref, wa_ref, wb_ref, pea_ref, peb_ref, b1_ref, w2_ref, b2_ref, o_ref):
    x = x_ref[0].astype(BF16)
    r = x.shape[0]
    ha = _dot(x, wa_ref[...])
    hb = _dot(x, wb_ref[...])
    hpe = _dot(pea_ref[...], wa_ref[...]) + _dot(peb_ref[...], wb_ref[...])
    hpe = hpe[0:1, :] + b1_ref[...]
    hb_next = pltpu.roll(hb, r - 1, 0)
    hid = _gelu_tanh(ha + hb_next + hpe).astype(BF16)
    o_ref[0] = (_dot(hid, w2_ref[...]) + b2_ref[...]).astype(o_ref.dtype)


def _compress_weights(cmp_pe, cmp_w1, cmp_b1, cmp_w2, cmp_b2):
    eye = jnp.eye(2, dtype=F32)
    kvw = 2 * N_KV_B * HEAD_DIM

    def big(w1h):
        t = jnp.einsum('jcdf,jJ,hH->cjhdJHf', w1h, eye, eye)
        return t.reshape(CMP_STRIDE * kvw, 2 * N_KV_B * CMP_HID).astype(BF16)

    def pe_row(peh):
        t = jnp.broadcast_to(peh.transpose(1, 0, 2)[:, :, None, :], (CMP_STRIDE, 2, N_KV_B, HEAD_DIM))
        row = t.reshape(1, CMP_STRIDE * kvw)
        return jnp.concatenate([row, jnp.zeros((15, CMP_STRIDE * kvw), F32)], axis=0).astype(BF16)

    wa = big(cmp_w1[:, :CMP_STRIDE])
    wb = big(cmp_w1[:, CMP_STRIDE:])
    pea = pe_row(cmp_pe[:, :CMP_STRIDE])
    peb = pe_row(cmp_pe[:, CMP_STRIDE:])
    b1 = jnp.broadcast_to(cmp_b1[:, None, :], (2, N_KV_B, CMP_HID)).reshape(1, -1)
    w2 = jnp.einsum('jfd,jJ,hH->jhfJHd', cmp_w2, eye, eye).reshape(2 * N_KV_B * CMP_HID, kvw).astype(BF16)
    b2 = jnp.broadcast_to(cmp_b2[:, None, :], (2, N_KV_B, HEAD_DIM)).reshape(1, -1)
    return wa, wb, pea, peb, b1, w2, b2


def _compress(x_chunks, cw):
    b, r, cwid = x_chunks.shape
    wa, wb, pea, peb, b1, w2, b2 = cw
    return pl.pallas_call(
        _cmp_kernel,
        grid=(b,),
        in_specs=[pl.BlockSpec((1, r, cwid), lambda i: (i, 0, 0)),
                  _const_spec(wa.shape), _const_spec(wb.shape), _const_spec(pea.shape),
                  _const_spec(peb.shape), _const_spec(b1.shape), _const_spec(w2.shape),
                  _const_spec(b2.shape)],
        out_specs=pl.BlockSpec((1, r, w2.shape[1]), lambda i: (i, 0, 0)),
        out_shape=jax.ShapeDtypeStruct((b, r, w2.shape[1]), BF16),
        compiler_params=_cparams(1),
        name="compress",
    )(x_chunks, wa, wb, pea, peb, b1, w2, b2)


def _stack_group_queries(q_ref, tq):
    lane = lax.broadcasted_iota(I32, (tq, LANES), 1)
    low = lane < HEAD_DIM
    q = q_ref[0].astype(F32)
    cols = [q[:, g * LANES:(g + 1) * LANES] for g in range(4)]
    q0 = jnp.concatenate([jnp.where(low, c, 0.0) for c in cols], axis=0).astype(BF16)
    q1 = jnp.concatenate([jnp.where(low, 0.0, c) for c in cols], axis=0).astype(BF16)
    return (q0, q1), low


def _flash_init(m_scr, acc_scr):
    m_scr[...] = jnp.full(m_scr.shape, NEG, F32)
    acc_scr[...] = jnp.zeros(acc_scr.shape, F32)


def _flash_step(qs, k128, v0, v1, biases, m_scr, acc_scr, tq):
    half = 4 * tq
    kc = k128.shape[0]
    s = jnp.concatenate([_dot_nt(qs[0], k128), _dot_nt(qs[1], k128)], axis=0)
    s = s + jnp.concatenate([biases[0]] * 4 + [biases[1]] * 4, axis=0)
    m_prev = m_scr[...]
    m_new = jnp.maximum(m_prev, jnp.max(s, axis=1, keepdims=True))
    alpha = jnp.exp2(m_prev - m_new)
    p = jnp.exp2(s - jnp.concatenate([m_new] * (kc // LANES), axis=1)).astype(BF16)
    m_scr[...] = m_new
    pv = jnp.concatenate([_dot(p[:half], v0), _dot(p[half:], v1)], axis=0)
    acc_scr[...] = alpha * acc_scr[...] + pv


def _flash_result(acc_scr):
    acc = acc_scr[...]
    l = pltpu.roll(acc, HEAD_DIM, 1)
    return jnp.where(l > 0.0, acc / l, 0.0)


def _softmax_rows(s):
    m = jnp.maximum(jnp.max(s, axis=1, keepdims=True), NEG)
    p = jnp.exp2(s - m)
    l = jnp.sum(p, axis=1, keepdims=True)
    return jnp.where(l > 0.0, p / l, 0.0)


def _key_of(v):
    bits = lax.bitcast_convert_type(v, I32)
    return jnp.where(bits < 0, bits ^ 0x7FFFFFFF, bits)


def _value_of(key):
    return lax.bitcast_convert_type(jnp.where(key < 0, key ^ 0x7FFFFFFF, key), F32)


def _floor_bf16(v):
    return lax.bitcast_convert_type(lax.bitcast_convert_type(v, I32) & -65536, F32)


VALUE_SEARCH_ROUNDS = 26
MAX_SEARCH_ROUNDS = VALUE_SEARCH_ROUNDS + 34
ROUNDS_PER_EXIT_TEST = 3


def _dsa_kernel(qi_ref, qa_ref, misc_ref, ki_ref, kv_ref, o_ref,
                key_scr, key16_scr, jsel_scr, m_scr, acc_scr, *, tq, kc, q_pos0, topk, n_chunks):
    q0 = q_pos0 + pl.program_id(1) * tq
    row = lax.broadcasted_iota(I32, (tq, 1), 0)
    qpos = q0 + row
    whole = tq <= 16
    nkc = n_chunks if whole else jnp.minimum(lax.div(q0 + tq - 1, kc) + 1, n_chunks)
    lane = lax.broadcasted_iota(I32, (tq, LANES), 1)
    low = lane < IDX_DIM
    col_idx = lax.broadcasted_iota(I32, (tq, kc), 1)
    n_tiles = kc // LANES

    qi = qi_ref[0].astype(F32)
    misc = misc_ref[0]
    q_heads, w_heads = [], []
    for hh in range(N_IDX_HEADS):
        c = qi[:, (hh // 2) * LANES:(hh // 2 + 1) * LANES]
        q_heads.append(jnp.where(low, c, 0.0) if hh % 2 == 0 else jnp.where(low, 0.0, c))
        w_heads.append(misc[:, hh:hh + 1])
    q_all = jnp.concatenate(q_heads, axis=0).astype(BF16)

    def score_body(c, carry):
        vmin, vmax = carry
        k0 = pl.multiple_of(c * kc, kc)
        d = _dot_nt(q_all, ki_ref[0, pl.ds(k0, kc), :])
        acc = None
        for hh in range(N_IDX_HEADS):
            t = w_heads[hh] * jnp.maximum(d[hh * tq:(hh + 1) * tq], 0.0)
            acc = t if acc is None else acc + t
        acc = jnp.where(acc == 0.0, 0.0, acc)
        adm = k0 + col_idx <= qpos
        key_scr[c] = _key_of(jnp.where(adm, acc, NEG))
        if not whole:
            key16_scr[c] = _floor_bf16(jnp.where(adm, acc, 0.0)).astype(BF16)
        lo_c = jnp.where(adm, acc, jnp.inf)
        hi_c = jnp.where(adm, acc, -jnp.inf)
        for j in range(n_tiles):
            vmin = jnp.minimum(vmin, lo_c[:, j * LANES:(j + 1) * LANES])
            vmax = jnp.maximum(vmax, hi_c[:, j * LANES:(j + 1) * LANES])
        return vmin, vmax

    vmin, vmax = lax.fori_loop(0, nkc, score_body, (jnp.full((tq, LANES), jnp.inf, F32),
                                                    jnp.full((tq, LANES), -jnp.inf, F32)))
    vmin = jnp.min(vmin, axis=1, keepdims=True)
    vmax = jnp.max(vmax, axis=1, keepdims=True)

    def count_ge(cands):
        if whole:
            keys = key_scr[...]
            counts = []
            for cand in cands:
                hit = jnp.sum(jnp.where(keys >= cand[None], 1.0, 0.0), axis=0)
                part = hit[:, :LANES]
                for j in range(1, n_tiles):
                    part = part + hit[:, j * LANES:(j + 1) * LANES]
                counts.append(jnp.sum(part, axis=1, keepdims=True))
            return counts

        def body(c, parts):
            keys = key_scr[c]
            out = []
            for cand, part in zip(cands, parts):
                hit = jnp.where(keys >= cand, 1.0, 0.0)
                for j in range(n_tiles):
                    part = part + hit[:, j * LANES:(j + 1) * LANES]
                out.append(part)
            return tuple(out)
        parts = lax.fori_loop(0, nkc, body, tuple(jnp.zeros((tq, LANES), F32) for _ in cands))
        return [jnp.sum(part, axis=1, keepdims=True) for part in parts]

    kf = float(topk)
    kmin = _key_of(vmin)
    zero = jnp.zeros((tq, 1), I32)
    c_all, c_nonneg, c_pos = count_ge([kmin, zero, zero + 1])
    at_zero = jnp.logical_and(c_pos < kf, c_nonneg >= kf)
    above = c_pos >= kf
    lo0 = jnp.where(at_zero, 0, jnp.where(above, 1, kmin))
    clo0 = jnp.where(at_zero, c_nonneg, jnp.where(above, c_pos, c_all))
    hi0 = jnp.where(above, _key_of(vmax) + 1, 0)
    chi0 = jnp.where(above, 0.0, c_nonneg)
    take_all = c_all <= kf
    lo0 = jnp.where(take_all, kmin, lo0)
    clo0 = jnp.where(take_all, c_all, clo0)
    todo0 = jnp.where(jnp.logical_or(jnp.logical_or(take_all, at_zero), clo0 == kf), 0.0, 1.0)

    def search_cond(carry):
        r, _, _, _, _, todo = carry
        return jnp.logical_and(r < MAX_SEARCH_ROUNDS, jnp.max(todo) > 0.5)

    if not whole:
        def count16_ge(cand_v):
            cand = jnp.concatenate([jnp.broadcast_to(cand_v, (tq, LANES)).astype(BF16)] * n_tiles, axis=1)
            one = jnp.ones((tq, kc), BF16)
            nil = jnp.zeros((tq, kc), BF16)

            def body(c, part):
                hit = jnp.where(key16_scr[c] >= cand, one, nil)
                for j in range(n_tiles):
                    part = part + hit[:, j * LANES:(j + 1) * LANES]
                return part
            part = lax.fori_loop(0, nkc, body, jnp.zeros((tq, LANES), BF16))
            return jnp.sum(part.astype(F32), axis=1, keepdims=True)

        def coarse_round(carry):
            r, lo_v, hi_v, clo, chi, todo = carry
            mid = _floor_bf16((lo_v + hi_v) * 0.5)
            cnt = count16_ge(mid)
            active = jnp.logical_and(todo > 0.5, mid > lo_v)
            up = jnp.logical_and(active, cnt >= kf)
            down = jnp.logical_and(active, cnt < kf)
            lo_v = jnp.where(up, mid, lo_v)
            clo = jnp.where(up, cnt, clo)
            hi_v = jnp.where(down, mid, hi_v)
            chi = jnp.where(down, cnt, chi)
            return r + 1, lo_v, hi_v, clo, chi, jnp.where(jnp.logical_and(active, clo != kf), todo, 0.0)

        coarse0 = jnp.where(above, todo0, 0.0)
        hi_v0 = _value_of((_key_of(jnp.maximum(vmax, 0.0)) & -65536) + 65536)
        _, lo_v, hi_v, clo_c, chi_c, _ = lax.while_loop(
            search_cond,
            lambda carry: lax.fori_loop(0, ROUNDS_PER_EXIT_TEST, lambda _, c: coarse_round(c), carry),
            (jnp.int32(0), jnp.zeros((tq, 1), F32), hi_v0, clo0, chi0, coarse0))
        use = coarse0 > 0.5
        lo0 = jnp.where(use, jnp.maximum(_key_of(lo_v), 1), lo0)
        hi0 = jnp.where(use, _key_of(hi_v), hi0)
        clo0 = jnp.where(use, clo_c, clo0)
        chi0 = jnp.where(use, chi_c, chi0)
        todo0 = jnp.where(jnp.logical_or(clo0 == kf, hi0 == lo0 + 1), 0.0, todo0)

    def search_body(carry):
        return lax.fori_loop(0, ROUNDS_PER_EXIT_TEST, lambda _, c: search_round(c), carry)

    def search_round(carry):
        r, lo, hi, clo, chi, todo = carry
        vlo = _value_of(lo)
        vmid = vlo + (_value_of(hi) - vlo) * 0.5
        kmid = lax.shift_right_arithmetic(lo, 1) + lax.shift_right_arithmetic(hi, 1) + (lo & hi & 1)
        mid = jnp.where(r < VALUE_SEARCH_ROUNDS, _key_of(vmid), kmid)
        mid = jnp.minimum(jnp.maximum(mid, lo + 1), hi - 1)
        cnt, = count_ge([mid])
        active = todo > 0.5
        up = jnp.logical_and(active, cnt >= kf)
        down = jnp.logical_and(active, cnt < kf)
        lo = jnp.where(up, mid, lo)
        clo = jnp.where(up, cnt, clo)
        hi = jnp.where(down, mid, hi)
        chi = jnp.where(down, cnt, chi)
        done = jnp.logical_or(clo == kf, hi == lo + 1)
        return r + 1, lo, hi, clo, chi, jnp.where(done, 0.0, todo)

    _, lo, _, clo, _, _ = lax.while_loop(
        search_cond, search_body, (jnp.int32(0), lo0, hi0, clo0, chi0, todo0))
    lo1 = lo + 1
    excess = jnp.where(take_all, 0.0, clo - kf)

    jsel_scr[...] = jnp.full(jsel_scr.shape, n_chunks * kc, I32)
    n_bits = max(1, int(math.ceil(math.log2(n_chunks * kc))))

    @pl.when(jnp.max(excess) > 0.5)
    def _():
        c_above, = count_ge([lo1])
        need = kf - c_above

        def count_eq_upto(t):
            def body(c, part):
                hit = jnp.where(key_scr[c] == lo, jnp.where(c * kc + col_idx <= t, 1.0, 0.0), 0.0)
                for j in range(n_tiles):
                    part = part + hit[:, j * LANES:(j + 1) * LANES]
                return part
            part = lax.fori_loop(0, nkc, body, jnp.zeros((tq, LANES), F32))
            return jnp.sum(part, axis=1, keepdims=True)

        def idx_body(i, jcur):
            step = lax.shift_left(jnp.int32(1), n_bits - 1 - i)
            return jnp.where(count_eq_upto(jcur + step - 1) < need, jcur + step, jcur)

        jfin = lax.fori_loop(0, n_bits, idx_body, jnp.zeros((tq, 1), I32))
        jsel_scr[...] = jnp.broadcast_to(jnp.where(excess > 0.5, jfin, n_chunks * kc), jsel_scr.shape)

    jsel = jsel_scr[:, :1]

    qs, low_q = _stack_group_queries(qa_ref, tq)
    _flash_init(m_scr, acc_scr)

    def att_body(c, carry):
        k0 = pl.multiple_of(c * kc, kc)
        kidx = k0 + col_idx
        thr = jnp.where(kidx <= jsel, lo, lo1)
        bias = jnp.where(key_scr[c] >= thr, jnp.where(kidx <= qpos, 0.0, MASK_VALUE), MASK_VALUE)
        kvc = kv_ref[0, pl.ds(k0, kc), :]
        _flash_step(qs, kvc[:, :LANES], kvc[:, LANES:2 * LANES], kvc[:, 2 * LANES:], (bias, bias),
                    m_scr, acc_scr, tq)
        return carry

    lax.fori_loop(0, nkc, att_body, 0)

    res = _flash_result(acc_scr)
    for g in range(4):
        o = jnp.where(low_q, res[g * tq:(g + 1) * tq], res[(4 + g) * tq:(5 + g) * tq])
        o_ref[0, :, g * LANES:(g + 1) * LANES] = o.astype(o_ref.dtype)


def _dsa(qi, qa, misc, ki_all, kv_all, *, tq, q_pos0, topk, kc=KEY_CHUNK):
    b, tq_all, _ = qa.shape
    lp = ki_all.shape[1]
    assert lp % kc == 0 and kc >= topk and tq_all % tq == 0
    n_chunks = lp // kc
    kern = functools.partial(_dsa_kernel, tq=tq, kc=kc, q_pos0=q_pos0, topk=topk, n_chunks=n_chunks)
    qspec = lambda w: pl.BlockSpec((1, tq, w), lambda bi, qi_: (bi, qi_, 0))
    kspec = lambda w: pl.BlockSpec((1, lp, w), lambda bi, qi_: (bi, 0, 0))
    return pl.pallas_call(
        kern,
        grid=(b, tq_all // tq),
        in_specs=[qspec(512), qspec(512), qspec(LANES), kspec(LANES), kspec(3 * LANES)],
        out_specs=qspec(512),
        out_shape=jax.ShapeDtypeStruct((b, tq_all, 512), BF16),
        scratch_shapes=[pltpu.VMEM((n_chunks, tq, kc), I32),
                        pltpu.VMEM((n_chunks, max(tq, 16), kc), BF16),
                        pltpu.VMEM((tq, LANES), I32),
                        pltpu.VMEM((8 * tq, LANES), F32),
                        pltpu.VMEM((8 * tq, LANES), F32)],
        compiler_params=_cparams(2),
        name="dsa",
    )(qi, qa, misc, ki_all, kv_all)


def _nsa_kernel(qb_ref, misc_ref, cmp_ref, ks_ref, kw_ref, mcs_ref, o_ref,
                m_scr, acc_scr, *, tq, kc, q_pos0, n_cmp, n_chunks, topn,
                win_pos0, win_len, win_rows):
    q0 = q_pos0 + pl.program_id(1) * tq
    row = lax.broadcasted_iota(I32, (tq, 1), 0)
    qpos = q0 + row
    ncp = cmp_ref.shape[1]
    nsp = mcs_ref.shape[1]
    qs, low = _stack_group_queries(qb_ref, tq)
    q0s, q1s = qs
    misc = misc_ref[0]

    cmp_all = cmp_ref[0]
    kc128 = cmp_all[:, :LANES]
    vc128 = cmp_all[:, LANES:]
    n_idx = lax.broadcasted_iota(I32, (tq, ncp), 1)
    bias_c = jnp.where(n_idx < n_cmp,
                       jnp.where(n_idx * CMP_STRIDE + (CMP_LEN - 1) <= qpos, 0.0, MASK_VALUE), MASK_VALUE)
    half = 4 * tq
    s = jnp.concatenate([_dot_nt(q0s, kc128), _dot_nt(q1s, kc128)], axis=0)
    pn = _softmax_rows(s + jnp.concatenate([bias_c] * 8, axis=0)).astype(BF16)
    o_cmp = jnp.concatenate([_dot(pn[:half], vc128), _dot(pn[half:], vc128)], axis=0)
    i8 = _dot(pn, mcs_ref[...])
    imp = [i8[(4 * h) * tq:(4 * h + 1) * tq] + i8[(4 * h + 1) * tq:(4 * h + 2) * tq]
           + i8[(4 * h + 2) * tq:(4 * h + 3) * tq] + i8[(4 * h + 3) * tq:(4 * h + 4) * tq]
           for h in range(N_KV_B)]

    blk = lax.broadcasted_iota(I32, (tq, nsp), 1)
    cur = lax.shift_right_logical(qpos, int(math.log2(SLC_BLOCK)))
    future = blk * SLC_BLOCK > qpos
    work0 = jnp.concatenate(
        [jnp.where(future, NEG, jnp.where(blk == 0, BIG, jnp.where(blk == cur, BIG,
                                                                   jnp.where(blk == cur - 1, BIG, imp[h]))))
         for h in range(N_KV_B)], axis=0)
    blk_f = lax.broadcasted_iota(I32, (2 * tq, nsp), 1).astype(F32)

    def pick_body(i, carry):
        work, chosen = carry
        mx = jnp.max(work, axis=1, keepdims=True)
        first = jnp.min(jnp.where(work == mx, blk_f, 1e9), axis=1, keepdims=True)
        hit = blk_f == first
        return jnp.where(hit, -jnp.inf, work), jnp.where(hit, 1.0, chosen)

    _, chosen = lax.fori_loop(0, topn, pick_body, (work0, jnp.zeros((2 * tq, nsp), F32)))
    sel2 = chosen.astype(BF16)

    nkc = jnp.minimum(lax.div(q0 + tq - 1, kc) + 1, n_chunks)
    col_idx = lax.broadcasted_iota(I32, (tq, kc), 1)
    e_row = lax.broadcasted_iota(I32, (nsp, kc), 0)
    e_col = lax.broadcasted_iota(I32, (nsp, kc), 1)
    _flash_init(m_scr, acc_scr)

    def slc_body(c, carry):
        k0 = pl.multiple_of(c * kc, kc)
        expand = jnp.where(lax.shift_right_logical(k0 + e_col, int(math.log2(SLC_BLOCK))) == e_row,
                           1.0, 0.0).astype(BF16)
        kmask = _dot(sel2, expand)
        causal = k0 + col_idx <= qpos
        kvc = ks_ref[0, pl.ds(k0, kc), :]
        biases = [jnp.where(kmask[h * tq:(h + 1) * tq] > 0.5, jnp.where(causal, 0.0, MASK_VALUE), MASK_VALUE)
                  for h in range(N_KV_B)]
        _flash_step(qs, kvc[:, :LANES], kvc[:, LANES:2 * LANES], kvc[:, 2 * LANES:], biases,
                    m_scr, acc_scr, tq)
        return carry

    lax.fori_loop(0, nkc, slc_body, 0)

    ws = jnp.clip(q0 - WINDOW - win_pos0, 0, win_rows - win_len)
    ws = pl.multiple_of(ws, 16)
    kwc = kw_ref[0, pl.ds(ws, win_len), :]
    kw128 = kwc[:, :LANES]
    vw128 = kwc[:, LANES:]
    kpos = win_pos0 + ws + lax.broadcasted_iota(I32, (tq, win_len), 1)
    dist = qpos - kpos
    bias_w = jnp.where(dist >= 0, jnp.where(dist <= WINDOW, 0.0, MASK_VALUE), MASK_VALUE)
    s = jnp.concatenate([_dot_nt(q0s, kw128), _dot_nt(q1s, kw128)], axis=0)
    pn = _softmax_rows(s + jnp.concatenate([bias_w] * 8, axis=0)).astype(BF16)
    o_win = jnp.concatenate([_dot(pn[:half], vw128), _dot(pn[half:], vw128)], axis=0)
    o_slc = _flash_result(acc_scr)

    def gate(r, h, g):
        ln = _GATE_LANE0 + r * N_HEADS_B + h * G_B + g
        return misc[:, ln:ln + 1]

    for g in range(4):
        halves = []
        for h in range(N_KV_B):
            sl = slice((4 * h + g) * tq, (4 * h + g + 1) * tq)
            halves.append(gate(0, h, g) * o_cmp[sl] + gate(1, h, g) * o_slc[sl] + gate(2, h, g) * o_win[sl])
        o_ref[0, :, g * LANES:(g + 1) * LANES] = jnp.where(low, halves[0], halves[1]).astype(o_ref.dtype)


def _slc_map(n_cmp, n_slc, ncp, nsp):
    st = np.arange(ncp) * CMP_STRIDE
    bs = np.arange(nsp) * SLC_BLOCK
    m = (st[:, None] < bs[None, :] + SLC_BLOCK) & (st[:, None] + CMP_LEN > bs[None, :])
    m &= (np.arange(ncp)[:, None] < n_cmp) & (np.arange(nsp)[None, :] < n_slc)
    return jnp.asarray(m.astype(np.float32), BF16)


def _nsa(qb, misc, cmp_kv, ks_all, kw_all, *, tq, q_pos0, n_cmp, n_slc, win_pos0, win_len, kc=KEY_CHUNK):
    b, tq_all, _ = qb.shape
    lp = ks_all.shape[1]
    ncp = cmp_kv.shape[1]
    nsp = -(-n_slc // LANES) * LANES
    assert lp % kc == 0 and ncp % LANES == 0 and ncp >= n_cmp and tq_all % tq == 0
    win_rows = kw_all.shape[1]
    mcs = _slc_map(n_cmp, n_slc, ncp, nsp)
    kern = functools.partial(_nsa_kernel, tq=tq, kc=kc, q_pos0=q_pos0, n_cmp=n_cmp,
                             n_chunks=lp // kc, topn=min(SLC_TOPN, n_slc),
                             win_pos0=win_pos0, win_len=win_len, win_rows=win_rows)
    qspec = lambda w: pl.BlockSpec((1, tq, w), lambda bi, qi_: (bi, qi_, 0))
    full = lambda n, w: pl.BlockSpec((1, n, w), lambda bi, qi_: (bi, 0, 0))
    return pl.pallas_call(
        kern,
        grid=(b, tq_all // tq),
        in_specs=[qspec(512), qspec(LANES), full(ncp, 256), full(lp, 3 * LANES), full(win_rows, 256),
                  _const_spec(mcs.shape)],
        out_specs=qspec(512),
        out_shape=jax.ShapeDtypeStruct((b, tq_all, 512), BF16),
        scratch_shapes=[pltpu.VMEM((8 * tq, LANES), F32),
                        pltpu.VMEM((8 * tq, LANES), F32)],
        compiler_params=_cparams(2),
        name="nsa",
    )(qb, misc, cmp_kv, ks_all, kw_all, mcs)


def _finish_kernel(*refs, tm, packed, t_new, tiles_per_batch):
    if packed:
        (x_ref, oa_ref, ob_ref, mg_ref, mod_ref, e1_ref, e2_ref, tpos_ref, wpa, wpb, wo, ln1g, ln1b,
         wg, wu, cw, cb, wd, ln2g, ln2b, y_o, tail_o, ext) = refs
    else:
        (x_ref, oa_ref, ob_ref, mg_ref, mod_ref, wpa, wpb, wo, ln1g, ln1b, wg, wu,
         cw, cb, wd, ln2g, ln2b, y_o, tail_o, ext) = refs
    d = D_MODEL
    mod = mod_ref[0]
    gate1, shift2, scale2, gate2 = (mod[:, i * d:(i + 1) * d] for i in range(4))
    x = x_ref[...]
    mg = mg_ref[...].astype(F32)
    m = mg[:, :d] * _dot(oa_ref[...], wpa[...]) + mg[:, d:] * _dot(ob_ref[...], wpb[...])
    x1 = _layernorm(DN_ALPHA * x + (1.0 + gate1) * _dot(m.astype(BF16), wo[...]), ln1g[...], ln1b[...])
    h2 = (x1 * (1.0 + scale2) + shift2).astype(BF16)
    ug = _dot(h2, wg[...])
    u = _dot(h2, wu[...])

    if packed:
        ext[0:8, :] = jnp.zeros((8, D_FF), F32)
    else:
        @pl.when(pl.program_id(0) % tiles_per_batch == 0)
        def _():
            ext[0:8, :] = jnp.zeros((8, D_FF), F32)
    ext[8:8 + tm, :] = ug
    prev1 = ext[7:7 + tm, :]
    prev2 = ext[6:6 + tm, :]
    if packed:
        tpos = tpos_ref[:, :1]
        prev1 = jnp.where(tpos >= 1, prev1, e1_ref[...])
        prev2 = jnp.where(tpos >= 2, prev2, e2_ref[...])
    w = cw[...]
    conv = cb[...] + w[2:3, :] * ug
    conv = conv + w[0:1, :] * prev2
    conv = conv + w[1:2, :] * prev1
    tail = ext[tm:tm + 8, :]
    ext[0:8, :] = tail
    if packed:
        tail_o[...] = ug
    else:
        tail_o[0] = tail
    act = (_gelu_tanh(conv) * u).astype(BF16)
    f = _dot(act, wd[...])
    y_o[...] = _layernorm(DN_ALPHA * x1 + (1.0 + gate2) * f, ln2g[...], ln2b[...])


def _finish(x_flat, oa, ob, mg, mod, hist, fw, *, tm, packed, t_new, tiles_per_batch):
    rows = x_flat.shape[0]
    n_tiles = rows // tm
    r_mod = mod.shape[1]
    row_spec = lambda w: pl.BlockSpec((tm, w), lambda i: (i, 0))
    in_specs = [row_spec(D_MODEL), row_spec(512), row_spec(512), row_spec(2 * D_MODEL),
                pl.BlockSpec((1, r_mod, 4 * D_MODEL), lambda i: (i // tiles_per_batch, 0, 0))]
    args = [x_flat, oa, ob, mg, mod]
    if packed:
        in_specs += [row_spec(D_FF), row_spec(D_FF), row_spec(LANES)]
        tpos = jnp.broadcast_to((jnp.arange(rows, dtype=I32) % t_new)[:, None], (rows, LANES))
        args += list(hist) + [tpos]
        tail_shape = jax.ShapeDtypeStruct((rows, D_FF), F32)
        tail_spec = row_spec(D_FF)
    else:
        nb = n_tiles // tiles_per_batch
        tail_shape = jax.ShapeDtypeStruct((nb, 8, D_FF), F32)
        tail_spec = pl.BlockSpec((1, 8, D_FF), lambda i: (i // tiles_per_batch, 0, 0))
    in_specs += [_const_spec(a.shape) for a in fw]
    args += list(fw)
    kern = functools.partial(_finish_kernel, tm=tm, packed=packed, t_new=t_new,
                             tiles_per_batch=tiles_per_batch)
    return pl.pallas_call(
        kern,
        grid=(n_tiles,),
        in_specs=in_specs,
        out_specs=[row_spec(D_MODEL), tail_spec],
        out_shape=[jax.ShapeDtypeStruct((rows, D_MODEL), F32), tail_shape],
        scratch_shapes=[pltpu.VMEM((tm + 8, D_FF), F32)],
        compiler_params=_cparams(1),
        name="finish",
    )(*args)


def kernel(x_prompt, x_sample, cache_a_kv, cache_a_idx, cache_b_cmp_kv, cache_b_slc_kv, state_b_win_kv, state_ffn_conv, page_table, c_prompt, c_sample, w_ada, b_ada, w_in, cmp_pe, cmp_w1, cmp_b1, cmp_w2, cmp_b2, w_pa, w_pb, w_o, ln1_g, ln1_b, w_ffn_gate, w_ffn_up, conv_w, conv_b, w_down, ln2_g, ln2_b):
    d = D_MODEL
    b, s, _ = x_prompt.shape
    bd, tn, _ = x_sample.shape
    n_pages = page_table.shape[1]
    past = n_pages * PAGE_SIZE
    kvw = 2 * N_KV_B * HEAD_DIM
    assert s % PROJ_ROWS == 0 and s % FINISH_ROWS == 0 and s % KEY_CHUNK == 0 and tn <= 8
    assert s >= WINDOW + Q_TILE and state_b_win_kv.shape[1] == WINDOW and tn >= CONV_W - 1

    perm = _proj_perm()
    w_perm = jnp.where(perm[None, :] >= 0, jnp.take(w_in, np.maximum(perm, 0), axis=1), 0.0).astype(BF16)
    head_rows = np.concatenate([np.r_[g * HEAD_DIM:(g + 1) * HEAD_DIM, (4 + g) * HEAD_DIM:(5 + g) * HEAD_DIM]
                                for g in range(4)])
    vec = lambda a: a.reshape(1, -1)
    fw = (w_pa[head_rows].astype(BF16), w_pb[head_rows].astype(BF16), w_o.astype(BF16),
          vec(ln1_g), vec(ln1_b), w_ffn_gate.astype(BF16), w_ffn_up.astype(BF16),
          jnp.concatenate([conv_w, jnp.zeros((8 - CONV_W, D_FF), F32)], axis=0), vec(conv_b),
          w_down.astype(BF16), vec(ln2_g), vec(ln2_b))
    cw = _compress_weights(cmp_pe, cmp_w1, cmp_b1, cmp_w2, cmp_b2)

    rows_c = -(-(b + bd) // 16) * 16
    c_all = jnp.concatenate([c_prompt, c_sample, jnp.zeros((rows_c - b - bd, d), F32)], axis=0)
    ada = _ada(c_all, w_ada.astype(BF16), b_ada)
    ada_p, ada_s = ada[:b], ada[b:b + bd]

    cos_p, sin_p = _rope_tables(jnp.arange(s))
    (qa, qi, qb, kva_bf, ki_bf, kv_cmp_bf, kv_slc_bf, kv_win_bf, misc, mg,
     kva_t, ki_t, kv_cmp_t, kv_slc_t, kv_win_t) = _project(
        x_prompt.reshape(b * s, d), ada_p[:, None, :2 * d], cos_p, sin_p, w_perm,
        PROJ_ROWS, s // PROJ_ROWS, s // PROJ_ROWS, n_batch=b)
    r3 = lambda a: a.reshape(b, s, a.shape[-1])
    oa = _dsa(r3(qi), r3(qa), r3(misc), r3(ki_bf), r3(kva_bf),
              tq=Q_TILE, q_pos0=0, topk=min(DSA_TOPK, s // 4))
    n_ch = -(-s // CMP_STRIDE)
    assert s % CMP_STRIDE == 0 and n_ch % LANES == 0
    cmp_p = _compress(kv_cmp_bf.reshape(b, n_ch, CMP_STRIDE * kvw), cw)
    ob = _nsa(r3(qb), r3(misc), cmp_p, r3(kv_slc_bf), r3(kv_win_bf),
              tq=Q_TILE, q_pos0=0, n_cmp=n_ch - 1, n_slc=-(-s // SLC_BLOCK),
              win_pos0=0, win_len=WINDOW + Q_TILE)
    y_p, tail_p = _finish(x_prompt.reshape(b * s, d), oa.reshape(b * s, 512), ob.reshape(b * s, 512), mg,
                          ada_p[:, None, 2 * d:], None, fw, tm=FINISH_ROWS, packed=False, t_new=0,
                          tiles_per_batch=s // FINISH_ROWS)

    rows_s = bd * tn
    pos_s = past + jnp.arange(tn)
    cos_s, sin_s = _rope_tables(jnp.tile(pos_s, bd))
    rep = lambda a: jnp.repeat(a, tn, axis=0)[None]
    (qa_s, qi_s, qb_s, _, _, _, _, kv_win_s_bf, misc_s, mg_s,
     kva_s, ki_s, kv_cmp_s, kv_slc_s, kv_win_s) = _project(
        x_sample.reshape(rows_s, d), rep(ada_s[:, :2 * d]), cos_s, sin_s, w_perm, rows_s, 1, 1)

    def new16(a):
        a = a.reshape(bd, tn, a.shape[-1])
        return jnp.concatenate([a, jnp.zeros((bd, 16 - tn, a.shape[-1]), F32)], axis=1)

    def q8(a):
        a = a.reshape(bd, tn, a.shape[-1])
        return jnp.concatenate([a, jnp.zeros((bd, 8 - tn, a.shape[-1]), a.dtype)], axis=1)

    pages_t = lambda c: jnp.moveaxis(c, 1, -1).reshape(c.shape[0], -1, PAGE_SIZE)
    ki_all = _gather_pages(pages_t(cache_a_idx), page_table, new16(ki_s), mode="dup")
    kva_all = _gather_pages(pages_t(cache_a_kv), page_table, new16(kva_s), mode="ones")
    cmp_all = _gather_pages(pages_t(cache_b_cmp_kv), page_table, new16(kv_cmp_s), mode="chunks")
    slc_all = _gather_pages(pages_t(cache_b_slc_kv), page_table, new16(kv_slc_s), mode="ones")
    l_tot = past + tn
    lp = kva_all.shape[1]
    oa_s = _dsa(q8(qi_s), q8(qa_s), q8(misc_s), ki_all, kva_all,
                tq=8, q_pos0=past, topk=min(DSA_TOPK, l_tot // 4), kc=DECODE_KEY_CHUNK)
    assert cmp_all.shape[1] % LANES == 0
    cmp_s = _compress(cmp_all, cw)
    win_len = WINDOW + Q_TILE
    win_all = jnp.concatenate(
        [state_b_win_kv.reshape(bd, WINDOW, kvw).astype(BF16), kv_win_s_bf.reshape(bd, tn, kvw),
         jnp.zeros((bd, win_len - WINDOW - tn, kvw), BF16)], axis=1)
    ob_s = _nsa(q8(qb_s), q8(misc_s), cmp_s, slc_all, win_all,
                tq=8, kc=DECODE_KEY_CHUNK, q_pos0=past, n_cmp=-(-l_tot // CMP_STRIDE) - 1, n_slc=-(-l_tot // SLC_BLOCK),
                win_pos0=past - WINDOW, win_len=win_len)
    hist = state_ffn_conv
    zrow = jnp.zeros((bd, 1, D_FF), F32)
    e1 = jnp.concatenate([hist[:, 1:2]] + [zrow] * (tn - 1), axis=1).reshape(rows_s, D_FF)
    e2 = jnp.concatenate([hist[:, 0:1], hist[:, 1:2]] + [zrow] * (tn - 2), axis=1).reshape(rows_s, D_FF)
    y_s, ug_s = _finish(x_sample.reshape(rows_s, d), oa_s[:, :tn].reshape(rows_s, 512),
                        ob_s[:, :tn].reshape(rows_s, 512), mg_s, rep(ada_s[:, 2 * d:]), (e1, e2), fw,
                        tm=rows_s, packed=True, t_new=tn, tiles_per_batch=1)

    kvshape = (2, N_KV_A, HEAD_DIM)
    win_s = jnp.concatenate([state_b_win_kv[:, tn:], kv_win_s.reshape((bd, tn) + kvshape)], axis=1)
    untr = lambda a: a.reshape((b,) + kvshape + (a.shape[-1],)).transpose(0, 4, 1, 2, 3)
    return (y_p.reshape(b, s, d), y_s.reshape(bd, tn, d),
            untr(kva_t), kva_s.reshape((bd, tn) + kvshape),
            ki_t.transpose(0, 2, 1), ki_s.reshape(bd, tn, IDX_DIM),
            untr(kv_cmp_t), kv_cmp_s.reshape((bd, tn) + kvshape),
            untr(kv_slc_t), kv_slc_s.reshape((bd, tn) + kvshape),
            untr(kv_win_t[:, :, s - min(WINDOW, s):]), win_s,
            tail_p[:, 8 - (CONV_W - 1):], ug_s.reshape(bd, tn, D_FF)[:, tn - (CONV_W - 1):])
```

```python
import functools
import math

import numpy as np
import jax
import jax.numpy as jnp
from jax import lax
from jax.experimental import pallas as pl
from jax.experimental.pallas import tpu as pltpu

D_MODEL = 1024
PAGE_SIZE = 128
HEAD_DIM = 64
N_HEADS_A = 8
N_KV_A = 2
N_IDX_HEADS = 8
IDX_DIM = 64
DSA_TOPK = 256
N_HEADS_B = 8
N_KV_B = 2
G_B = N_HEADS_B // N_KV_B
CMP_LEN = 32
CMP_STRIDE = 16
CMP_HID = 2 * HEAD_DIM
SLC_BLOCK = 64
SLC_TOPN = 16
WINDOW = 512
D_FF = 2816
CONV_W = 3
ROPE_THETA = 10000.0
LN_EPS = 1e-5
DEPTH = 1
DN_ALPHA = (2 * DEPTH) ** 0.25
NEG = -1e30
BIG = 1e30
MASK_VALUE = -2e30

LANES = 128
VMEM_LIMIT_BYTES = 56 * 1024 * 1024
Q_TILE = 256
KEY_CHUNK = 512
DECODE_KEY_CHUNK = 2048
PAGES_PER_STEP = 16
PROJ_ROWS = 512
FINISH_ROWS = 512

F32 = jnp.float32
BF16 = jnp.bfloat16
I32 = jnp.int32
INT_MIN = -2 ** 31

_O_QA = 0
_O_KVA = _O_QA + N_HEADS_A * HEAD_DIM
_O_QI = _O_KVA + 2 * N_KV_A * HEAD_DIM
_O_KI = _O_QI + N_IDX_HEADS * IDX_DIM
_O_WI = _O_KI + IDX_DIM
_O_QB = _O_WI + N_IDX_HEADS
_O_KVB = _O_QB + N_HEADS_B * HEAD_DIM
_O_GB = _O_KVB + 6 * N_KV_B * HEAD_DIM
_O_MGA = _O_GB + 3 * N_HEADS_B
_O_MGB = _O_MGA + D_MODEL
_N_IN = _O_MGB + D_MODEL
_GATE_LANE0 = N_IDX_HEADS


def _cparams(n_axes):
    return pltpu.CompilerParams(dimension_semantics=("arbitrary",) * n_axes,
                                vmem_limit_bytes=VMEM_LIMIT_BYTES)


def _const_spec(shape):
    nd = len(shape)
    return pl.BlockSpec(shape, lambda *a: (0,) * nd, pipeline_mode=pl.Buffered(1))


def _dot(a, b):
    return jnp.dot(a, b, preferred_element_type=F32)


def _dot_nt(a, b):
    return lax.dot_general(a, b, (((1,), (1,)), ((), ())), preferred_element_type=F32)


def _sigmoid(x):
    return 1.0 / (1.0 + jnp.exp(-x))


def _gelu_tanh(x):
    return 0.5 * x * (1.0 + jnp.tanh(math.sqrt(2.0 / math.pi) * (x + 0.044715 * (x * x * x))))


def _layernorm(x, g, b):
    mu = jnp.mean(x, axis=-1, keepdims=True)
    xc = x - mu
    var = jnp.mean(xc * xc, axis=-1, keepdims=True)
    return xc * lax.rsqrt(var + LN_EPS) * g + b


def _ada_kernel(c_ref, w_ref, b_ref, o_ref):
    c = c_ref[...]
    s = (c * _sigmoid(c)).astype(BF16)
    o_ref[...] = _dot(s, w_ref[...]) + b_ref[...]


def _ada(c_all, w_ada_bf, b_ada):
    rows = c_all.shape[0]
    n_out = w_ada_bf.shape[1]
    tn = D_MODEL
    return pl.pallas_call(
        _ada_kernel,
        grid=(n_out // tn,),
        in_specs=[pl.BlockSpec((rows, D_MODEL), lambda j: (0, 0)),
                  pl.BlockSpec((D_MODEL, tn), lambda j: (0, j)),
                  pl.BlockSpec((1, tn), lambda j: (0, j))],
        out_specs=pl.BlockSpec((rows, tn), lambda j: (0, j)),
        out_shape=jax.ShapeDtypeStruct((rows, n_out), F32),
        compiler_params=_cparams(1),
        name="ada",
    )(c_all, w_ada_bf, b_ada.reshape(1, n_out))


_P_QA, _P_QI, _P_QB = 0, 512, 1024
_P_K = 1536
_P_V = 2176
_P_MISC = 2688
_P_MG = 2816
_P_END = _P_MG + 2 * D_MODEL


def _proj_perm():
    def paired_heads(base):
        cols = []
        for g in range(4):
            cols += list(range(base + g * HEAD_DIM, base + (g + 1) * HEAD_DIM))
            cols += list(range(base + (4 + g) * HEAD_DIM, base + (5 + g) * HEAD_DIM))
        return cols
    kv = 2 * N_KV_B * HEAD_DIM
    half = N_KV_B * HEAD_DIM
    cols = []
    cols += paired_heads(_O_QA)
    cols += list(range(_O_QI, _O_QI + N_IDX_HEADS * IDX_DIM))
    cols += paired_heads(_O_QB)
    cols += list(range(_O_KVA, _O_KVA + N_KV_A * HEAD_DIM))
    cols += list(range(_O_KI, _O_KI + IDX_DIM)) * 2
    for r in range(3):
        cols += list(range(_O_KVB + r * kv, _O_KVB + r * kv + half))
    cols += list(range(_O_KVA + N_KV_A * HEAD_DIM, _O_KVA + 2 * N_KV_A * HEAD_DIM))
    for r in range(3):
        cols += list(range(_O_KVB + r * kv + half, _O_KVB + (r + 1) * kv))
    misc = list(range(_O_WI, _O_WI + N_IDX_HEADS)) + list(range(_O_GB, _O_GB + 3 * N_HEADS_B))
    cols += misc + [-1] * (LANES - len(misc))
    cols += list(range(_O_MGA, _O_MGA + 2 * D_MODEL))
    assert len(cols) == _P_END
    return np.asarray(cols, np.int32)


def _with_ones(v, low):
    return jnp.where(low, v, 1.0), jnp.where(low, 1.0, v)


def _proj_kernel(x_ref, mod_ref, cos_ref, sin_ref, w_ref,
                 qa_o, qi_o, qb_o, kvab_o, kib_o, cmpb_o, slcb_o, winb_o, misc_o, mg_o,
                 kva_o, ki_o, cmp_o, slc_o, win_o, *, transposed):
    mod = mod_ref[0]
    shift1 = mod[:, :D_MODEL]
    scale1 = mod[:, D_MODEL:]
    h = (x_ref[...] * (1.0 + scale1) + shift1).astype(BF16)
    tm = h.shape[0]
    cos = cos_ref[...]
    sin = sin_ref[...]
    lane = lax.broadcasted_iota(I32, (tm, LANES), 1)
    first_half = (lane & (HEAD_DIM - 1)) < (HEAD_DIM // 2)
    low = lane < HEAD_DIM

    def rope(y):
        partner = jnp.where(first_half, pltpu.roll(y, LANES - HEAD_DIM // 2, 1),
                            pltpu.roll(y, HEAD_DIM // 2, 1))
        return y * cos + partner * sin

    def mm(a, b):
        return _dot(h, w_ref[:, a:b])

    q_scale = HEAD_DIM ** -0.5 * math.log2(math.e)
    for off, out, scale in ((_P_QA, qa_o, q_scale), (_P_QI, qi_o, 1.0), (_P_QB, qb_o, q_scale)):
        y = mm(off, off + 4 * LANES)
        for j in range(4):
            out[:, j * LANES:(j + 1) * LANES] = (rope(y[:, j * LANES:(j + 1) * LANES]) * scale).astype(BF16)

    yk = mm(_P_K, _P_K + 5 * LANES)
    ka, ki2, kc, ks, kw = [rope(yk[:, j * LANES:(j + 1) * LANES]) for j in range(5)]
    yv = mm(_P_V, _P_V + 4 * LANES)
    va, vc, vs, vw = [yv[:, j * LANES:(j + 1) * LANES] for j in range(4)]

    for out, k, v in ((kvab_o, ka, va), (slcb_o, ks, vs)):
        v0, v1 = _with_ones(v, low)
        out[:, :LANES] = k.astype(BF16)
        out[:, LANES:2 * LANES] = v0.astype(BF16)
        out[:, 2 * LANES:] = v1.astype(BF16)
    kib_o[...] = ki2.astype(BF16)
    for out, k, v in ((cmpb_o, kc, vc), (winb_o, kw, vw)):
        out[:, :LANES] = k.astype(BF16)
        out[:, LANES:] = v.astype(BF16)

    for out, k, v in ((kva_o, ka, va), (cmp_o, kc, vc), (slc_o, ks, vs), (win_o, kw, vw)):
        if transposed:
            out[0, :LANES, :] = k.T
            out[0, LANES:, :] = v.T
        else:
            out[:, :LANES] = k
            out[:, LANES:] = v
    if transposed:
        ki_o[0] = ki2.T[:IDX_DIM, :]
    else:
        ki_o[...] = ki2[:, :IDX_DIM]

    ym = mm(_P_MISC, _P_MISC + LANES)
    wi_scale = (N_IDX_HEADS * IDX_DIM) ** -0.5
    misc_o[...] = jnp.where(lane < N_IDX_HEADS, ym * wi_scale, _sigmoid(ym))
    for j in range(2):
        a = _P_MG + j * D_MODEL
        mg_o[:, j * D_MODEL:(j + 1) * D_MODEL] = _sigmoid(mm(a, a + D_MODEL)).astype(BF16)


def _project(x_flat, mod, cos_t, sin_t, w_perm, tm, tiles_per_group, tiles_per_table, n_batch=None):
    rows = x_flat.shape[0]
    n_tiles = rows // tm
    r_mod = mod.shape[1]
    transposed = n_batch is not None

    def row_spec(width):
        return pl.BlockSpec((tm, width), lambda i: (i, 0))

    bf_defs = [(512, BF16), (512, BF16), (512, BF16),
               (384, BF16), (LANES, BF16),
               (256, BF16), (384, BF16), (256, BF16),
               (LANES, F32),
               (2 * D_MODEL, BF16)]
    f32_widths = [256, IDX_DIM, 256, 256, 256]
    out_specs = [row_spec(w) for w, _ in bf_defs]
    out_shape = [jax.ShapeDtypeStruct((rows, w), dt) for w, dt in bf_defs]
    if transposed:
        s = rows // n_batch
        tpb = s // tm
        out_specs += [pl.BlockSpec((1, w, tm), lambda i: (i // tpb, 0, i % tpb)) for w in f32_widths]
        out_shape += [jax.ShapeDtypeStruct((n_batch, w, s), F32) for w in f32_widths]
    else:
        out_specs += [row_spec(w) for w in f32_widths]
        out_shape += [jax.ShapeDtypeStruct((rows, w), F32) for w in f32_widths]
    return pl.pallas_call(
        functools.partial(_proj_kernel, transposed=transposed),
        grid=(n_tiles,),
        in_specs=[row_spec(D_MODEL),
                  pl.BlockSpec((1, r_mod, 2 * D_MODEL), lambda i: (i // tiles_per_group, 0, 0)),
                  pl.BlockSpec((tm, LANES), lambda i: (i % tiles_per_table, 0)),
                  pl.BlockSpec((tm, LANES), lambda i: (i % tiles_per_table, 0)),
                  _const_spec((D_MODEL, _P_END))],
        out_specs=out_specs,
        out_shape=out_shape,
        compiler_params=_cparams(1),
        name="proj",
    )(x_flat, mod, cos_t, sin_t, w_perm)


def _rope_tables(pos):
    half = HEAD_DIM // 2
    inv = ROPE_THETA ** (-jnp.arange(half, dtype=F32) / half)
    ang = pos.astype(F32)[:, None] * inv[None, :]
    cos = jnp.tile(jnp.cos(ang), (1, LANES // half))
    sign = np.where((np.arange(LANES) % HEAD_DIM) < half, -1.0, 1.0).astype(np.float32)
    sin = jnp.tile(jnp.sin(ang), (1, LANES // half)) * sign[None, :]
    return cos, sin


def _gather_kernel(pt_ref, *refs, n_groups, mode):
    pages = refs[:PAGES_PER_STEP]
    new_ref = refs[PAGES_PER_STEP]
    out_ref = refs[PAGES_PER_STEP + 1]
    g = pl.program_id(1)

    if mode == "chunks":
        xs = refs[PAGES_PER_STEP + 2]
        n_rows = PAGES_PER_STEP * PAGE_SIZE // CMP_STRIDE
        w = 2 * LANES

        @pl.when(g < n_groups)
        def _():
            for i in range(PAGES_PER_STEP):
                x = pages[i][0].T
                xs[0, i * PAGE_SIZE:(i + 1) * PAGE_SIZE, :] = x[:, :LANES]
                xs[1, i * PAGE_SIZE:(i + 1) * PAGE_SIZE, :] = x[:, LANES:]
            for c in range(CMP_STRIDE):
                for hf in range(2):
                    a = c * w + hf * LANES
                    out_ref[0, :, a:a + LANES] = xs[hf, pl.ds(c, n_rows, stride=CMP_STRIDE), :].astype(BF16)

        @pl.when(g == n_groups)
        def _():
            new = new_ref[0]
            first = jnp.concatenate([new[c:c + 1, :] for c in range(CMP_STRIDE)], axis=1)
            out_ref[0] = jnp.zeros(out_ref.shape[1:], BF16)
            out_ref[0, :16, :] = jnp.concatenate([first, jnp.zeros((15, CMP_STRIDE * w), F32)],
                                                 axis=0).astype(BF16)
        return

    def put(r0, rows):
        n = rows.shape[0]
        if mode == "ones":
            low = lax.broadcasted_iota(I32, (n, LANES), 1) < HEAD_DIM
            v0, v1 = _with_ones(rows[:, LANES:], low)
            out_ref[0, r0:r0 + n, :LANES] = rows[:, :LANES].astype(BF16)
            out_ref[0, r0:r0 + n, LANES:2 * LANES] = v0.astype(BF16)
            out_ref[0, r0:r0 + n, 2 * LANES:] = v1.astype(BF16)
        elif mode == "dup":
            out_ref[0, r0:r0 + n, :] = jnp.concatenate([rows, rows], axis=1).astype(BF16)
        else:
            out_ref[0, r0:r0 + n, :] = rows.astype(BF16)

    @pl.when(g < n_groups)
    def _():
        for i in range(PAGES_PER_STEP):
            put(i * PAGE_SIZE, pages[i][0].T)

    @pl.when(g == n_groups)
    def _():
        out_ref[0] = jnp.zeros(out_ref.shape[1:], BF16)
        put(0, new_ref[0])


def _gather_pages(cache_t, page_table, new_rows, mode="plain"):
    n_phys, w, _ = cache_t.shape
    b, n_pages = page_table.shape
    assert n_pages % PAGES_PER_STEP == 0
    n_groups = n_pages // PAGES_PER_STEP
    rows_step = PAGES_PER_STEP * PAGE_SIZE
    scratch = []
    if mode == "chunks":
        assert w == 2 * LANES
        rows_step, w_out = rows_step // CMP_STRIDE, CMP_STRIDE * w
        scratch = [pltpu.VMEM((2, PAGES_PER_STEP * PAGE_SIZE, LANES), F32)]
    else:
        w_out = {"plain": w, "ones": w + LANES, "dup": 2 * w}[mode]

    def page_spec(i):
        def imap(bi, g, pt):
            gg = jnp.minimum(g, n_groups - 1)
            return (pt[bi, gg * PAGES_PER_STEP + i], 0, 0)
        return pl.BlockSpec((1, w, PAGE_SIZE), imap)

    grid_spec = pltpu.PrefetchScalarGridSpec(
        num_scalar_prefetch=1,
        grid=(b, n_groups + 1),
        in_specs=[page_spec(i) for i in range(PAGES_PER_STEP)]
        + [pl.BlockSpec((1, new_rows.shape[1], w), lambda bi, g, pt: (bi, 0, 0))],
        out_specs=pl.BlockSpec((1, rows_step, w_out), lambda bi, g, pt: (bi, g, 0)),
        scratch_shapes=scratch,
    )
    return pl.pallas_call(
        functools.partial(_gather_kernel, n_groups=n_groups, mode=mode),
        grid_spec=grid_spec,
        out_shape=jax.ShapeDtypeStruct((b, (n_groups + 1) * rows_step, w_out), BF16),
        compiler_params=_cparams(2),
        name="gather_pages",
    )(page_table, *([cache_t] * PAGES_PER_STEP), new_rows)


def _cmp_kernel(x_ref, wa_ref, wb_ref, pea_ref, peb_ref, b1_ref, w2_ref, b2_ref, o_ref):
    x = x_ref[0].astype(BF16)
    r = x.shape[0]
    ha = _dot(x, wa_ref[...])
    hb = _dot(x, wb_ref[...])
    hpe = _dot(pea_ref[...], wa_ref[...]) + _dot(peb_ref[...], wb_ref[...])
    hpe = hpe[0:1, :] + b1_ref[...]
    hb_next = pltpu.roll(hb, r - 1, 0)
    hid = _gelu_tanh(ha + hb_next + hpe).astype(BF16)
    o_ref[0] = (_dot(hid, w2_ref[...]) + b2_ref[...]).astype(o_ref.dtype)


def _compress_weights(cmp_pe, cmp_w1, cmp_b1, cmp_w2, cmp_b2):
    eye = jnp.eye(2, dtype=F32)
    kvw = 2 * N_KV_B * HEAD_DIM

    def big(w1h):
        t = jnp.einsum('jcdf,jJ,hH->cjhdJHf', w1h, eye, eye)
        return t.reshape(CMP_STRIDE * kvw, 2 * N_KV_B * CMP_HID).astype(BF16)

    def pe_row(peh):
        t = jnp.broadcast_to(peh.transpose(1, 0, 2)[:, :, None, :], (CMP_STRIDE, 2, N_KV_B, HEAD_DIM))
        row = t.reshape(1, CMP_STRIDE * kvw)
        return jnp.concatenate([row, jnp.zeros((15, CMP_STRIDE * kvw), F32)], axis=0).astype(BF16)

    wa = big(cmp_w1[:, :CMP_STRIDE])
    wb = big(cmp_w1[:, CMP_STRIDE:])
    pea = pe_row(cmp_pe[:, :CMP_STRIDE])
    peb = pe_row(cmp_pe[:, CMP_STRIDE:])
    b1 = jnp.broadcast_to(cmp_b1[:, None, :], (2, N_KV_B, CMP_HID)).reshape(1, -1)
    w2 = jnp.einsum('jfd,jJ,hH->jhfJHd', cmp_w2, eye, eye).reshape(2 * N_KV_B * CMP_HID, kvw).astype(BF16)
    b2 = jnp.broadcast_to(cmp_b2[:, None, :], (2, N_KV_B, HEAD_DIM)).reshape(1, -1)
    return wa, wb, pea, peb, b1, w2, b2


def _compress(x_chunks, cw):
    b, r, cwid = x_chunks.shape
    wa, wb, pea, peb, b1, w2, b2 = cw
    return pl.pallas_call(
        _cmp_kernel,
        grid=(b,),
        in_specs=[pl.BlockSpec((1, r, cwid), lambda i: (i, 0, 0)),
                  _const_spec(wa.shape), _const_spec(wb.shape), _const_spec(pea.shape),
                  _const_spec(peb.shape), _const_spec(b1.shape), _const_spec(w2.shape),
                  _const_spec(b2.shape)],
        out_specs=pl.BlockSpec((1, r, w2.shape[1]), lambda i: (i, 0, 0)),
        out_shape=jax.ShapeDtypeStruct((b, r, w2.shape[1]), BF16),
        compiler_params=_cparams(1),
        name="compress",
    )(x_chunks, wa, wb, pea, peb, b1, w2, b2)


def _stack_group_queries(q_ref, tq):
    lane = lax.broadcasted_iota(I32, (tq, LANES), 1)
    low = lane < HEAD_DIM
    q = q_ref[0].astype(F32)
    cols = [q[:, g * LANES:(g + 1) * LANES] for g in range(4)]
    q0 = jnp.concatenate([jnp.where(low, c, 0.0) for c in cols], axis=0).astype(BF16)
    q1 = jnp.concatenate([jnp.where(low, 0.0, c) for c in cols], axis=0).astype(BF16)
    return (q0, q1), low


def _flash_init(m_scr, acc_scr):
    m_scr[...] = jnp.full(m_scr.shape, NEG, F32)
    acc_scr[...] = jnp.zeros(acc_scr.shape, F32)


def _flash_step(qs, k128, v0, v1, biases, m_scr, acc_scr, tq):
    half = 4 * tq
    kc = k128.shape[0]
    s = jnp.concatenate([_dot_nt(qs[0], k128), _dot_nt(qs[1], k128)], axis=0)
    s = s + jnp.concatenate([biases[0]] * 4 + [biases[1]] * 4, axis=0)
    m_prev = m_scr[...]
    m_new = jnp.maximum(m_prev, jnp.max(s, axis=1, keepdims=True))
    alpha = jnp.exp2(m_prev - m_new)
    p = jnp.exp2(s - jnp.concatenate([m_new] * (kc // LANES), axis=1)).astype(BF16)
    m_scr[...] = m_new
    pv = jnp.concatenate([_dot(p[:half], v0), _dot(p[half:], v1)], axis=0)
    acc_scr[...] = alpha * acc_scr[...] + pv


def _flash_result(acc_scr):
    acc = acc_scr[...]
    l = pltpu.roll(acc, HEAD_DIM, 1)
    return jnp.where(l > 0.0, acc / l, 0.0)


def _softmax_rows(s):
    m = jnp.maximum(jnp.max(s, axis=1, keepdims=True), NEG)
    p = jnp.exp2(s - m)
    l = jnp.sum(p, axis=1, keepdims=True)
    return jnp.where(l > 0.0, p / l, 0.0)


def _key_of(v):
    bits = lax.bitcast_convert_type(v, I32)
    return jnp.where(bits < 0, bits ^ 0x7FFFFFFF, bits)


def _value_of(key):
    return lax.bitcast_convert_type(jnp.where(key < 0, key ^ 0x7FFFFFFF, key), F32)


VALUE_SEARCH_ROUNDS = 26
MAX_SEARCH_ROUNDS = VALUE_SEARCH_ROUNDS + 34
ROUNDS_PER_EXIT_TEST = 3


def _dsa_kernel(qi_ref, qa_ref, misc_ref, ki_ref, kv_ref, o_ref,
                key_scr, jsel_scr, m_scr, acc_scr, *, tq, kc, q_pos0, topk, n_chunks):
    q0 = q_pos0 + pl.program_id(1) * tq
    row = lax.broadcasted_iota(I32, (tq, 1), 0)
    qpos = q0 + row
    whole = tq <= 16
    nkc = n_chunks if whole else jnp.minimum(lax.div(q0 + tq - 1, kc) + 1, n_chunks)
    lane = lax.broadcasted_iota(I32, (tq, LANES), 1)
    low = lane < IDX_DIM
    col_idx = lax.broadcasted_iota(I32, (tq, kc), 1)
    n_tiles = kc // LANES

    qi = qi_ref[0].astype(F32)
    misc = misc_ref[0]
    q_heads, w_heads = [], []
    for hh in range(N_IDX_HEADS):
        c = qi[:, (hh // 2) * LANES:(hh // 2 + 1) * LANES]
        q_heads.append(jnp.where(low, c, 0.0) if hh % 2 == 0 else jnp.where(low, 0.0, c))
        w_heads.append(misc[:, hh:hh + 1])
    q_all = jnp.concatenate(q_heads, axis=0).astype(BF16)

    def score_body(c, carry):
        vmin, vmax = carry
        k0 = pl.multiple_of(c * kc, kc)
        d = _dot_nt(q_all, ki_ref[0, pl.ds(k0, kc), :])
        acc = None
        for hh in range(N_IDX_HEADS):
            t = w_heads[hh] * jnp.maximum(d[hh * tq:(hh + 1) * tq], 0.0)
            acc = t if acc is None else acc + t
        acc = jnp.where(acc == 0.0, 0.0, acc)
        adm = k0 + col_idx <= qpos
        key_scr[c] = _key_of(jnp.where(adm, acc, NEG))
        lo_c = jnp.where(adm, acc, jnp.inf)
        hi_c = jnp.where(adm, acc, -jnp.inf)
        for j in range(n_tiles):
            vmin = jnp.minimum(vmin, lo_c[:, j * LANES:(j + 1) * LANES])
            vmax = jnp.maximum(vmax, hi_c[:, j * LANES:(j + 1) * LANES])
        return vmin, vmax

    vmin, vmax = lax.fori_loop(0, nkc, score_body, (jnp.full((tq, LANES), jnp.inf, F32),
                                                    jnp.full((tq, LANES), -jnp.inf, F32)))
    vmin = jnp.min(vmin, axis=1, keepdims=True)
    vmax = jnp.max(vmax, axis=1, keepdims=True)

    def count_ge(cands):
        if whole:
            keys = key_scr[...]
            counts = []
            for cand in cands:
                hit = jnp.sum(jnp.where(keys >= cand[None], 1.0, 0.0), axis=0)
                part = hit[:, :LANES]
                for j in range(1, n_tiles):
                    part = part + hit[:, j * LANES:(j + 1) * LANES]
                counts.append(jnp.sum(part, axis=1, keepdims=True))
            return counts

        def body(c, parts):
            keys = key_scr[c]
            out = []
            for cand, part in zip(cands, parts):
                hit = jnp.where(keys >= cand, 1.0, 0.0)
                for j in range(n_tiles):
                    part = part + hit[:, j * LANES:(j + 1) * LANES]
                out.append(part)
            return tuple(out)
        parts = lax.fori_loop(0, nkc, body, tuple(jnp.zeros((tq, LANES), F32) for _ in cands))
        return [jnp.sum(part, axis=1, keepdims=True) for part in parts]

    def count_eq_upto(lo, t):
        def body(c, part):
            hit = jnp.where(key_scr[c] == lo, jnp.where(c * kc + col_idx <= t, 1.0, 0.0), 0.0)
            for j in range(n_tiles):
                part = part + hit[:, j * LANES:(j + 1) * LANES]
            return part
        part = lax.fori_loop(0, nkc, body, jnp.zeros((tq, LANES), F32))
        return jnp.sum(part, axis=1, keepdims=True)

    lo, jsel = _select_threshold(count_ge, count_eq_upto, vmin, vmax, topk, tq, n_chunks * kc, jsel_scr)
    lo1 = lo + 1
    _dsa_attend(qa_ref, kv_ref, o_ref, key_scr, m_scr, acc_scr, lo, lo1, jsel, qpos, col_idx, nkc, tq, kc)


def _select_threshold(count_ge, count_eq_upto, vmin, vmax, topk, tq, n_pos, jsel_scr):
    kf = float(topk)
    kmin = _key_of(vmin)
    zero = jnp.zeros((tq, 1), I32)
    c_all, c_nonneg, c_pos = count_ge([kmin, zero, zero + 1])
    at_zero = jnp.logical_and(c_pos < kf, c_nonneg >= kf)
    above = c_pos >= kf
    lo0 = jnp.where(at_zero, 0, jnp.where(above, 1, kmin))
    clo0 = jnp.where(at_zero, c_nonneg, jnp.where(above, c_pos, c_all))
    hi0 = jnp.where(above, _key_of(vmax) + 1, 0)
    chi0 = jnp.where(above, 0.0, c_nonneg)
    take_all = c_all <= kf
    lo0 = jnp.where(take_all, kmin, lo0)
    clo0 = jnp.where(take_all, c_all, clo0)
    todo0 = jnp.where(jnp.logical_or(jnp.logical_or(take_all, at_zero), clo0 == kf), 0.0, 1.0)

    def search_cond(carry):
        r, _, _, _, _, todo = carry
        return jnp.logical_and(r < MAX_SEARCH_ROUNDS, jnp.max(todo) > 0.5)

    def search_body(carry):
        return lax.fori_loop(0, ROUNDS_PER_EXIT_TEST, lambda _, c: search_round(c), carry)

    def search_round(carry):
        r, lo, hi, clo, chi, todo = carry
        vlo = _value_of(lo)
        vmid = vlo + (_value_of(hi) - vlo) * 0.5
        kmid = lax.shift_right_arithmetic(lo, 1) + lax.shift_right_arithmetic(hi, 1) + (lo & hi & 1)
        mid = jnp.where(r < VALUE_SEARCH_ROUNDS, _key_of(vmid), kmid)
        mid = jnp.minimum(jnp.maximum(mid, lo + 1), hi - 1)
        cnt, = count_ge([mid])
        active = todo > 0.5
        up = jnp.logical_and(active, cnt >= kf)
        down = jnp.logical_and(active, cnt < kf)
        lo = jnp.where(up, mid, lo)
        clo = jnp.where(up, cnt, clo)
        hi = jnp.where(down, mid, hi)
        chi = jnp.where(down, cnt, chi)
        done = jnp.logical_or(clo == kf, hi == lo + 1)
        return r + 1, lo, hi, clo, chi, jnp.where(done, 0.0, todo)

    _, lo, _, clo, _, _ = lax.while_loop(
        search_cond, search_body, (jnp.int32(0), lo0, hi0, clo0, chi0, todo0))
    lo1 = lo + 1
    excess = jnp.where(take_all, 0.0, clo - kf)

    jsel_scr[...] = jnp.full(jsel_scr.shape, n_pos, I32)
    n_bits = max(1, int(math.ceil(math.log2(n_pos))))

    @pl.when(jnp.max(excess) > 0.5)
    def _():
        c_above, = count_ge([lo1])
        need = kf - c_above

        def idx_body(i, jcur):
            step = lax.shift_left(jnp.int32(1), n_bits - 1 - i)
            return jnp.where(count_eq_upto(lo, jcur + step - 1) < need, jcur + step, jcur)

        jfin = lax.fori_loop(0, n_bits, idx_body, jnp.zeros((tq, 1), I32))
        jsel_scr[...] = jnp.broadcast_to(jnp.where(excess > 0.5, jfin, n_pos), jsel_scr.shape)

    return lo, jsel_scr[:, :1]


def _dsa_attend(qa_ref, kv_ref, o_ref, key_scr, m_scr, acc_scr, lo, lo1, jsel, qpos, col_idx, nkc, tq, kc):
    qs, low_q = _stack_group_queries(qa_ref, tq)
    _flash_init(m_scr, acc_scr)

    def att_body(c, carry):
        k0 = pl.multiple_of(c * kc, kc)
        kidx = k0 + col_idx
        thr = jnp.where(kidx <= jsel, lo, lo1)
        bias = jnp.where(key_scr[c] >= thr, jnp.where(kidx <= qpos, 0.0, MASK_VALUE), MASK_VALUE)
        kvc = kv_ref[0, pl.ds(k0, kc), :]
        _flash_step(qs, kvc[:, :LANES], kvc[:, LANES:2 * LANES], kvc[:, 2 * LANES:], (bias, bias),
                    m_scr, acc_scr, tq)
        return carry

    lax.fori_loop(0, nkc, att_body, 0)

    res = _flash_result(acc_scr)
    for g in range(4):
        o = jnp.where(low_q, res[g * tq:(g + 1) * tq], res[(4 + g) * tq:(5 + g) * tq])
        o_ref[0, :, g * LANES:(g + 1) * LANES] = o.astype(o_ref.dtype)


def _dsa(qi, qa, misc, ki_all, kv_all, *, tq, q_pos0, topk, kc=KEY_CHUNK):
    b, tq_all, _ = qa.shape
    lp = ki_all.shape[1]
    assert lp % kc == 0 and kc >= topk and tq_all % tq == 0
    n_chunks = lp // kc
    kern = functools.partial(_dsa_kernel, tq=tq, kc=kc, q_pos0=q_pos0, topk=topk, n_chunks=n_chunks)
    qspec = lambda w: pl.BlockSpec((1, tq, w), lambda bi, qi_: (bi, qi_, 0))
    kspec = lambda w: pl.BlockSpec((1, lp, w), lambda bi, qi_: (bi, 0, 0))
    return pl.pallas_call(
        kern,
        grid=(b, tq_all // tq),
        in_specs=[qspec(512), qspec(512), qspec(LANES), kspec(LANES), kspec(3 * LANES)],
        out_specs=qspec(512),
        out_shape=jax.ShapeDtypeStruct((b, tq_all, 512), BF16),
        scratch_shapes=[pltpu.VMEM((n_chunks, tq, kc), I32),
                        pltpu.VMEM((tq, LANES), I32),
                        pltpu.VMEM((8 * tq, LANES), F32),
                        pltpu.VMEM((8 * tq, LANES), F32)],
        compiler_params=_cparams(2),
        name="dsa",
    )(qi, qa, misc, ki_all, kv_all)


def _dsa_decode_kernel(pt_ref, qi_ref, qa_ref, misc_ref, *refs, n_groups, q_pos0, topk):
    ps = PAGES_PER_STEP
    idx_pages, kv_pages = refs[:ps], refs[ps:2 * ps]
    ki_new_ref, kv_new_ref, o_ref, key_scr, jsel_scr, lo_scr, m_scr, l_scr, acc_scr = refs[2 * ps:]
    tq = qi_ref.shape[1]
    kc = ps * PAGE_SIZE
    n_chunks = n_groups + 1
    g = pl.program_id(1)
    qpos = q_pos0 + lax.broadcasted_iota(I32, (tq, 1), 0)
    col_idx = lax.broadcasted_iota(I32, (tq, kc), 1)

    def page_scores(ki_t):
        qi = qi_ref[0].astype(F32)
        misc = misc_ref[0]
        q_all = jnp.concatenate([qi[:, hh * IDX_DIM:(hh + 1) * IDX_DIM] for hh in range(N_IDX_HEADS)],
                                axis=0).astype(BF16)
        d = _dot(q_all, ki_t.astype(BF16))
        acc = None
        for hh in range(N_IDX_HEADS):
            t = misc[:, hh:hh + 1] * jnp.maximum(d[hh * tq:(hh + 1) * tq], 0.0)
            acc = t if acc is None else acc + t
        return acc

    def keys_of(acc, k0):
        acc = jnp.where(acc == 0.0, 0.0, acc)
        return _key_of(jnp.where(k0 + col_idx <= qpos, acc, NEG))

    def attend(pages_t, keys, k0):
        n = len(pages_t)
        (q0s, q1s), _ = _stack_group_queries(qa_ref, tq)
        q_all = jnp.concatenate([q0s, q1s], axis=0)
        lo = lo_scr[:, :1]
        jsel = jsel_scr[:, :1]
        kidx = k0 + lax.broadcasted_iota(I32, keys.shape, 1)
        thr = jnp.where(kidx <= jsel, lo, lo + 1)
        bias = jnp.where(keys >= thr, jnp.where(kidx <= qpos, 0.0, MASK_VALUE), MASK_VALUE)
        kv = [p.astype(BF16) for p in pages_t]
        s = jnp.concatenate([_dot(q_all, p[:LANES]) for p in kv], axis=1)
        s = s + jnp.concatenate([bias] * 8, axis=0)
        m_prev = m_scr[...]
        m_new = jnp.maximum(m_prev, jnp.max(s, axis=1, keepdims=True))
        alpha = jnp.exp2(m_prev - m_new)
        p = jnp.exp2(s - jnp.concatenate([m_new] * n, axis=1))
        l_scr[...] = alpha * l_scr[...] + jnp.sum(p, axis=1, keepdims=True)
        m_scr[...] = m_new
        pb = p.astype(BF16)
        pv = None
        for i in range(n):
            t = _dot_nt(pb[:, i * PAGE_SIZE:(i + 1) * PAGE_SIZE], kv[i][LANES:])
            pv = t if pv is None else pv + t
        acc_scr[...] = alpha * acc_scr[...] + pv

    @pl.when(g < n_groups)
    def _():
        sc = jnp.concatenate([page_scores(idx_pages[i][0]) for i in range(ps)], axis=1)
        key_scr[g] = keys_of(sc, g * kc)

    @pl.when(g == n_groups)
    def _():
        sc = jnp.concatenate([page_scores(ki_new_ref[0]), jnp.zeros((tq, kc - PAGE_SIZE), F32)], axis=1)
        key_scr[n_groups] = keys_of(sc, n_groups * kc)

        keys = key_scr[...]
        vals = _value_of(keys)
        adm = keys > _key_of(jnp.full((1, 1, 1), NEG, F32))
        vmin = jnp.min(jnp.min(jnp.where(adm, vals, jnp.inf), axis=0), axis=1, keepdims=True)
        vmax = jnp.max(jnp.max(jnp.where(adm, vals, -jnp.inf), axis=0), axis=1, keepdims=True)
        pos = (lax.broadcasted_iota(I32, keys.shape, 0) * kc + lax.broadcasted_iota(I32, keys.shape, 2))

        def lane_total(x):
            part = x[:, :LANES]
            for j in range(1, kc // LANES):
                part = part + x[:, j * LANES:(j + 1) * LANES]
            return jnp.sum(part, axis=1, keepdims=True)

        def count_ge(cands):
            return [lane_total(jnp.sum(jnp.where(keys >= c[None], 1.0, 0.0), axis=0)) for c in cands]

        def count_eq_upto(lo, t):
            return lane_total(jnp.sum(jnp.where(keys == lo[None], jnp.where(pos <= t[None], 1.0, 0.0), 0.0),
                                      axis=0))

        lo, jsel = _select_threshold(count_ge, count_eq_upto, vmin, vmax, topk, tq, n_chunks * kc, jsel_scr)
        lo_scr[...] = jnp.broadcast_to(lo, lo_scr.shape)
        m_scr[...] = jnp.full(m_scr.shape, NEG, F32)
        l_scr[...] = jnp.zeros(l_scr.shape, F32)
        acc_scr[...] = jnp.zeros(acc_scr.shape, F32)
        attend([kv_new_ref[0]], key_scr[n_groups][:, :PAGE_SIZE], n_groups * kc)

    @pl.when(g > n_groups)
    def _():
        gi = g - n_groups - 1
        attend([kv_pages[i][0] for i in range(ps)], key_scr[gi], gi * kc)

    @pl.when(g == 2 * n_groups)
    def _():
        low = lax.broadcasted_iota(I32, (tq, LANES), 1) < HEAD_DIM
        l = l_scr[...]
        res = jnp.where(l > 0.0, acc_scr[...] / l, 0.0)
        for gg in range(4):
            o = jnp.where(low, res[gg * tq:(gg + 1) * tq], res[(4 + gg) * tq:(5 + gg) * tq])
            o_ref[0, :, gg * LANES:(gg + 1) * LANES] = o.astype(o_ref.dtype)


def _dsa_decode(qi, qa, misc, idx_t, kv_t, page_table, ki_new_t, kv_new_t, *, q_pos0, topk):
    b, tq, _ = qa.shape
    n_pages = page_table.shape[1]
    ps = PAGES_PER_STEP
    assert n_pages % ps == 0 and ps * PAGE_SIZE >= topk
    n_groups = n_pages // ps
    kc = ps * PAGE_SIZE

    def idx_spec(i):
        return pl.BlockSpec((1, IDX_DIM, PAGE_SIZE),
                            lambda bi, g, pt: (pt[bi, jnp.minimum(g, n_groups - 1) * ps + i], 0, 0))

    def kv_spec(i):
        return pl.BlockSpec((1, 2 * LANES, PAGE_SIZE),
                            lambda bi, g, pt: (pt[bi, jnp.clip(g - n_groups - 1, 0, n_groups - 1) * ps + i], 0, 0))

    per_seq = lambda r, w: pl.BlockSpec((1, r, w), lambda bi, g, pt: (bi, 0, 0))
    grid_spec = pltpu.PrefetchScalarGridSpec(
        num_scalar_prefetch=1,
        grid=(b, 2 * n_groups + 1),
        in_specs=[per_seq(tq, 512), per_seq(tq, 512), per_seq(tq, LANES)]
        + [idx_spec(i) for i in range(ps)] + [kv_spec(i) for i in range(ps)]
        + [per_seq(IDX_DIM, PAGE_SIZE), per_seq(2 * LANES, PAGE_SIZE)],
        out_specs=per_seq(tq, 512),
        scratch_shapes=[pltpu.VMEM((n_groups + 1, tq, kc), I32),
                        pltpu.VMEM((tq, LANES), I32),
                        pltpu.VMEM((tq, LANES), I32),
                        pltpu.VMEM((8 * tq, LANES), F32),
                        pltpu.VMEM((8 * tq, LANES), F32),
                        pltpu.VMEM((8 * tq, LANES), F32)],
    )
    return pl.pallas_call(
        functools.partial(_dsa_decode_kernel, n_groups=n_groups, q_pos0=q_pos0, topk=topk),
        grid_spec=grid_spec,
        out_shape=jax.ShapeDtypeStruct((b, tq, 512), BF16),
        compiler_params=_cparams(2),
        name="dsa_decode",
    )(page_table, qi, qa, misc, *([idx_t] * ps), *([kv_t] * ps), ki_new_t, kv_new_t)


def _nsa_kernel(qb_ref, misc_ref, cmp_ref, ks_ref, kw_ref, mcs_ref, o_ref,
                m_scr, acc_scr, *, tq, kc, q_pos0, n_cmp, n_chunks, topn,
                win_pos0, win_len, win_rows):
    q0 = q_pos0 + pl.program_id(1) * tq
    row = lax.broadcasted_iota(I32, (tq, 1), 0)
    qpos = q0 + row
    ncp = cmp_ref.shape[1]
    nsp = mcs_ref.shape[1]
    qs, low = _stack_group_queries(qb_ref, tq)
    q0s, q1s = qs
    misc = misc_ref[0]

    cmp_all = cmp_ref[0]
    kc128 = cmp_all[:, :LANES]
    vc128 = cmp_all[:, LANES:]
    n_idx = lax.broadcasted_iota(I32, (tq, ncp), 1)
    bias_c = jnp.where(n_idx < n_cmp,
                       jnp.where(n_idx * CMP_STRIDE + (CMP_LEN - 1) <= qpos, 0.0, MASK_VALUE), MASK_VALUE)
    half = 4 * tq
    s = jnp.concatenate([_dot_nt(q0s, kc128), _dot_nt(q1s, kc128)], axis=0)
    pn = _softmax_rows(s + jnp.concatenate([bias_c] * 8, axis=0)).astype(BF16)
    o_cmp = jnp.concatenate([_dot(pn[:half], vc128), _dot(pn[half:], vc128)], axis=0)
    i8 = _dot(pn, mcs_ref[...])
    imp = [i8[(4 * h) * tq:(4 * h + 1) * tq] + i8[(4 * h + 1) * tq:(4 * h + 2) * tq]
           + i8[(4 * h + 2) * tq:(4 * h + 3) * tq] + i8[(4 * h + 3) * tq:(4 * h + 4) * tq]
           for h in range(N_KV_B)]

    blk = lax.broadcasted_iota(I32, (tq, nsp), 1)
    cur = lax.shift_right_logical(qpos, int(math.log2(SLC_BLOCK)))
    future = blk * SLC_BLOCK > qpos
    work0 = jnp.concatenate(
        [jnp.where(future, NEG, jnp.where(blk == 0, BIG, jnp.where(blk == cur, BIG,
                                                                   jnp.where(blk == cur - 1, BIG, imp[h]))))
         for h in range(N_KV_B)], axis=0)
    axis = 0 if 2 * tq >= LANES else 1
    if axis == 0:
        work0 = work0.T
    blk_f = lax.broadcasted_iota(I32, work0.shape, axis).astype(F32)

    def pick_body(i, carry):
        work, chosen = carry
        mx = jnp.max(work, axis=axis, keepdims=True)
        first = jnp.min(jnp.where(work == mx, blk_f, 1e9), axis=axis, keepdims=True)
        hit = blk_f == first
        return jnp.where(hit, -jnp.inf, work), jnp.where(hit, 1.0, chosen)

    _, chosen = lax.fori_loop(0, topn, pick_body, (work0, jnp.zeros(work0.shape, F32)))
    sel2 = (chosen.T if axis == 0 else chosen).astype(BF16)

    nkc = jnp.minimum(lax.div(q0 + tq - 1, kc) + 1, n_chunks)
    col_idx = lax.broadcasted_iota(I32, (tq, kc), 1)
    e_row = lax.broadcasted_iota(I32, (nsp, kc), 0)
    e_col = lax.broadcasted_iota(I32, (nsp, kc), 1)
    _flash_init(m_scr, acc_scr)

    def slc_body(c, carry):
        k0 = pl.multiple_of(c * kc, kc)
        expand = jnp.where(lax.shift_right_logical(k0 + e_col, int(math.log2(SLC_BLOCK))) == e_row,
                           1.0, 0.0).astype(BF16)
        kmask = _dot(sel2, expand)
        causal = k0 + col_idx <= qpos
        kvc = ks_ref[0, pl.ds(k0, kc), :]
        biases = [jnp.where(kmask[h * tq:(h + 1) * tq] > 0.5, jnp.where(causal, 0.0, MASK_VALUE), MASK_VALUE)
                  for h in range(N_KV_B)]
        _flash_step(qs, kvc[:, :LANES], kvc[:, LANES:2 * LANES], kvc[:, 2 * LANES:], biases,
                    m_scr, acc_scr, tq)
        return carry

    lax.fori_loop(0, nkc, slc_body, 0)

    ws = jnp.clip(q0 - WINDOW - win_pos0, 0, win_rows - win_len)
    ws = pl.multiple_of(ws, 16)
    kwc = kw_ref[0, pl.ds(ws, win_len), :]
    kw128 = kwc[:, :LANES]
    vw128 = kwc[:, LANES:]
    kpos = win_pos0 + ws + lax.broadcasted_iota(I32, (tq, win_len), 1)
    dist = qpos - kpos
    bias_w = jnp.where(dist >= 0, jnp.where(dist <= WINDOW, 0.0, MASK_VALUE), MASK_VALUE)
    s = jnp.concatenate([_dot_nt(q0s, kw128), _dot_nt(q1s, kw128)], axis=0)
    pn = _softmax_rows(s + jnp.concatenate([bias_w] * 8, axis=0)).astype(BF16)
    o_win = jnp.concatenate([_dot(pn[:half], vw128), _dot(pn[half:], vw128)], axis=0)
    o_slc = _flash_result(acc_scr)

    def gate(r, h, g):
        ln = _GATE_LANE0 + r * N_HEADS_B + h * G_B + g
        return misc[:, ln:ln + 1]

    for g in range(4):
        halves = []
        for h in range(N_KV_B):
            sl = slice((4 * h + g) * tq, (4 * h + g + 1) * tq)
            halves.append(gate(0, h, g) * o_cmp[sl] + gate(1, h, g) * o_slc[sl] + gate(2, h, g) * o_win[sl])
        o_ref[0, :, g * LANES:(g + 1) * LANES] = jnp.where(low, halves[0], halves[1]).astype(o_ref.dtype)


def _slc_map(n_cmp, n_slc, ncp, nsp):
    st = np.arange(ncp) * CMP_STRIDE
    bs = np.arange(nsp) * SLC_BLOCK
    m = (st[:, None] < bs[None, :] + SLC_BLOCK) & (st[:, None] + CMP_LEN > bs[None, :])
    m &= (np.arange(ncp)[:, None] < n_cmp) & (np.arange(nsp)[None, :] < n_slc)
    return jnp.asarray(m.astype(np.float32), BF16)


def _nsa(qb, misc, cmp_kv, ks_all, kw_all, *, tq, q_pos0, n_cmp, n_slc, win_pos0, win_len, kc=KEY_CHUNK):
    b, tq_all, _ = qb.shape
    lp = ks_all.shape[1]
    ncp = cmp_kv.shape[1]
    nsp = -(-n_slc // LANES) * LANES
    assert lp % kc == 0 and ncp % LANES == 0 and ncp >= n_cmp and tq_all % tq == 0
    win_rows = kw_all.shape[1]
    mcs = _slc_map(n_cmp, n_slc, ncp, nsp)
    kern = functools.partial(_nsa_kernel, tq=tq, kc=kc, q_pos0=q_pos0, n_cmp=n_cmp,
                             n_chunks=lp // kc, topn=min(SLC_TOPN, n_slc),
                             win_pos0=win_pos0, win_len=win_len, win_rows=win_rows)
    qspec = lambda w: pl.BlockSpec((1, tq, w), lambda bi, qi_: (bi, qi_, 0))
    full = lambda n, w: pl.BlockSpec((1, n, w), lambda bi, qi_: (bi, 0, 0))
    return pl.pallas_call(
        kern,
        grid=(b, tq_all // tq),
        in_specs=[qspec(512), qspec(LANES), full(ncp, 256), full(lp, 3 * LANES), full(win_rows, 256),
                  _const_spec(mcs.shape)],
        out_specs=qspec(512),
        out_shape=jax.ShapeDtypeStruct((b, tq_all, 512), BF16),
        scratch_shapes=[pltpu.VMEM((8 * tq, LANES), F32),
                        pltpu.VMEM((8 * tq, LANES), F32)],
        compiler_params=_cparams(2),
        name="nsa",
    )(qb, misc, cmp_kv, ks_all, kw_all, mcs)


def _finish_kernel(*refs, tm, packed, t_new, tiles_per_batch):
    if packed:
        (x_ref, oa_ref, ob_ref, mg_ref, mod_ref, e1_ref, e2_ref, tpos_ref, wpa, wpb, wo, ln1g, ln1b,
         wg, wu, cw, cb, wd, ln2g, ln2b, y_o, tail_o, ext) = refs
    else:
        (x_ref, oa_ref, ob_ref, mg_ref, mod_ref, wpa, wpb, wo, ln1g, ln1b, wg, wu,
         cw, cb, wd, ln2g, ln2b, y_o, tail_o, ext) = refs
    d = D_MODEL
    mod = mod_ref[0]
    gate1, shift2, scale2, gate2 = (mod[:, i * d:(i + 1) * d] for i in range(4))
    x = x_ref[...]
    mg = mg_ref[...].astype(F32)
    m = mg[:, :d] * _dot(oa_ref[...], wpa[...]) + mg[:, d:] * _dot(ob_ref[...], wpb[...])
    x1 = _layernorm(DN_ALPHA * x + (1.0 + gate1) * _dot(m.astype(BF16), wo[...]), ln1g[...], ln1b[...])
    h2 = (x1 * (1.0 + scale2) + shift2).astype(BF16)
    ug = _dot(h2, wg[...])
    u = _dot(h2, wu[...])

    if packed:
        ext[0:8, :] = jnp.zeros((8, D_FF), F32)
    else:
        @pl.when(pl.program_id(0) % tiles_per_batch == 0)
        def _():
            ext[0:8, :] = jnp.zeros((8, D_FF), F32)
    ext[8:8 + tm, :] = ug
    prev1 = ext[7:7 + tm, :]
    prev2 = ext[6:6 + tm, :]
    if packed:
        tpos = tpos_ref[:, :1]
        prev1 = jnp.where(tpos >= 1, prev1, e1_ref[...])
        prev2 = jnp.where(tpos >= 2, prev2, e2_ref[...])
    w = cw[...]
    conv = cb[...] + w[2:3, :] * ug
    conv = conv + w[0:1, :] * prev2
    conv = conv + w[1:2, :] * prev1
    tail = ext[tm:tm + 8, :]
    ext[0:8, :] = tail
    if packed:
        tail_o[...] = ug
    else:
        tail_o[0] = tail
    act = (_gelu_tanh(conv) * u).astype(BF16)
    f = _dot(act, wd[...])
    y_o[...] = _layernorm(DN_ALPHA * x1 + (1.0 + gate2) * f, ln2g[...], ln2b[...])


def _finish(x_flat, oa, ob, mg, mod, hist, fw, *, tm, packed, t_new, tiles_per_batch):
    rows = x_flat.shape[0]
    n_tiles = rows // tm
    r_mod = mod.shape[1]
    row_spec = lambda w: pl.BlockSpec((tm, w), lambda i: (i, 0))
    in_specs = [row_spec(D_MODEL), row_spec(512), row_spec(512), row_spec(2 * D_MODEL),
                pl.BlockSpec((1, r_mod, 4 * D_MODEL), lambda i: (i // tiles_per_batch, 0, 0))]
    args = [x_flat, oa, ob, mg, mod]
    if packed:
        in_specs += [row_spec(D_FF), row_spec(D_FF), row_spec(LANES)]
        tpos = jnp.broadcast_to((jnp.arange(rows, dtype=I32) % t_new)[:, None], (rows, LANES))
        args += list(hist) + [tpos]
        tail_shape = jax.ShapeDtypeStruct((rows, D_FF), F32)
        tail_spec = row_spec(D_FF)
    else:
        nb = n_tiles // tiles_per_batch
        tail_shape = jax.ShapeDtypeStruct((nb, 8, D_FF), F32)
        tail_spec = pl.BlockSpec((1, 8, D_FF), lambda i: (i // tiles_per_batch, 0, 0))
    in_specs += [_const_spec(a.shape) for a in fw]
    args += list(fw)
    kern = functools.partial(_finish_kernel, tm=tm, packed=packed, t_new=t_new,
                             tiles_per_batch=tiles_per_batch)
    return pl.pallas_call(
        kern,
        grid=(n_tiles,),
        in_specs=in_specs,
        out_specs=[row_spec(D_MODEL), tail_spec],
        out_shape=[jax.ShapeDtypeStruct((rows, D_MODEL), F32), tail_shape],
        scratch_shapes=[pltpu.VMEM((tm + 8, D_FF), F32)],
        compiler_params=_cparams(1),
        name="finish",
    )(*args)


def kernel(x_prompt, x_sample, cache_a_kv, cache_a_idx, cache_b_cmp_kv, cache_b_slc_kv, state_b_win_kv, state_ffn_conv, page_table, c_prompt, c_sample, w_ada, b_ada, w_in, cmp_pe, cmp_w1, cmp_b1, cmp_w2, cmp_b2, w_pa, w_pb, w_o, ln1_g, ln1_b, w_ffn_gate, w_ffn_up, conv_w, conv_b, w_down, ln2_g, ln2_b):
    d = D_MODEL
    b, s, _ = x_prompt.shape
    bd, tn, _ = x_sample.shape
    n_pages = page_table.shape[1]
    past = n_pages * PAGE_SIZE
    kvw = 2 * N_KV_B * HEAD_DIM
    assert s % PROJ_ROWS == 0 and s % FINISH_ROWS == 0 and s % KEY_CHUNK == 0 and tn <= 8
    assert s >= WINDOW + Q_TILE and state_b_win_kv.shape[1] == WINDOW and tn >= CONV_W - 1

    perm = _proj_perm()
    w_perm = jnp.where(perm[None, :] >= 0, jnp.take(w_in, np.maximum(perm, 0), axis=1), 0.0).astype(BF16)
    head_rows = np.concatenate([np.r_[g * HEAD_DIM:(g + 1) * HEAD_DIM, (4 + g) * HEAD_DIM:(5 + g) * HEAD_DIM]
                                for g in range(4)])
    vec = lambda a: a.reshape(1, -1)
    fw = (w_pa[head_rows].astype(BF16), w_pb[head_rows].astype(BF16), w_o.astype(BF16),
          vec(ln1_g), vec(ln1_b), w_ffn_gate.astype(BF16), w_ffn_up.astype(BF16),
          jnp.concatenate([conv_w, jnp.zeros((8 - CONV_W, D_FF), F32)], axis=0), vec(conv_b),
          w_down.astype(BF16), vec(ln2_g), vec(ln2_b))
    cw = _compress_weights(cmp_pe, cmp_w1, cmp_b1, cmp_w2, cmp_b2)

    rows_c = -(-(b + bd) // 16) * 16
    c_all = jnp.concatenate([c_prompt, c_sample, jnp.zeros((rows_c - b - bd, d), F32)], axis=0)
    ada = _ada(c_all, w_ada.astype(BF16), b_ada)
    ada_p, ada_s = ada[:b], ada[b:b + bd]

    cos_p, sin_p = _rope_tables(jnp.arange(s))
    (qa, qi, qb, kva_bf, ki_bf, kv_cmp_bf, kv_slc_bf, kv_win_bf, misc, mg,
     kva_t, ki_t, kv_cmp_t, kv_slc_t, kv_win_t) = _project(
        x_prompt.reshape(b * s, d), ada_p[:, None, :2 * d], cos_p, sin_p, w_perm,
        PROJ_ROWS, s // PROJ_ROWS, s // PROJ_ROWS, n_batch=b)
    r3 = lambda a: a.reshape(b, s, a.shape[-1])
    oa = _dsa(r3(qi), r3(qa), r3(misc), r3(ki_bf), r3(kva_bf),
              tq=Q_TILE, q_pos0=0, topk=min(DSA_TOPK, s // 4))
    n_ch = -(-s // CMP_STRIDE)
    assert s % CMP_STRIDE == 0 and n_ch % LANES == 0
    cmp_p = _compress(kv_cmp_bf.reshape(b, n_ch, CMP_STRIDE * kvw), cw)
    ob = _nsa(r3(qb), r3(misc), cmp_p, r3(kv_slc_bf), r3(kv_win_bf),
              tq=Q_TILE, q_pos0=0, n_cmp=n_ch - 1, n_slc=-(-s // SLC_BLOCK),
              win_pos0=0, win_len=WINDOW + Q_TILE)
    y_p, tail_p = _finish(x_prompt.reshape(b * s, d), oa.reshape(b * s, 512), ob.reshape(b * s, 512), mg,
                          ada_p[:, None, 2 * d:], None, fw, tm=FINISH_ROWS, packed=False, t_new=0,
                          tiles_per_batch=s // FINISH_ROWS)

    rows_s = bd * tn
    pos_s = past + jnp.arange(tn)
    cos_s, sin_s = _rope_tables(jnp.tile(pos_s, bd))
    rep = lambda a: jnp.repeat(a, tn, axis=0)[None]
    (qa_s, qi_s, qb_s, _, _, _, _, kv_win_s_bf, misc_s, mg_s,
     kva_s, ki_s, kv_cmp_s, kv_slc_s, kv_win_s) = _project(
        x_sample.reshape(rows_s, d), rep(ada_s[:, :2 * d]), cos_s, sin_s, w_perm, rows_s, 1, 1)

    def new16(a):
        a = a.reshape(bd, tn, a.shape[-1])
        return jnp.concatenate([a, jnp.zeros((bd, 16 - tn, a.shape[-1]), F32)], axis=1)

    def q8(a):
        a = a.reshape(bd, tn, a.shape[-1])
        return jnp.concatenate([a, jnp.zeros((bd, 8 - tn, a.shape[-1]), a.dtype)], axis=1)

    pages_t = lambda c: jnp.moveaxis(c, 1, -1).reshape(c.shape[0], -1, PAGE_SIZE)
    def new_page_t(a):
        a = a.reshape(bd, tn, a.shape[-1]).transpose(0, 2, 1)
        return jnp.concatenate([a, jnp.zeros((bd, a.shape[1], PAGE_SIZE - tn), F32)], axis=2)

    l_tot = past + tn
    oa_s = _dsa_decode(q8(qi_s), q8(qa_s), q8(misc_s), pages_t(cache_a_idx), pages_t(cache_a_kv), page_table,
                       new_page_t(ki_s), new_page_t(kva_s), q_pos0=past, topk=min(DSA_TOPK, l_tot // 4))
    cmp_all = _gather_pages(pages_t(cache_b_cmp_kv), page_table, new16(kv_cmp_s), mode="chunks")
    slc_all = _gather_pages(pages_t(cache_b_slc_kv), page_table, new16(kv_slc_s), mode="ones")
    assert cmp_all.shape[1] % LANES == 0
    cmp_s = _compress(cmp_all, cw)
    win_len = WINDOW + Q_TILE
    win_all = jnp.concatenate(
        [state_b_win_kv.reshape(bd, WINDOW, kvw).astype(BF16), kv_win_s_bf.reshape(bd, tn, kvw),
         jnp.zeros((bd, win_len - WINDOW - tn, kvw), BF16)], axis=1)
    ob_s = _nsa(q8(qb_s), q8(misc_s), cmp_s, slc_all, win_all,
                tq=8, kc=DECODE_KEY_CHUNK, q_pos0=past, n_cmp=-(-l_tot // CMP_STRIDE) - 1, n_slc=-(-l_tot // SLC_BLOCK),
                win_pos0=past - WINDOW, win_len=win_len)
    hist = state_ffn_conv
    zrow = jnp.zeros((bd, 1, D_FF), F32)
    e1 = jnp.concatenate([hist[:, 1:2]] + [zrow] * (tn - 1), axis=1).reshape(rows_s, D_FF)
    e2 = jnp.concatenate([hist[:, 0:1], hist[:, 1:2]] + [zrow] * (tn - 2), axis=1).reshape(rows_s, D_FF)
    y_s, ug_s = _finish(x_sample.reshape(rows_s, d), oa_s[:, :tn].reshape(rows_s, 512),
                        ob_s[:, :tn].reshape(rows_s, 512), mg_s, rep(ada_s[:, 2 * d:]), (e1, e2), fw,
                        tm=rows_s, packed=True, t_new=tn, tiles_per_batch=1)

    kvshape = (2, N_KV_A, HEAD_DIM)
    win_s = jnp.concatenate([state_b_win_kv[:, tn:], kv_win_s.reshape((bd, tn) + kvshape)], axis=1)
    untr = lambda a: a.reshape((b,) + kvshape + (a.shape[-1],)).transpose(0, 4, 1, 2, 3)
    return (y_p.reshape(b, s, d), y_s.reshape(bd, tn, d),
            untr(kva_t), kva_s.reshape((bd, tn) + kvshape),
            ki_t.transpose(0, 2, 1), ki_s.reshape(bd, tn, IDX_DIM),
            untr(kv_cmp_t), kv_cmp_s.reshape((bd, tn) + kvshape),
            untr(kv_slc_t), kv_slc_s.reshape((bd, tn) + kvshape),
            untr(kv_win_t[:, :, s - min(WINDOW, s):]), win_s,
            tail_p[:, 8 - (CONV_W - 1):], ug_s.reshape(bd, tn, D_FF)[:, tn - (CONV_W - 1):])
```

```python
import functools
import math

import numpy as np
import jax
import jax.numpy as jnp
from jax import lax
from jax.experimental import pallas as pl
from jax.experimental.pallas import tpu as pltpu

D_MODEL = 1024
PAGE_SIZE = 128
HEAD_DIM = 64
N_HEADS_A = 8
N_KV_A = 2
N_IDX_HEADS = 8
IDX_DIM = 64
DSA_TOPK = 256
N_HEADS_B = 8
N_KV_B = 2
G_B = N_HEADS_B // N_KV_B
CMP_LEN = 32
CMP_STRIDE = 16
CMP_HID = 2 * HEAD_DIM
SLC_BLOCK = 64
SLC_TOPN = 16
WINDOW = 512
D_FF = 2816
CONV_W = 3
ROPE_THETA = 10000.0
LN_EPS = 1e-5
DEPTH = 1
DN_ALPHA = (2 * DEPTH) ** 0.25
NEG = -1e30
BIG = 1e30
MASK_VALUE = -2e30

LANES = 128
VMEM_LIMIT_BYTES = 56 * 1024 * 1024
Q_TILE = 256
KEY_CHUNK = 512
DECODE_KEY_CHUNK = 2048
PAGES_PER_STEP = 32
PROJ_ROWS = 512
FINISH_ROWS = 512

F32 = jnp.float32
BF16 = jnp.bfloat16
I32 = jnp.int32
INT_MIN = -2 ** 31

_O_QA = 0
_O_KVA = _O_QA + N_HEADS_A * HEAD_DIM
_O_QI = _O_KVA + 2 * N_KV_A * HEAD_DIM
_O_KI = _O_QI + N_IDX_HEADS * IDX_DIM
_O_WI = _O_KI + IDX_DIM
_O_QB = _O_WI + N_IDX_HEADS
_O_KVB = _O_QB + N_HEADS_B * HEAD_DIM
_O_GB = _O_KVB + 6 * N_KV_B * HEAD_DIM
_O_MGA = _O_GB + 3 * N_HEADS_B
_O_MGB = _O_MGA + D_MODEL
_N_IN = _O_MGB + D_MODEL
_GATE_LANE0 = N_IDX_HEADS


def _cparams(n_axes):
    return pltpu.CompilerParams(dimension_semantics=("arbitrary",) * n_axes,
                                vmem_limit_bytes=VMEM_LIMIT_BYTES)


def _const_spec(shape):
    nd = len(shape)
    return pl.BlockSpec(shape, lambda *a: (0,) * nd, pipeline_mode=pl.Buffered(1))


def _dot(a, b):
    return jnp.dot(a, b, preferred_element_type=F32)


def _dot_nt(a, b):
    return lax.dot_general(a, b, (((1,), (1,)), ((), ())), preferred_element_type=F32)


def _sigmoid(x):
    return 1.0 / (1.0 + jnp.exp(-x))


def _gelu_tanh(x):
    return 0.5 * x * (1.0 + jnp.tanh(math.sqrt(2.0 / math.pi) * (x + 0.044715 * (x * x * x))))


def _layernorm(x, g, b):
    mu = jnp.mean(x, axis=-1, keepdims=True)
    xc = x - mu
    var = jnp.mean(xc * xc, axis=-1, keepdims=True)
    return xc * lax.rsqrt(var + LN_EPS) * g + b


def _ada_kernel(c_ref, w_ref, b_ref, o_ref):
    c = c_ref[...]
    s = (c * _sigmoid(c)).astype(BF16)
    o_ref[...] = _dot(s, w_ref[...]) + b_ref[...]


def _ada(c_all, w_ada_bf, b_ada):
    rows = c_all.shape[0]
    n_out = w_ada_bf.shape[1]
    tn = D_MODEL
    return pl.pallas_call(
        _ada_kernel,
        grid=(n_out // tn,),
        in_specs=[pl.BlockSpec((rows, D_MODEL), lambda j: (0, 0)),
                  pl.BlockSpec((D_MODEL, tn), lambda j: (0, j)),
                  pl.BlockSpec((1, tn), lambda j: (0, j))],
        out_specs=pl.BlockSpec((rows, tn), lambda j: (0, j)),
        out_shape=jax.ShapeDtypeStruct((rows, n_out), F32),
        compiler_params=_cparams(1),
        name="ada",
    )(c_all, w_ada_bf, b_ada.reshape(1, n_out))


_P_QA, _P_QI, _P_QB = 0, 512, 1024
_P_K = 1536
_P_V = 2176
_P_MISC = 2688
_P_MG = 2816
_P_END = _P_MG + 2 * D_MODEL


def _proj_perm():
    def paired_heads(base):
        cols = []
        for g in range(4):
            cols += list(range(base + g * HEAD_DIM, base + (g + 1) * HEAD_DIM))
            cols += list(range(base + (4 + g) * HEAD_DIM, base + (5 + g) * HEAD_DIM))
        return cols
    kv = 2 * N_KV_B * HEAD_DIM
    half = N_KV_B * HEAD_DIM
    cols = []
    cols += paired_heads(_O_QA)
    cols += list(range(_O_QI, _O_QI + N_IDX_HEADS * IDX_DIM))
    cols += paired_heads(_O_QB)
    cols += list(range(_O_KVA, _O_KVA + N_KV_A * HEAD_DIM))
    cols += list(range(_O_KI, _O_KI + IDX_DIM)) * 2
    for r in range(3):
        cols += list(range(_O_KVB + r * kv, _O_KVB + r * kv + half))
    cols += list(range(_O_KVA + N_KV_A * HEAD_DIM, _O_KVA + 2 * N_KV_A * HEAD_DIM))
    for r in range(3):
        cols += list(range(_O_KVB + r * kv + half, _O_KVB + (r + 1) * kv))
    misc = list(range(_O_WI, _O_WI + N_IDX_HEADS)) + list(range(_O_GB, _O_GB + 3 * N_HEADS_B))
    cols += misc + [-1] * (LANES - len(misc))
    cols += list(range(_O_MGA, _O_MGA + 2 * D_MODEL))
    assert len(cols) == _P_END
    return np.asarray(cols, np.int32)


def _with_ones(v, low):
    return jnp.where(low, v, 1.0), jnp.where(low, 1.0, v)


def _proj_kernel(x_ref, mod_ref, cos_ref, sin_ref, w_ref,
                 qa_o, qi_o, qb_o, kvab_o, kib_o, cmpb_o, slcb_o, winb_o, misc_o, mg_o,
                 kva_o, ki_o, cmp_o, slc_o, win_o, *, transposed):
    mod = mod_ref[0]
    shift1 = mod[:, :D_MODEL]
    scale1 = mod[:, D_MODEL:]
    h = (x_ref[...] * (1.0 + scale1) + shift1).astype(BF16)
    tm = h.shape[0]
    cos = cos_ref[...]
    sin = sin_ref[...]
    lane = lax.broadcasted_iota(I32, (tm, LANES), 1)
    first_half = (lane & (HEAD_DIM - 1)) < (HEAD_DIM // 2)
    low = lane < HEAD_DIM

    def rope(y):
        partner = jnp.where(first_half, pltpu.roll(y, LANES - HEAD_DIM // 2, 1),
                            pltpu.roll(y, HEAD_DIM // 2, 1))
        return y * cos + partner * sin

    def mm(a, b):
        return _dot(h, w_ref[:, a:b])

    q_scale = HEAD_DIM ** -0.5 * math.log2(math.e)
    for off, out, scale in ((_P_QA, qa_o, q_scale), (_P_QI, qi_o, 1.0), (_P_QB, qb_o, q_scale)):
        y = mm(off, off + 4 * LANES)
        for j in range(4):
            out[:, j * LANES:(j + 1) * LANES] = (rope(y[:, j * LANES:(j + 1) * LANES]) * scale).astype(BF16)

    yk = mm(_P_K, _P_K + 5 * LANES)
    ka, ki2, kc, ks, kw = [rope(yk[:, j * LANES:(j + 1) * LANES]) for j in range(5)]
    yv = mm(_P_V, _P_V + 4 * LANES)
    va, vc, vs, vw = [yv[:, j * LANES:(j + 1) * LANES] for j in range(4)]

    for out, k, v in ((kvab_o, ka, va), (slcb_o, ks, vs)):
        v0, v1 = _with_ones(v, low)
        out[:, :LANES] = k.astype(BF16)
        out[:, LANES:2 * LANES] = v0.astype(BF16)
        out[:, 2 * LANES:] = v1.astype(BF16)
    kib_o[...] = ki2.astype(BF16)
    for out, k, v in ((cmpb_o, kc, vc), (winb_o, kw, vw)):
        out[:, :LANES] = k.astype(BF16)
        out[:, LANES:] = v.astype(BF16)

    for out, k, v in ((kva_o, ka, va), (cmp_o, kc, vc), (slc_o, ks, vs), (win_o, kw, vw)):
        if transposed:
            out[0, :LANES, :] = k.T
            out[0, LANES:, :] = v.T
        else:
            out[:, :LANES] = k
            out[:, LANES:] = v
    if transposed:
        ki_o[0] = ki2.T[:IDX_DIM, :]
    else:
        ki_o[...] = ki2[:, :IDX_DIM]

    ym = mm(_P_MISC, _P_MISC + LANES)
    wi_scale = (N_IDX_HEADS * IDX_DIM) ** -0.5
    misc_o[...] = jnp.where(lane < N_IDX_HEADS, ym * wi_scale, _sigmoid(ym))
    for j in range(2):
        a = _P_MG + j * D_MODEL
        mg_o[:, j * D_MODEL:(j + 1) * D_MODEL] = _sigmoid(mm(a, a + D_MODEL)).astype(BF16)


def _project(x_flat, mod, cos_t, sin_t, w_perm, tm, tiles_per_group, tiles_per_table, n_batch=None):
    rows = x_flat.shape[0]
    n_tiles = rows // tm
    r_mod = mod.shape[1]
    transposed = n_batch is not None

    def row_spec(width):
        return pl.BlockSpec((tm, width), lambda i: (i, 0))

    bf_defs = [(512, BF16), (512, BF16), (512, BF16),
               (384, BF16), (LANES, BF16),
               (256, BF16), (384, BF16), (256, BF16),
               (LANES, F32),
               (2 * D_MODEL, BF16)]
    f32_widths = [256, IDX_DIM, 256, 256, 256]
    out_specs = [row_spec(w) for w, _ in bf_defs]
    out_shape = [jax.ShapeDtypeStruct((rows, w), dt) for w, dt in bf_defs]
    if transposed:
        s = rows // n_batch
        tpb = s // tm
        out_specs += [pl.BlockSpec((1, w, tm), lambda i: (i // tpb, 0, i % tpb)) for w in f32_widths]
        out_shape += [jax.ShapeDtypeStruct((n_batch, w, s), F32) for w in f32_widths]
    else:
        out_specs += [row_spec(w) for w in f32_widths]
        out_shape += [jax.ShapeDtypeStruct((rows, w), F32) for w in f32_widths]
    return pl.pallas_call(
        functools.partial(_proj_kernel, transposed=transposed),
        grid=(n_tiles,),
        in_specs=[row_spec(D_MODEL),
                  pl.BlockSpec((1, r_mod, 2 * D_MODEL), lambda i: (i // tiles_per_group, 0, 0)),
                  pl.BlockSpec((tm, LANES), lambda i: (i % tiles_per_table, 0)),
                  pl.BlockSpec((tm, LANES), lambda i: (i % tiles_per_table, 0)),
                  _const_spec((D_MODEL, _P_END))],
        out_specs=out_specs,
        out_shape=out_shape,
        compiler_params=_cparams(1),
        name="proj",
    )(x_flat, mod, cos_t, sin_t, w_perm)


def _rope_tables(pos):
    half = HEAD_DIM // 2
    inv = ROPE_THETA ** (-jnp.arange(half, dtype=F32) / half)
    ang = pos.astype(F32)[:, None] * inv[None, :]
    cos = jnp.tile(jnp.cos(ang), (1, LANES // half))
    sign = np.where((np.arange(LANES) % HEAD_DIM) < half, -1.0, 1.0).astype(np.float32)
    sin = jnp.tile(jnp.sin(ang), (1, LANES // half)) * sign[None, :]
    return cos, sin


def _gather_kernel(pt_ref, *refs, n_groups, mode):
    pages = refs[:PAGES_PER_STEP]
    new_ref = refs[PAGES_PER_STEP]
    out_ref = refs[PAGES_PER_STEP + 1]
    g = pl.program_id(1)

    if mode == "chunks":
        xs = refs[PAGES_PER_STEP + 2]
        n_rows = PAGES_PER_STEP * PAGE_SIZE // CMP_STRIDE
        w = 2 * LANES

        @pl.when(g < n_groups)
        def _():
            for i in range(PAGES_PER_STEP):
                x = pages[i][0].T
                xs[0, i * PAGE_SIZE:(i + 1) * PAGE_SIZE, :] = x[:, :LANES]
                xs[1, i * PAGE_SIZE:(i + 1) * PAGE_SIZE, :] = x[:, LANES:]
            for c in range(CMP_STRIDE):
                for hf in range(2):
                    a = c * w + hf * LANES
                    out_ref[0, :, a:a + LANES] = xs[hf, pl.ds(c, n_rows, stride=CMP_STRIDE), :].astype(BF16)

        @pl.when(g == n_groups)
        def _():
            new = new_ref[0]
            first = jnp.concatenate([new[c:c + 1, :] for c in range(CMP_STRIDE)], axis=1)
            out_ref[0] = jnp.zeros(out_ref.shape[1:], BF16)
            out_ref[0, :16, :] = jnp.concatenate([first, jnp.zeros((15, CMP_STRIDE * w), F32)],
                                                 axis=0).astype(BF16)
        return

    def put(r0, rows):
        n = rows.shape[0]
        if mode == "ones":
            low = lax.broadcasted_iota(I32, (n, LANES), 1) < HEAD_DIM
            v0, v1 = _with_ones(rows[:, LANES:], low)
            out_ref[0, r0:r0 + n, :LANES] = rows[:, :LANES].astype(BF16)
            out_ref[0, r0:r0 + n, LANES:2 * LANES] = v0.astype(BF16)
            out_ref[0, r0:r0 + n, 2 * LANES:] = v1.astype(BF16)
        elif mode == "dup":
            out_ref[0, r0:r0 + n, :] = jnp.concatenate([rows, rows], axis=1).astype(BF16)
        else:
            out_ref[0, r0:r0 + n, :] = rows.astype(BF16)

    @pl.when(g < n_groups)
    def _():
        for i in range(PAGES_PER_STEP):
            put(i * PAGE_SIZE, pages[i][0].T)

    @pl.when(g == n_groups)
    def _():
        out_ref[0] = jnp.zeros(out_ref.shape[1:], BF16)
        put(0, new_ref[0])


def _gather_pages(cache_t, page_table, new_rows, mode="plain"):
    n_phys, w, _ = cache_t.shape
    b, n_pages = page_table.shape
    assert n_pages % PAGES_PER_STEP == 0
    n_groups = n_pages // PAGES_PER_STEP
    rows_step = PAGES_PER_STEP * PAGE_SIZE
    scratch = []
    if mode == "chunks":
        assert w == 2 * LANES
        rows_step, w_out = rows_step // CMP_STRIDE, CMP_STRIDE * w
        scratch = [pltpu.VMEM((2, PAGES_PER_STEP * PAGE_SIZE, LANES), F32)]
    else:
        w_out = {"plain": w, "ones": w + LANES, "dup": 2 * w}[mode]

    def page_spec(i):
        def imap(bi, g, pt):
            gg = jnp.minimum(g, n_groups - 1)
            return (pt[bi, gg * PAGES_PER_STEP + i], 0, 0)
        return pl.BlockSpec((1, w, PAGE_SIZE), imap)

    grid_spec = pltpu.PrefetchScalarGridSpec(
        num_scalar_prefetch=1,
        grid=(b, n_groups + 1),
        in_specs=[page_spec(i) for i in range(PAGES_PER_STEP)]
        + [pl.BlockSpec((1, new_rows.shape[1], w), lambda bi, g, pt: (bi, 0, 0))],
        out_specs=pl.BlockSpec((1, rows_step, w_out), lambda bi, g, pt: (bi, g, 0)),
        scratch_shapes=scratch,
    )
    return pl.pallas_call(
        functools.partial(_gather_kernel, n_groups=n_groups, mode=mode),
        grid_spec=grid_spec,
        out_shape=jax.ShapeDtypeStruct((b, (n_groups + 1) * rows_step, w_out), BF16),
        compiler_params=_cparams(2),
        name="gather_pages",
    )(page_table, *([cache_t] * PAGES_PER_STEP), new_rows)


def _cmp_kernel(x_ref, wa_ref, wb_ref, pea_ref, peb_ref, b1_ref, w2_ref, b2_ref, o_ref):
    x = x_ref[0].astype(BF16)
    r = x.shape[0]
    ha = _dot(x, wa_ref[...])
    hb = _dot(x, wb_ref[...])
    hpe = _dot(pea_ref[...], wa_ref[...]) + _dot(peb_ref[...], wb_ref[...])
    hpe = hpe[0:1, :] + b1_ref[...]
    hb_next = pltpu.roll(hb, r - 1, 0)
    hid = _gelu_tanh(ha + hb_next + hpe).astype(BF16)
    o_ref[0] = (_dot(hid, w2_ref[...]) + b2_ref[...]).astype(o_ref.dtype)


def _compress_weights(cmp_pe, cmp_w1, cmp_b1, cmp_w2, cmp_b2):
    eye = jnp.eye(2, dtype=F32)
    kvw = 2 * N_KV_B * HEAD_DIM

    def big(w1h):
        t = jnp.einsum('jcdf,jJ,hH->cjhdJHf', w1h, eye, eye)
        return t.reshape(CMP_STRIDE * kvw, 2 * N_KV_B * CMP_HID).astype(BF16)

    def pe_row(peh):
        t = jnp.broadcast_to(peh.transpose(1, 0, 2)[:, :, None, :], (CMP_STRIDE, 2, N_KV_B, HEAD_DIM))
        row = t.reshape(1, CMP_STRIDE * kvw)
        return jnp.concatenate([row, jnp.zeros((15, CMP_STRIDE * kvw), F32)], axis=0).astype(BF16)

    wa = big(cmp_w1[:, :CMP_STRIDE])
    wb = big(cmp_w1[:, CMP_STRIDE:])
    pea = pe_row(cmp_pe[:, :CMP_STRIDE])
    peb = pe_row(cmp_pe[:, CMP_STRIDE:])
    b1 = jnp.broadcast_to(cmp_b1[:, None, :], (2, N_KV_B, CMP_HID)).reshape(1, -1)
    w2 = jnp.einsum('jfd,jJ,hH->jhfJHd', cmp_w2, eye, eye).reshape(2 * N_KV_B * CMP_HID, kvw).astype(BF16)
    b2 = jnp.broadcast_to(cmp_b2[:, None, :], (2, N_KV_B, HEAD_DIM)).reshape(1, -1)
    return wa, wb, pea, peb, b1, w2, b2


def _compress(x_chunks, cw):
    b, r, cwid = x_chunks.shape
    wa, wb, pea, peb, b1, w2, b2 = cw
    return pl.pallas_call(
        _cmp_kernel,
        grid=(b,),
        in_specs=[pl.BlockSpec((1, r, cwid), lambda i: (i, 0, 0)),
                  _const_spec(wa.shape), _const_spec(wb.shape), _const_spec(pea.shape),
                  _const_spec(peb.shape), _const_spec(b1.shape), _const_spec(w2.shape),
                  _const_spec(b2.shape)],
        out_specs=pl.BlockSpec((1, r, w2.shape[1]), lambda i: (i, 0, 0)),
        out_shape=jax.ShapeDtypeStruct((b, r, w2.shape[1]), BF16),
        compiler_params=_cparams(1),
        name="compress",
    )(x_chunks, wa, wb, pea, peb, b1, w2, b2)


def _stack_group_queries(q_ref, tq):
    lane = lax.broadcasted_iota(I32, (tq, LANES), 1)
    low = lane < HEAD_DIM
    q = q_ref[0].astype(F32)
    cols = [q[:, g * LANES:(g + 1) * LANES] for g in range(4)]
    q0 = jnp.concatenate([jnp.where(low, c, 0.0) for c in cols], axis=0).astype(BF16)
    q1 = jnp.concatenate([jnp.where(low, 0.0, c) for c in cols], axis=0).astype(BF16)
    return (q0, q1), low


def _flash_init(m_scr, acc_scr):
    m_scr[...] = jnp.full(m_scr.shape, NEG, F32)
    acc_scr[...] = jnp.zeros(acc_scr.shape, F32)


def _flash_step(qs, k128, v0, v1, biases, m_scr, acc_scr, tq):
    half = 4 * tq
    kc = k128.shape[0]
    s = jnp.concatenate([_dot_nt(qs[0], k128), _dot_nt(qs[1], k128)], axis=0)
    s = s + jnp.concatenate([biases[0]] * 4 + [biases[1]] * 4, axis=0)
    m_prev = m_scr[...]
    m_new = jnp.maximum(m_prev, jnp.max(s, axis=1, keepdims=True))
    alpha = jnp.exp2(m_prev - m_new)
    p = jnp.exp2(s - jnp.concatenate([m_new] * (kc // LANES), axis=1)).astype(BF16)
    m_scr[...] = m_new
    pv = jnp.concatenate([_dot(p[:half], v0), _dot(p[half:], v1)], axis=0)
    acc_scr[...] = alpha * acc_scr[...] + pv


def _flash_result(acc_scr):
    acc = acc_scr[...]
    l = pltpu.roll(acc, HEAD_DIM, 1)
    return jnp.where(l > 0.0, acc / l, 0.0)


def _softmax_rows(s):
    m = jnp.maximum(jnp.max(s, axis=1, keepdims=True), NEG)
    p = jnp.exp2(s - m)
    l = jnp.sum(p, axis=1, keepdims=True)
    return jnp.where(l > 0.0, p / l, 0.0)


def _key_of(v):
    bits = lax.bitcast_convert_type(v, I32)
    return jnp.where(bits < 0, bits ^ 0x7FFFFFFF, bits)


def _value_of(key):
    return lax.bitcast_convert_type(jnp.where(key < 0, key ^ 0x7FFFFFFF, key), F32)


VALUE_SEARCH_ROUNDS = 26
MAX_SEARCH_ROUNDS = VALUE_SEARCH_ROUNDS + 34
ROUNDS_PER_EXIT_TEST = 3


def _dsa_kernel(qi_ref, qa_ref, misc_ref, ki_ref, kv_ref, o_ref,
                key_scr, jsel_scr, m_scr, acc_scr, *, tq, kc, q_pos0, topk, n_chunks):
    q0 = q_pos0 + pl.program_id(1) * tq
    row = lax.broadcasted_iota(I32, (tq, 1), 0)
    qpos = q0 + row
    nkc = jnp.minimum(lax.div(q0 + tq - 1, kc) + 1, n_chunks)
    lane = lax.broadcasted_iota(I32, (tq, LANES), 1)
    low = lane < IDX_DIM
    col_idx = lax.broadcasted_iota(I32, (tq, kc), 1)
    n_tiles = kc // LANES

    qi = qi_ref[0].astype(F32)
    misc = misc_ref[0]
    q_heads, w_heads = [], []
    for hh in range(N_IDX_HEADS):
        c = qi[:, (hh // 2) * LANES:(hh // 2 + 1) * LANES]
        q_heads.append(jnp.where(low, c, 0.0) if hh % 2 == 0 else jnp.where(low, 0.0, c))
        w_heads.append(misc[:, hh:hh + 1])
    q_all = jnp.concatenate(q_heads, axis=0).astype(BF16)

    def score_body(c, carry):
        vmin, vmax, n_nonneg, n_pos = carry
        k0 = pl.multiple_of(c * kc, kc)
        d = _dot_nt(q_all, ki_ref[0, pl.ds(k0, kc), :])
        acc = None
        for hh in range(N_IDX_HEADS):
            t = w_heads[hh] * jnp.maximum(d[hh * tq:(hh + 1) * tq], 0.0)
            acc = t if acc is None else acc + t
        acc = jnp.where(acc == 0.0, 0.0, acc)
        adm = k0 + col_idx <= qpos
        key_scr[c] = _key_of(jnp.where(adm, acc, NEG))
        lo_c = jnp.where(adm, acc, jnp.inf)
        hi_c = jnp.where(adm, acc, -jnp.inf)
        nn_c = jnp.where(hi_c >= 0.0, 1.0, 0.0)
        pos_c = jnp.where(hi_c > 0.0, 1.0, 0.0)
        for j in range(n_tiles):
            sl = slice(j * LANES, (j + 1) * LANES)
            vmin = jnp.minimum(vmin, lo_c[:, sl])
            vmax = jnp.maximum(vmax, hi_c[:, sl])
            n_nonneg = n_nonneg + nn_c[:, sl]
            n_pos = n_pos + pos_c[:, sl]
        return vmin, vmax, n_nonneg, n_pos

    zeros = jnp.zeros((tq, LANES), F32)
    vmin, vmax, n_nonneg, n_pos = lax.fori_loop(
        0, nkc, score_body, (jnp.full((tq, LANES), jnp.inf, F32), jnp.full((tq, LANES), -jnp.inf, F32),
                             zeros, zeros))
    vmin = jnp.min(vmin, axis=1, keepdims=True)
    vmax = jnp.max(vmax, axis=1, keepdims=True)
    n_adm = jnp.minimum(qpos + 1, n_chunks * kc).astype(F32)
    pre_counts = (n_adm, jnp.sum(n_nonneg, axis=1, keepdims=True), jnp.sum(n_pos, axis=1, keepdims=True))

    def count_ge(cands):
        def body(c, parts):
            keys = key_scr[c]
            out = []
            for cand, part in zip(cands, parts):
                hit = jnp.where(keys >= cand, 1.0, 0.0)
                for j in range(n_tiles):
                    part = part + hit[:, j * LANES:(j + 1) * LANES]
                out.append(part)
            return tuple(out)
        parts = lax.fori_loop(0, nkc, body, tuple(jnp.zeros((tq, LANES), F32) for _ in cands))
        return [jnp.sum(part, axis=1, keepdims=True) for part in parts]

    def count_eq_upto(lo, t):
        def body(c, part):
            hit = jnp.where(key_scr[c] == lo, jnp.where(c * kc + col_idx <= t, 1.0, 0.0), 0.0)
            for j in range(n_tiles):
                part = part + hit[:, j * LANES:(j + 1) * LANES]
            return part
        part = lax.fori_loop(0, nkc, body, jnp.zeros((tq, LANES), F32))
        return jnp.sum(part, axis=1, keepdims=True)

    lo, jsel = _select_threshold(count_ge, count_eq_upto, vmin, vmax, topk, tq, n_chunks * kc, jsel_scr,
                                 pre_counts)
    lo1 = lo + 1
    _dsa_attend(qa_ref, kv_ref, o_ref, key_scr, m_scr, acc_scr, lo, lo1, jsel, qpos, col_idx, nkc, tq, kc)


def _select_threshold(count_ge, count_eq_upto, vmin, vmax, topk, tq, n_pos, jsel_scr, pre_counts=None):
    kf = float(topk)
    kmin = _key_of(vmin)
    zero = jnp.zeros((tq, 1), I32)
    c_all, c_nonneg, c_pos = pre_counts if pre_counts is not None else count_ge([kmin, zero, zero + 1])
    at_zero = jnp.logical_and(c_pos < kf, c_nonneg >= kf)
    above = c_pos >= kf
    lo0 = jnp.where(at_zero, 0, jnp.where(above, 1, kmin))
    clo0 = jnp.where(at_zero, c_nonneg, jnp.where(above, c_pos, c_all))
    hi0 = jnp.where(above, _key_of(vmax) + 1, 0)
    chi0 = jnp.where(above, 0.0, c_nonneg)
    take_all = c_all <= kf
    lo0 = jnp.where(take_all, kmin, lo0)
    clo0 = jnp.where(take_all, c_all, clo0)
    todo0 = jnp.where(jnp.logical_or(jnp.logical_or(take_all, at_zero), clo0 == kf), 0.0, 1.0)

    def search_cond(carry):
        r, _, _, _, _, todo = carry
        return jnp.logical_and(r < MAX_SEARCH_ROUNDS, jnp.max(todo) > 0.5)

    def search_body(carry):
        return lax.fori_loop(0, ROUNDS_PER_EXIT_TEST, lambda _, c: search_round(c), carry)

    def search_round(carry):
        r, lo, hi, clo, chi, todo = carry
        vlo = _value_of(lo)
        vmid = vlo + (_value_of(hi) - vlo) * 0.5
        kmid = lax.shift_right_arithmetic(lo, 1) + lax.shift_right_arithmetic(hi, 1) + (lo & hi & 1)
        mid = jnp.where(r < VALUE_SEARCH_ROUNDS, _key_of(vmid), kmid)
        mid = jnp.minimum(jnp.maximum(mid, lo + 1), hi - 1)
        cnt, = count_ge([mid])
        active = todo > 0.5
        up = jnp.logical_and(active, cnt >= kf)
        down = jnp.logical_and(active, cnt < kf)
        lo = jnp.where(up, mid, lo)
        clo = jnp.where(up, cnt, clo)
        hi = jnp.where(down, mid, hi)
        chi = jnp.where(down, cnt, chi)
        done = jnp.logical_or(clo == kf, hi == lo + 1)
        return r + 1, lo, hi, clo, chi, jnp.where(done, 0.0, todo)

    _, lo, _, clo, _, _ = lax.while_loop(
        search_cond, search_body, (jnp.int32(0), lo0, hi0, clo0, chi0, todo0))
    lo1 = lo + 1
    excess = jnp.where(take_all, 0.0, clo - kf)

    jsel_scr[...] = jnp.full(jsel_scr.shape, n_pos, I32)
    n_bits = max(1, int(math.ceil(math.log2(n_pos))))

    @pl.when(jnp.max(excess) > 0.5)
    def _():
        c_above, = count_ge([lo1])
        need = kf - c_above

        def idx_body(i, jcur):
            step = lax.shift_left(jnp.int32(1), n_bits - 1 - i)
            return jnp.where(count_eq_upto(lo, jcur + step - 1) < need, jcur + step, jcur)

        jfin = lax.fori_loop(0, n_bits, idx_body, jnp.zeros((tq, 1), I32))
        jsel_scr[...] = jnp.broadcast_to(jnp.where(excess > 0.5, jfin, n_pos), jsel_scr.shape)

    return lo, jsel_scr[:, :1]


def _dsa_attend(qa_ref, kv_ref, o_ref, key_scr, m_scr, acc_scr, lo, lo1, jsel, qpos, col_idx, nkc, tq, kc):
    qs, low_q = _stack_group_queries(qa_ref, tq)
    _flash_init(m_scr, acc_scr)

    def att_body(c, carry):
        k0 = pl.multiple_of(c * kc, kc)
        kidx = k0 + col_idx
        thr = jnp.where(kidx <= jsel, lo, lo1)
        bias = jnp.where(key_scr[c] >= thr, jnp.where(kidx <= qpos, 0.0, MASK_VALUE), MASK_VALUE)
        kvc = kv_ref[0, pl.ds(k0, kc), :]
        _flash_step(qs, kvc[:, :LANES], kvc[:, LANES:2 * LANES], kvc[:, 2 * LANES:], (bias, bias),
                    m_scr, acc_scr, tq)
        return carry

    lax.fori_loop(0, nkc, att_body, 0)

    res = _flash_result(acc_scr)
    for g in range(4):
        o = jnp.where(low_q, res[g * tq:(g + 1) * tq], res[(4 + g) * tq:(5 + g) * tq])
        o_ref[0, :, g * LANES:(g + 1) * LANES] = o.astype(o_ref.dtype)


def _dsa(qi, qa, misc, ki_all, kv_all, *, tq, q_pos0, topk, kc=KEY_CHUNK):
    b, tq_all, _ = qa.shape
    lp = ki_all.shape[1]
    assert lp % kc == 0 and kc >= topk and tq_all % tq == 0
    n_chunks = lp // kc
    kern = functools.partial(_dsa_kernel, tq=tq, kc=kc, q_pos0=q_pos0, topk=topk, n_chunks=n_chunks)
    qspec = lambda w: pl.BlockSpec((1, tq, w), lambda bi, qi_: (bi, qi_, 0))
    kspec = lambda w: pl.BlockSpec((1, lp, w), lambda bi, qi_: (bi, 0, 0))
    return pl.pallas_call(
        kern,
        grid=(b, tq_all // tq),
        in_specs=[qspec(512), qspec(512), qspec(LANES), kspec(LANES), kspec(3 * LANES)],
        out_specs=qspec(512),
        out_shape=jax.ShapeDtypeStruct((b, tq_all, 512), BF16),
        scratch_shapes=[pltpu.VMEM((n_chunks, tq, kc), I32),
                        pltpu.VMEM((tq, LANES), I32),
                        pltpu.VMEM((8 * tq, LANES), F32),
                        pltpu.VMEM((8 * tq, LANES), F32)],
        compiler_params=_cparams(2),
        name="dsa",
    )(qi, qa, misc, ki_all, kv_all)


def _dsa_decode_kernel(pt_ref, qi_ref, qa_ref, misc_ref, *refs, n_groups, q_pos0, topk):
    ps = PAGES_PER_STEP
    idx_pages, kv_pages = refs[:ps], refs[ps:2 * ps]
    ki_new_ref, kv_new_ref, o_ref, key_scr, jsel_scr, lo_scr, m_scr, l_scr, acc_scr = refs[2 * ps:]
    tq = qi_ref.shape[1]
    kc = ps * PAGE_SIZE
    n_chunks = n_groups + 1
    g = pl.program_id(1)
    qpos = q_pos0 + lax.broadcasted_iota(I32, (tq, 1), 0)
    col_idx = lax.broadcasted_iota(I32, (tq, kc), 1)

    def page_scores(ki_t):
        qi = qi_ref[0].astype(F32)
        misc = misc_ref[0]
        q_all = jnp.concatenate([qi[:, hh * IDX_DIM:(hh + 1) * IDX_DIM] for hh in range(N_IDX_HEADS)],
                                axis=0).astype(BF16)
        d = _dot(q_all, ki_t.astype(BF16))
        acc = None
        for hh in range(N_IDX_HEADS):
            t = misc[:, hh:hh + 1] * jnp.maximum(d[hh * tq:(hh + 1) * tq], 0.0)
            acc = t if acc is None else acc + t
        return acc

    def keys_of(acc, k0):
        acc = jnp.where(acc == 0.0, 0.0, acc)
        return _key_of(jnp.where(k0 + col_idx <= qpos, acc, NEG))

    def attend(pages_t, keys, k0):
        n = len(pages_t)
        (q0s, q1s), _ = _stack_group_queries(qa_ref, tq)
        q_all = jnp.concatenate([q0s, q1s], axis=0)
        lo = lo_scr[:, :1]
        jsel = jsel_scr[:, :1]
        kidx = k0 + lax.broadcasted_iota(I32, keys.shape, 1)
        thr = jnp.where(kidx <= jsel, lo, lo + 1)
        bias = jnp.where(keys >= thr, jnp.where(kidx <= qpos, 0.0, MASK_VALUE), MASK_VALUE)
        kv = [p.astype(BF16) for p in pages_t]
        s = jnp.concatenate([_dot(q_all, p[:LANES]) for p in kv], axis=1)
        s = s + jnp.concatenate([bias] * 8, axis=0)
        m_prev = m_scr[...]
        m_new = jnp.maximum(m_prev, jnp.max(s, axis=1, keepdims=True))
        alpha = jnp.exp2(m_prev - m_new)
        p = jnp.exp2(s - jnp.concatenate([m_new] * n, axis=1))
        l_scr[...] = alpha * l_scr[...] + jnp.sum(p, axis=1, keepdims=True)
        m_scr[...] = m_new
        pb = p.astype(BF16)
        pv = None
        for i in range(n):
            t = _dot_nt(pb[:, i * PAGE_SIZE:(i + 1) * PAGE_SIZE], kv[i][LANES:])
            pv = t if pv is None else pv + t
        acc_scr[...] = alpha * acc_scr[...] + pv

    @pl.when(g < n_groups)
    def _():
        sc = jnp.concatenate([page_scores(idx_pages[i][0]) for i in range(ps)], axis=1)
        key_scr[g] = keys_of(sc, g * kc)

    @pl.when(g == n_groups)
    def _():
        sc = jnp.concatenate([page_scores(ki_new_ref[0]), jnp.zeros((tq, kc - PAGE_SIZE), F32)], axis=1)
        key_scr[n_groups] = keys_of(sc, n_groups * kc)

        keys = key_scr[...]
        vals = _value_of(keys)
        adm = keys > _key_of(jnp.full((1, 1, 1), NEG, F32))
        vmin = jnp.min(jnp.min(jnp.where(adm, vals, jnp.inf), axis=0), axis=1, keepdims=True)
        vmax = jnp.max(jnp.max(jnp.where(adm, vals, -jnp.inf), axis=0), axis=1, keepdims=True)
        pos = (lax.broadcasted_iota(I32, keys.shape, 0) * kc + lax.broadcasted_iota(I32, keys.shape, 2))

        def lane_total(x):
            part = x[:, :LANES]
            for j in range(1, kc // LANES):
                part = part + x[:, j * LANES:(j + 1) * LANES]
            return jnp.sum(part, axis=1, keepdims=True)

        def count_ge(cands):
            return [lane_total(jnp.sum(jnp.where(keys >= c[None], 1.0, 0.0), axis=0)) for c in cands]

        def count_eq_upto(lo, t):
            return lane_total(jnp.sum(jnp.where(keys == lo[None], jnp.where(pos <= t[None], 1.0, 0.0), 0.0),
                                      axis=0))

        lo, jsel = _select_threshold(count_ge, count_eq_upto, vmin, vmax, topk, tq, n_chunks * kc, jsel_scr)
        lo_scr[...] = jnp.broadcast_to(lo, lo_scr.shape)
        m_scr[...] = jnp.full(m_scr.shape, NEG, F32)
        l_scr[...] = jnp.zeros(l_scr.shape, F32)
        acc_scr[...] = jnp.zeros(acc_scr.shape, F32)
        attend([kv_new_ref[0]], key_scr[n_groups][:, :PAGE_SIZE], n_groups * kc)

    @pl.when(g > n_groups)
    def _():
        gi = g - n_groups - 1
        attend([kv_pages[i][0] for i in range(ps)], key_scr[gi], gi * kc)

    @pl.when(g == 2 * n_groups)
    def _():
        low = lax.broadcasted_iota(I32, (tq, LANES), 1) < HEAD_DIM
        l = l_scr[...]
        res = jnp.where(l > 0.0, acc_scr[...] / l, 0.0)
        for gg in range(4):
            o = jnp.where(low, res[gg * tq:(gg + 1) * tq], res[(4 + gg) * tq:(5 + gg) * tq])
            o_ref[0, :, gg * LANES:(gg + 1) * LANES] = o.astype(o_ref.dtype)


def _dsa_decode(qi, qa, misc, idx_t, kv_t, page_table, ki_new_t, kv_new_t, *, q_pos0, topk):
    b, tq, _ = qa.shape
    n_pages = page_table.shape[1]
    ps = PAGES_PER_STEP
    assert n_pages % ps == 0 and ps * PAGE_SIZE >= topk
    n_groups = n_pages // ps
    kc = ps * PAGE_SIZE

    def idx_spec(i):
        return pl.BlockSpec((1, IDX_DIM, PAGE_SIZE),
                            lambda bi, g, pt: (pt[bi, jnp.minimum(g, n_groups - 1) * ps + i], 0, 0))

    def kv_spec(i):
        return pl.BlockSpec((1, 2 * LANES, PAGE_SIZE),
                            lambda bi, g, pt: (pt[bi, jnp.clip(g - n_groups - 1, 0, n_groups - 1) * ps + i], 0, 0))

    per_seq = lambda r, w: pl.BlockSpec((1, r, w), lambda bi, g, pt: (bi, 0, 0))
    grid_spec = pltpu.PrefetchScalarGridSpec(
        num_scalar_prefetch=1,
        grid=(b, 2 * n_groups + 1),
        in_specs=[per_seq(tq, 512), per_seq(tq, 512), per_seq(tq, LANES)]
        + [idx_spec(i) for i in range(ps)] + [kv_spec(i) for i in range(ps)]
        + [per_seq(IDX_DIM, PAGE_SIZE), per_seq(2 * LANES, PAGE_SIZE)],
        out_specs=per_seq(tq, 512),
        scratch_shapes=[pltpu.VMEM((n_groups + 1, tq, kc), I32),
                        pltpu.VMEM((tq, LANES), I32),
                        pltpu.VMEM((tq, LANES), I32),
                        pltpu.VMEM((8 * tq, LANES), F32),
                        pltpu.VMEM((8 * tq, LANES), F32),
                        pltpu.VMEM((8 * tq, LANES), F32)],
    )
    return pl.pallas_call(
        functools.partial(_dsa_decode_kernel, n_groups=n_groups, q_pos0=q_pos0, topk=topk),
        grid_spec=grid_spec,
        out_shape=jax.ShapeDtypeStruct((b, tq, 512), BF16),
        compiler_params=_cparams(2),
        name="dsa_decode",
    )(page_table, qi, qa, misc, *([idx_t] * ps), *([kv_t] * ps), ki_new_t, kv_new_t)


def _nsa_kernel(qb_ref, misc_ref, cmp_ref, ks_ref, kw_ref, mcs_ref, o_ref,
                m_scr, acc_scr, *, tq, kc, q_pos0, n_cmp, n_chunks, topn,
                win_pos0, win_len, win_rows):
    q0 = q_pos0 + pl.program_id(1) * tq
    row = lax.broadcasted_iota(I32, (tq, 1), 0)
    qpos = q0 + row
    ncp = cmp_ref.shape[1]
    nsp = mcs_ref.shape[1]
    qs, low = _stack_group_queries(qb_ref, tq)
    q0s, q1s = qs
    misc = misc_ref[0]

    cmp_all = cmp_ref[0]
    kc128 = cmp_all[:, :LANES]
    vc128 = cmp_all[:, LANES:]
    n_idx = lax.broadcasted_iota(I32, (tq, ncp), 1)
    bias_c = jnp.where(n_idx < n_cmp,
                       jnp.where(n_idx * CMP_STRIDE + (CMP_LEN - 1) <= qpos, 0.0, MASK_VALUE), MASK_VALUE)
    half = 4 * tq
    s = jnp.concatenate([_dot_nt(q0s, kc128), _dot_nt(q1s, kc128)], axis=0)
    pn = _softmax_rows(s + jnp.concatenate([bias_c] * 8, axis=0)).astype(BF16)
    o_cmp = jnp.concatenate([_dot(pn[:half], vc128), _dot(pn[half:], vc128)], axis=0)
    i8 = _dot(pn, mcs_ref[...])
    imp = [i8[(4 * h) * tq:(4 * h + 1) * tq] + i8[(4 * h + 1) * tq:(4 * h + 2) * tq]
           + i8[(4 * h + 2) * tq:(4 * h + 3) * tq] + i8[(4 * h + 3) * tq:(4 * h + 4) * tq]
           for h in range(N_KV_B)]

    blk = lax.broadcasted_iota(I32, (tq, nsp), 1)
    cur = lax.shift_right_logical(qpos, int(math.log2(SLC_BLOCK)))
    future = blk * SLC_BLOCK > qpos
    work0 = jnp.concatenate(
        [jnp.where(future, NEG, jnp.where(blk == 0, BIG, jnp.where(blk == cur, BIG,
                                                                   jnp.where(blk == cur - 1, BIG, imp[h]))))
         for h in range(N_KV_B)], axis=0)
    axis = 0 if 2 * tq >= LANES else 1
    if axis == 0:
        work0 = work0.T
    blk_f = lax.broadcasted_iota(I32, work0.shape, axis).astype(F32)

    def pick_body(i, carry):
        work, chosen = carry
        mx = jnp.max(work, axis=axis, keepdims=True)
        first = jnp.min(jnp.where(work == mx, blk_f, 1e9), axis=axis, keepdims=True)
        hit = blk_f == first
        return jnp.where(hit, -jnp.inf, work), jnp.where(hit, 1.0, chosen)

    _, chosen = lax.fori_loop(0, topn, pick_body, (work0, jnp.zeros(work0.shape, F32)))
    sel2 = (chosen.T if axis == 0 else chosen).astype(BF16)

    nkc = jnp.minimum(lax.div(q0 + tq - 1, kc) + 1, n_chunks)
    col_idx = lax.broadcasted_iota(I32, (tq, kc), 1)
    e_row = lax.broadcasted_iota(I32, (nsp, kc), 0)
    e_col = lax.broadcasted_iota(I32, (nsp, kc), 1)
    _flash_init(m_scr, acc_scr)

    def slc_body(c, carry):
        k0 = pl.multiple_of(c * kc, kc)
        expand = jnp.where(lax.shift_right_logical(k0 + e_col, int(math.log2(SLC_BLOCK))) == e_row,
                           1.0, 0.0).astype(BF16)
        kmask = _dot(sel2, expand)
        causal = k0 + col_idx <= qpos
        kvc = ks_ref[0, pl.ds(k0, kc), :]
        biases = [jnp.where(kmask[h * tq:(h + 1) * tq] > 0.5, jnp.where(causal, 0.0, MASK_VALUE), MASK_VALUE)
                  for h in range(N_KV_B)]
        _flash_step(qs, kvc[:, :LANES], kvc[:, LANES:2 * LANES], kvc[:, 2 * LANES:], biases,
                    m_scr, acc_scr, tq)
        return carry

    lax.fori_loop(0, nkc, slc_body, 0)

    ws = jnp.clip(q0 - WINDOW - win_pos0, 0, win_rows - win_len)
    ws = pl.multiple_of(ws, 16)
    kwc = kw_ref[0, pl.ds(ws, win_len), :]
    kw128 = kwc[:, :LANES]
    vw128 = kwc[:, LANES:]
    kpos = win_pos0 + ws + lax.broadcasted_iota(I32, (tq, win_len), 1)
    dist = qpos - kpos
    bias_w = jnp.where(dist >= 0, jnp.where(dist <= WINDOW, 0.0, MASK_VALUE), MASK_VALUE)
    s = jnp.concatenate([_dot_nt(q0s, kw128), _dot_nt(q1s, kw128)], axis=0)
    pn = _softmax_rows(s + jnp.concatenate([bias_w] * 8, axis=0)).astype(BF16)
    o_win = jnp.concatenate([_dot(pn[:half], vw128), _dot(pn[half:], vw128)], axis=0)
    o_slc = _flash_result(acc_scr)

    def gate(r, h, g):
        ln = _GATE_LANE0 + r * N_HEADS_B + h * G_B + g
        return misc[:, ln:ln + 1]

    for g in range(4):
        halves = []
        for h in range(N_KV_B):
            sl = slice((4 * h + g) * tq, (4 * h + g + 1) * tq)
            halves.append(gate(0, h, g) * o_cmp[sl] + gate(1, h, g) * o_slc[sl] + gate(2, h, g) * o_win[sl])
        o_ref[0, :, g * LANES:(g + 1) * LANES] = jnp.where(low, halves[0], halves[1]).astype(o_ref.dtype)


def _slc_map(n_cmp, n_slc, ncp, nsp):
    st = np.arange(ncp) * CMP_STRIDE
    bs = np.arange(nsp) * SLC_BLOCK
    m = (st[:, None] < bs[None, :] + SLC_BLOCK) & (st[:, None] + CMP_LEN > bs[None, :])
    m &= (np.arange(ncp)[:, None] < n_cmp) & (np.arange(nsp)[None, :] < n_slc)
    return jnp.asarray(m.astype(np.float32), BF16)


def _nsa(qb, misc, cmp_kv, ks_all, kw_all, *, tq, q_pos0, n_cmp, n_slc, win_pos0, win_len, kc=KEY_CHUNK):
    b, tq_all, _ = qb.shape
    lp = ks_all.shape[1]
    ncp = cmp_kv.shape[1]
    nsp = -(-n_slc // LANES) * LANES
    assert lp % kc == 0 and ncp % LANES == 0 and ncp >= n_cmp and tq_all % tq == 0
    win_rows = kw_all.shape[1]
    mcs = _slc_map(n_cmp, n_slc, ncp, nsp)
    kern = functools.partial(_nsa_kernel, tq=tq, kc=kc, q_pos0=q_pos0, n_cmp=n_cmp,
                             n_chunks=lp // kc, topn=min(SLC_TOPN, n_slc),
                             win_pos0=win_pos0, win_len=win_len, win_rows=win_rows)
    qspec = lambda w: pl.BlockSpec((1, tq, w), lambda bi, qi_: (bi, qi_, 0))
    full = lambda n, w: pl.BlockSpec((1, n, w), lambda bi, qi_: (bi, 0, 0))
    return pl.pallas_call(
        kern,
        grid=(b, tq_all // tq),
        in_specs=[qspec(512), qspec(LANES), full(ncp, 256), full(lp, 3 * LANES), full(win_rows, 256),
                  _const_spec(mcs.shape)],
        out_specs=qspec(512),
        out_shape=jax.ShapeDtypeStruct((b, tq_all, 512), BF16),
        scratch_shapes=[pltpu.VMEM((8 * tq, LANES), F32),
                        pltpu.VMEM((8 * tq, LANES), F32)],
        compiler_params=_cparams(2),
        name="nsa",
    )(qb, misc, cmp_kv, ks_all, kw_all, mcs)


def _finish_kernel(*refs, tm, packed, t_new, tiles_per_batch):
    if packed:
        (x_ref, oa_ref, ob_ref, mg_ref, mod_ref, e1_ref, e2_ref, tpos_ref, wpa, wpb, wo, ln1g, ln1b,
         wg, wu, cw, cb, wd, ln2g, ln2b, y_o, tail_o, ext) = refs
    else:
        (x_ref, oa_ref, ob_ref, mg_ref, mod_ref, wpa, wpb, wo, ln1g, ln1b, wg, wu,
         cw, cb, wd, ln2g, ln2b, y_o, tail_o, ext) = refs
    d = D_MODEL
    mod = mod_ref[0]
    gate1, shift2, scale2, gate2 = (mod[:, i * d:(i + 1) * d] for i in range(4))
    x = x_ref[...]
    mg = mg_ref[...].astype(F32)
    m = mg[:, :d] * _dot(oa_ref[...], wpa[...]) + mg[:, d:] * _dot(ob_ref[...], wpb[...])
    x1 = _layernorm(DN_ALPHA * x + (1.0 + gate1) * _dot(m.astype(BF16), wo[...]), ln1g[...], ln1b[...])
    h2 = (x1 * (1.0 + scale2) + shift2).astype(BF16)
    ug = _dot(h2, wg[...])
    u = _dot(h2, wu[...])

    if packed:
        ext[0:8, :] = jnp.zeros((8, D_FF), F32)
    else:
        @pl.when(pl.program_id(0) % tiles_per_batch == 0)
        def _():
            ext[0:8, :] = jnp.zeros((8, D_FF), F32)
    ext[8:8 + tm, :] = ug
    prev1 = ext[7:7 + tm, :]
    prev2 = ext[6:6 + tm, :]
    if packed:
        tpos = tpos_ref[:, :1]
        prev1 = jnp.where(tpos >= 1, prev1, e1_ref[...])
        prev2 = jnp.where(tpos >= 2, prev2, e2_ref[...])
    w = cw[...]
    conv = cb[...] + w[2:3, :] * ug
    conv = conv + w[0:1, :] * prev2
    conv = conv + w[1:2, :] * prev1
    tail = ext[tm:tm + 8, :]
    ext[0:8, :] = tail
    if packed:
        tail_o[...] = ug
    else:
        tail_o[0] = tail
    act = (_gelu_tanh(conv) * u).astype(BF16)
    f = _dot(act, wd[...])
    y_o[...] = _layernorm(DN_ALPHA * x1 + (1.0 + gate2) * f, ln2g[...], ln2b[...])


def _finish(x_flat, oa, ob, mg, mod, hist, fw, *, tm, packed, t_new, tiles_per_batch):
    rows = x_flat.shape[0]
    n_tiles = rows // tm
    r_mod = mod.shape[1]
    row_spec = lambda w: pl.BlockSpec((tm, w), lambda i: (i, 0))
    in_specs = [row_spec(D_MODEL), row_spec(512), row_spec(512), row_spec(2 * D_MODEL),
                pl.BlockSpec((1, r_mod, 4 * D_MODEL), lambda i: (i // tiles_per_batch, 0, 0))]
    args = [x_flat, oa, ob, mg, mod]
    if packed:
        in_specs += [row_spec(D_FF), row_spec(D_FF), row_spec(LANES)]
        tpos = jnp.broadcast_to((jnp.arange(rows, dtype=I32) % t_new)[:, None], (rows, LANES))
        args += list(hist) + [tpos]
        tail_shape = jax.ShapeDtypeStruct((rows, D_FF), F32)
        tail_spec = row_spec(D_FF)
    else:
        nb = n_tiles // tiles_per_batch
        tail_shape = jax.ShapeDtypeStruct((nb, 8, D_FF), F32)
        tail_spec = pl.BlockSpec((1, 8, D_FF), lambda i: (i // tiles_per_batch, 0, 0))
    in_specs += [_const_spec(a.shape) for a in fw]
    args += list(fw)
    kern = functools.partial(_finish_kernel, tm=tm, packed=packed, t_new=t_new,
                             tiles_per_batch=tiles_per_batch)
    return pl.pallas_call(
        kern,
        grid=(n_tiles,),
        in_specs=in_specs,
        out_specs=[row_spec(D_MODEL), tail_spec],
        out_shape=[jax.ShapeDtypeStruct((rows, D_MODEL), F32), tail_shape],
        scratch_shapes=[pltpu.VMEM((tm + 8, D_FF), F32)],
        compiler_params=_cparams(1),
        name="finish",
    )(*args)


def kernel(x_prompt, x_sample, cache_a_kv, cache_a_idx, cache_b_cmp_kv, cache_b_slc_kv, state_b_win_kv, state_ffn_conv, page_table, c_prompt, c_sample, w_ada, b_ada, w_in, cmp_pe, cmp_w1, cmp_b1, cmp_w2, cmp_b2, w_pa, w_pb, w_o, ln1_g, ln1_b, w_ffn_gate, w_ffn_up, conv_w, conv_b, w_down, ln2_g, ln2_b):
    d = D_MODEL
    b, s, _ = x_prompt.shape
    bd, tn, _ = x_sample.shape
    n_pages = page_table.shape[1]
    past = n_pages * PAGE_SIZE
    kvw = 2 * N_KV_B * HEAD_DIM
    assert s % PROJ_ROWS == 0 and s % FINISH_ROWS == 0 and s % KEY_CHUNK == 0 and tn <= 8
    assert s >= WINDOW + Q_TILE and state_b_win_kv.shape[1] == WINDOW and tn >= CONV_W - 1

    perm = _proj_perm()
    w_perm = jnp.where(perm[None, :] >= 0, jnp.take(w_in, np.maximum(perm, 0), axis=1), 0.0).astype(BF16)
    head_rows = np.concatenate([np.r_[g * HEAD_DIM:(g + 1) * HEAD_DIM, (4 + g) * HEAD_DIM:(5 + g) * HEAD_DIM]
                                for g in range(4)])
    vec = lambda a: a.reshape(1, -1)
    fw = (w_pa[head_rows].astype(BF16), w_pb[head_rows].astype(BF16), w_o.astype(BF16),
          vec(ln1_g), vec(ln1_b), w_ffn_gate.astype(BF16), w_ffn_up.astype(BF16),
          jnp.concatenate([conv_w, jnp.zeros((8 - CONV_W, D_FF), F32)], axis=0), vec(conv_b),
          w_down.astype(BF16), vec(ln2_g), vec(ln2_b))
    cw = _compress_weights(cmp_pe, cmp_w1, cmp_b1, cmp_w2, cmp_b2)

    rows_c = -(-(b + bd) // 16) * 16
    c_all = jnp.concatenate([c_prompt, c_sample, jnp.zeros((rows_c - b - bd, d), F32)], axis=0)
    ada = _ada(c_all, w_ada.astype(BF16), b_ada)
    ada_p, ada_s = ada[:b], ada[b:b + bd]

    cos_p, sin_p = _rope_tables(jnp.arange(s))
    (qa, qi, qb, kva_bf, ki_bf, kv_cmp_bf, kv_slc_bf, kv_win_bf, misc, mg,
     kva_t, ki_t, kv_cmp_t, kv_slc_t, kv_win_t) = _project(
        x_prompt.reshape(b * s, d), ada_p[:, None, :2 * d], cos_p, sin_p, w_perm,
        PROJ_ROWS, s // PROJ_ROWS, s // PROJ_ROWS, n_batch=b)
    r3 = lambda a: a.reshape(b, s, a.shape[-1])
    oa = _dsa(r3(qi), r3(qa), r3(misc), r3(ki_bf), r3(kva_bf),
              tq=Q_TILE, q_pos0=0, topk=min(DSA_TOPK, s // 4))
    n_ch = -(-s // CMP_STRIDE)
    assert s % CMP_STRIDE == 0 and n_ch % LANES == 0
    cmp_p = _compress(kv_cmp_bf.reshape(b, n_ch, CMP_STRIDE * kvw), cw)
    ob = _nsa(r3(qb), r3(misc), cmp_p, r3(kv_slc_bf), r3(kv_win_bf),
              tq=Q_TILE, q_pos0=0, n_cmp=n_ch - 1, n_slc=-(-s // SLC_BLOCK),
              win_pos0=0, win_len=WINDOW + Q_TILE)
    y_p, tail_p = _finish(x_prompt.reshape(b * s, d), oa.reshape(b * s, 512), ob.reshape(b * s, 512), mg,
                          ada_p[:, None, 2 * d:], None, fw, tm=FINISH_ROWS, packed=False, t_new=0,
                          tiles_per_batch=s // FINISH_ROWS)

    rows_s = bd * tn
    pos_s = past + jnp.arange(tn)
    cos_s, sin_s = _rope_tables(jnp.tile(pos_s, bd))
    rep = lambda a: jnp.repeat(a, tn, axis=0)[None]
    (qa_s, qi_s, qb_s, _, _, _, _, kv_win_s_bf, misc_s, mg_s,
     kva_s, ki_s, kv_cmp_s, kv_slc_s, kv_win_s) = _project(
        x_sample.reshape(rows_s, d), rep(ada_s[:, :2 * d]), cos_s, sin_s, w_perm, rows_s, 1, 1)

    def new16(a):
        a = a.reshape(bd, tn, a.shape[-1])
        return jnp.concatenate([a, jnp.zeros((bd, 16 - tn, a.shape[-1]), F32)], axis=1)

    def q8(a):
        a = a.reshape(bd, tn, a.shape[-1])
        return jnp.concatenate([a, jnp.zeros((bd, 8 - tn, a.shape[-1]), a.dtype)], axis=1)

    pages_t = lambda c: jnp.moveaxis(c, 1, -1).reshape(c.shape[0], -1, PAGE_SIZE)
    def new_page_t(a):
        a = a.reshape(bd, tn, a.shape[-1]).transpose(0, 2, 1)
        return jnp.concatenate([a, jnp.zeros((bd, a.shape[1], PAGE_SIZE - tn), F32)], axis=2)

    l_tot = past + tn
    oa_s = _dsa_decode(q8(qi_s), q8(qa_s), q8(misc_s), pages_t(cache_a_idx), pages_t(cache_a_kv), page_table,
                       new_page_t(ki_s), new_page_t(kva_s), q_pos0=past, topk=min(DSA_TOPK, l_tot // 4))
    cmp_all = _gather_pages(pages_t(cache_b_cmp_kv), page_table, new16(kv_cmp_s), mode="chunks")
    slc_all = _gather_pages(pages_t(cache_b_slc_kv), page_table, new16(kv_slc_s), mode="ones")
    assert cmp_all.shape[1] % LANES == 0
    cmp_s = _compress(cmp_all, cw)
    win_len = WINDOW + Q_TILE
    win_all = jnp.concatenate(
        [state_b_win_kv.reshape(bd, WINDOW, kvw).astype(BF16), kv_win_s_bf.reshape(bd, tn, kvw),
         jnp.zeros((bd, win_len - WINDOW - tn, kvw), BF16)], axis=1)
    ob_s = _nsa(q8(qb_s), q8(misc_s), cmp_s, slc_all, win_all,
                tq=8, kc=DECODE_KEY_CHUNK, q_pos0=past, n_cmp=-(-l_tot // CMP_STRIDE) - 1, n_slc=-(-l_tot // SLC_BLOCK),
                win_pos0=past - WINDOW, win_len=win_len)
    hist = state_ffn_conv
    zrow = jnp.zeros((bd, 1, D_FF), F32)
    e1 = jnp.concatenate([hist[:, 1:2]] + [zrow] * (tn - 1), axis=1).reshape(rows_s, D_FF)
    e2 = jnp.concatenate([hist[:, 0:1], hist[:, 1:2]] + [zrow] * (tn - 2), axis=1).reshape(rows_s, D_FF)
    y_s, ug_s = _finish(x_sample.reshape(rows_s, d), oa_s[:, :tn].reshape(rows_s, 512),
                        ob_s[:, :tn].reshape(rows_s, 512), mg_s, rep(ada_s[:, 2 * d:]), (e1, e2), fw,
                        tm=rows_s, packed=True, t_new=tn, tiles_per_batch=1)

    kvshape = (2, N_KV_A, HEAD_DIM)
    win_s = jnp.concatenate([state_b_win_kv[:, tn:], kv_win_s.reshape((bd, tn) + kvshape)], axis=1)
    untr = lambda a: a.reshape((b,) + kvshape + (a.shape[-1],)).transpose(0, 4, 1, 2, 3)
    return (y_p.reshape(b, s, d), y_s.reshape(bd, tn, d),
            untr(kva_t), kva_s.reshape((bd, tn) + kvshape),
            ki_t.transpose(0, 2, 1), ki_s.reshape(bd, tn, IDX_DIM),
            untr(kv_cmp_t), kv_cmp_s.reshape((bd, tn) + kvshape),
            untr(kv_slc_t), kv_slc_s.reshape((bd, tn) + kvshape),
            untr(kv_win_t[:, :, s - min(WINDOW, s):]), win_s,
            tail_p[:, 8 - (CONV_W - 1):], ug_s.reshape(bd, tn, D_FF)[:, tn - (CONV_W - 1):])
```
